```python
import jax, jax.numpy as jnp
from jax import lax
import numpy as np

D_MODEL = 1024
BATCH = 2
SEQ = 8192
DEPTH = 2
DEC_BATCH = 32
DEC_SEQ = 4
PAST_LEN = 16384
PAGE_SIZE = 128

N_MIXERS = 2
N_A_LAYERS = (DEPTH + 1) // 2
N_B_LAYERS = DEPTH // 2
EPS = 1e-6
CHUNK = 128
D_SGU = 2 * D_MODEL
N_SGU_GROUPS = 8
SGU_GROUP_DIM = D_SGU // N_SGU_GROUPS
HEAD_DIM = 64
N_HEADS = D_MODEL // HEAD_DIM
N_KV_HEADS = 4
HEADS_PER_KV = N_HEADS // N_KV_HEADS
Q_DIM = N_HEADS * HEAD_DIM
KV_ROW = 2 * N_KV_HEADS * HEAD_DIM
N_BRANCH = 3
NSA_IN = Q_DIM + N_BRANCH * KV_ROW + N_BRANCH * N_HEADS
CMP_LEN = 32
CMP_STRIDE = 16
SLC_BLOCK = 64
SLC_TOP_N = 16
WINDOW = 512
Q_BLOCK = 128
D_FF = 2816
CONV_W = 3
NEG = -1e30
FORCE = 1e9

kernel_name = "hybrid_sgu_nsa_convffn_step"


def rmsnorm(x, g):
    xf = x.astype(jnp.float32)
    y = xf * lax.rsqrt(jnp.mean(xf * xf, axis=-1, keepdims=True) + EPS)
    return (y * g.astype(jnp.float32)).astype(x.dtype)


def alibi_slopes():
    h = np.arange(1, N_HEADS + 1, dtype=np.float32)
    return jnp.asarray(2.0 ** (-8.0 * h / N_HEADS), dtype=jnp.float32).reshape(N_KV_HEADS, HEADS_PER_KV)


def cmp_to_slc_overlap(n_cmp, n_slc):
    st = np.arange(n_cmp) * CMP_STRIDE
    bs = np.arange(n_slc) * SLC_BLOCK
    m = (st[:, None] < bs[None, :] + SLC_BLOCK) & (st[:, None] + CMP_LEN > bs[None, :])
    return jnp.asarray(m, dtype=jnp.float32)


def compress_rows(rows, pe, w):
    B, L = rows.shape[:2]
    n_sub = CMP_LEN // CMP_STRIDE
    n_seg = L // CMP_STRIDE
    n_cmp = n_seg - n_sub + 1
    seg = rows[:, :n_seg * CMP_STRIDE].reshape(B, n_seg, CMP_STRIDE, N_KV_HEADS, HEAD_DIM)
    out = None
    for m in range(n_sub):
        lo = m * CMP_STRIDE
        part = seg[:, m:m + n_cmp] + pe[lo:lo + CMP_STRIDE][None, None, :, None, :]
        term = jnp.einsum('bcrgd,rde->bcge', part, w[lo:lo + CMP_STRIDE])
        out = term if out is None else out + term
    pos = jnp.arange(n_cmp, dtype=jnp.int32) * CMP_STRIDE + (CMP_LEN - 1)
    return out, pos


def sgu_mixer(xn, w_in, b_in, v_norm, w_s, b_s, w_out):
    B, T, _ = xn.shape
    z = jax.nn.gelu(xn @ w_in + b_in)
    u, v = z[..., :D_SGU], z[..., D_SGU:]
    v = rmsnorm(v, v_norm)
    n_chunk = -(-T // CHUNK)
    vp = jnp.pad(v, ((0, 0), (0, n_chunk * CHUNK - T), (0, 0)))
    vp = vp.reshape(B, n_chunk, CHUNK, N_SGU_GROUPS, SGU_GROUP_DIM)
    w_causal = w_s * jnp.tril(jnp.ones((CHUNK, CHUNK), w_s.dtype))
    s = jnp.einsum('gts,bnsgc->bntgc', w_causal, vp) + b_s.T[None, None, :, :, None]
    s = s.reshape(B, n_chunk * CHUNK, D_SGU)[:, :T]
    return (u * s) @ w_out, v


def nsa_mixer(xn, q_off, past_cmp, past_slc, past_win, w_in, cmp_pe, cmp_w, w_out):
    B, T, _ = xn.shape
    dt = xn.dtype
    proj = xn @ w_in
    q = proj[..., :Q_DIM].reshape(B, T, N_KV_HEADS, HEADS_PER_KV, HEAD_DIM)
    o = Q_DIM
    kv_cmp = proj[..., o:o + KV_ROW].reshape(B, T, 2, N_KV_HEADS, HEAD_DIM)
    o += KV_ROW
    kv_slc = proj[..., o:o + KV_ROW].reshape(B, T, 2, N_KV_HEADS, HEAD_DIM)
    o += KV_ROW
    kv_win = proj[..., o:o + KV_ROW].reshape(B, T, 2, N_KV_HEADS, HEAD_DIM)
    o += KV_ROW
    gates = jax.nn.sigmoid(proj[..., o:].astype(jnp.float32)).reshape(B, T, N_BRANCH, N_KV_HEADS, HEADS_PER_KV)

    all_cmp = kv_cmp if past_cmp is None else jnp.concatenate([past_cmp, kv_cmp], axis=1)
    ck, cpos = compress_rows(all_cmp[:, :, 0], cmp_pe[0], cmp_w[0])
    cv, _ = compress_rows(all_cmp[:, :, 1], cmp_pe[1], cmp_w[1])
    L = q_off + T
    n_slc = -(-L // SLC_BLOCK)
    n_sel = min(SLC_TOP_N, n_slc)
    pad_rows = jnp.zeros((B, n_slc * SLC_BLOCK - L, 2, N_KV_HEADS, HEAD_DIM), dt)
    parts = [kv_slc, pad_rows] if past_slc is None else [past_slc, kv_slc, pad_rows]
    slc_blocks = jnp.concatenate(parts, axis=1).reshape(B, n_slc, SLC_BLOCK, 2, N_KV_HEADS, HEAD_DIM)
    overlap = cmp_to_slc_overlap(ck.shape[1], n_slc)
    all_win = kv_win if past_win is None else jnp.concatenate([past_win, kv_win], axis=1)
    win_off = q_off - (all_win.shape[1] - T)
    win_pad = jnp.concatenate([jnp.zeros((B, WINDOW, 2, N_KV_HEADS, HEAD_DIM), dt), all_win], axis=1)

    qb = Q_BLOCK if T % Q_BLOCK == 0 else T
    nb = T // qb
    scale = HEAD_DIM ** -0.5
    slopes = alibi_slopes()[None, None, :, :, None]
    blk = jnp.arange(n_slc, dtype=jnp.int32)
    sb_off = jnp.arange(SLC_BLOCK, dtype=jnp.int32)
    win_rel = jnp.arange(WINDOW + qb, dtype=jnp.int32)
    g_idx = jnp.arange(N_KV_HEADS)[None, :, None]

    def attend_block(args):
        qblk, pos = args
        qf = qblk * scale
        d_c = (pos[:, None] - cpos[None, :]).astype(jnp.float32)
        ok_c = (d_c >= 0)[None, :, None, None, :]
        s = jnp.einsum('bqghd,bcgd->bqghc', qf, ck).astype(jnp.float32) - slopes * d_c[None, :, None, None, :]
        p_c = jax.nn.softmax(jnp.where(ok_c, s, NEG), axis=-1) * ok_c
        o_c = jnp.einsum('bqghc,bcgd->bqghd', p_c.astype(dt), cv)
        imp = jnp.einsum('bqgc,cj->bqgj', p_c.sum(axis=3), overlap)
        cur = (pos // SLC_BLOCK)[:, None]
        forced = (blk[None] == 0) | (blk[None] == cur) | (blk[None] == cur - 1)
        imp = jnp.where(forced[None, :, None, :], FORCE, imp)
        imp = jnp.where((blk[None] > cur)[None, :, None, :], NEG, imp)
        _, idx = lax.top_k(imp, n_sel)
        kv_sel = jax.vmap(lambda kb, ix: kb[ix, :, :, g_idx, :])(slc_blocks, idx)
        d_s = (pos[None, :, None, None, None] - (idx[..., None] * SLC_BLOCK + sb_off)).astype(jnp.float32)
        d_s = d_s[:, :, :, None]
        s = jnp.einsum('bqghd,bqgnkd->bqghnk', qf, kv_sel[..., 0, :]).astype(jnp.float32) - slopes[..., None] * d_s
        s = jnp.where(d_s >= 0, s, NEG)
        p_s = jax.nn.softmax(s.reshape(s.shape[:4] + (-1,)), axis=-1).reshape(s.shape)
        o_s = jnp.einsum('bqghnk,bqgnkd->bqghd', p_s.astype(dt), kv_sel[..., 1, :])
        start = pos[0] - win_off
        kv_w = lax.dynamic_slice_in_dim(win_pad, start, WINDOW + qb, axis=1)
        kpos = pos[0] - WINDOW + win_rel
        d_w = pos[:, None] - kpos[None, :]
        ok_w = ((d_w >= 0) & (d_w < WINDOW) & (kpos[None, :] >= win_off))[None, :, None, None, :]
        s = jnp.einsum('bqghd,bkgd->bqghk', qf, kv_w[:, :, 0]).astype(jnp.float32) - slopes * d_w.astype(jnp.float32)[None, :, None, None, :]
        p_w = jax.nn.softmax(jnp.where(ok_w, s, NEG), axis=-1)
        o_w = jnp.einsum('bqghk,bkgd->bqghd', p_w.astype(dt), kv_w[:, :, 1])
        return o_c, o_s, o_w

    q_pos = jnp.arange(T, dtype=jnp.int32) + q_off
    q_blocks = q.reshape(B, nb, qb, N_KV_HEADS, HEADS_PER_KV, HEAD_DIM).swapaxes(0, 1)
    o_c, o_s, o_w = lax.map(attend_block, (q_blocks, q_pos.reshape(nb, qb)))

    def unblock(a):
        return a.swapaxes(0, 1).reshape(B, T, N_KV_HEADS, HEADS_PER_KV, HEAD_DIM)

    out = (gates[:, :, 0][..., None] * unblock(o_c) + gates[:, :, 1][..., None] * unblock(o_s)
           + gates[:, :, 2][..., None] * unblock(o_w)).astype(dt)
    y = out.reshape(B, T, D_MODEL) @ w_out
    win_buf = all_win[:, -min(WINDOW, all_win.shape[1]):]
    return y, kv_cmp, kv_slc, win_buf


def conv_ffn(xn, w_in, conv_w, conv_b, w_out, past):
    B, T, _ = xn.shape
    hg = xn @ w_in
    h, g = hg[..., :D_FF], hg[..., D_FF:]
    if past is None:
        past = jnp.zeros((B, CONV_W - 1, D_FF), h.dtype)
    hp = jnp.concatenate([past, h], axis=1)
    hc = conv_b + conv_w[0] * hp[:, 0:T]
    for k in range(1, CONV_W):
        hc = hc + conv_w[k] * hp[:, k:k + T]
    y = (jax.nn.gelu(hc) * g) @ w_out
    return y, hp[:, -(CONV_W - 1):]


def run_trunk(x, q_off, past_cmp, past_slc, past_win, past_conv, p):
    new_cmp, new_slc, new_win, new_conv, new_v = [], [], [], [], []
    for layer in range(DEPTH):
        xn = rmsnorm(x, p['norm_mix'][layer])
        if layer % N_MIXERS == 0:
            a = layer // N_MIXERS
            y, v = sgu_mixer(xn, p['sgu_w_in'][a], p['sgu_b_in'][a], p['sgu_v_norm'][a],
                             p['sgu_w_s'][a], p['sgu_b_s'][a], p['sgu_w_out'][a])
            new_v.append(v)
        else:
            b = layer // N_MIXERS
            y, kc, ks, kw = nsa_mixer(xn, q_off,
                                      None if past_cmp is None else past_cmp[b],
                                      None if past_slc is None else past_slc[b],
                                      None if past_win is None else past_win[b],
                                      p['nsa_w_in'][b], p['nsa_cmp_pe'][b], p['nsa_cmp_w'][b], p['nsa_w_out'][b])
            new_cmp.append(kc)
            new_slc.append(ks)
            new_win.append(kw)
        x = x + y
        xn = rmsnorm(x, p['norm_ffn'][layer])
        y, conv = conv_ffn(xn, p['ffn_w_in'][layer], p['ffn_conv_w'][layer], p['ffn_conv_b'][layer],
                           p['ffn_w_out'][layer], None if past_conv is None else past_conv[layer])
        new_conv.append(conv)
        x = x + y
    x = rmsnorm(x, p['norm_final'])
    return (x, jnp.stack(new_cmp), jnp.stack(new_slc), jnp.stack(new_win),
            jnp.stack(new_conv), jnp.stack(new_v))


def setup_inputs(seed: int = 0) -> dict:
    key = jax.random.key(seed)
    ks = jax.random.split(key, 24)
    f32 = jnp.float32
    n_pages = PAST_LEN // PAGE_SIZE
    n_used = DEC_BATCH * n_pages
    n_pool = n_used + n_used // 4
    win_buf = min(WINDOW, PAST_LEN)

    def nrm(k, shape, scale):
        return jax.random.normal(k, shape, f32) * scale

    page_table = jax.random.permutation(ks[6], n_pool)[:n_used].reshape(DEC_BATCH, n_pages).astype(jnp.int32)
    return {
        'x_prompt': nrm(ks[0], (BATCH, SEQ, D_MODEL), 1.0),
        'x_sample': nrm(ks[1], (DEC_BATCH, DEC_SEQ, D_MODEL), 1.0),
        'cache_cmp_kv': nrm(ks[2], (N_B_LAYERS, n_pool, PAGE_SIZE, 2, N_KV_HEADS, HEAD_DIM), 1.0),
        'cache_slc_kv': nrm(ks[3], (N_B_LAYERS, n_pool, PAGE_SIZE, 2, N_KV_HEADS, HEAD_DIM), 1.0),
        'state_win_kv': nrm(ks[4], (N_B_LAYERS, DEC_BATCH, win_buf, 2, N_KV_HEADS, HEAD_DIM), 1.0),
        'state_ffn_conv': nrm(ks[5], (DEPTH, DEC_BATCH, CONV_W - 1, D_FF), 1.0),
        'page_table': page_table,
        'norm_mix': 1.0 + nrm(ks[7], (DEPTH, D_MODEL), 0.02),
        'norm_ffn': 1.0 + nrm(ks[8], (DEPTH, D_MODEL), 0.02),
        'norm_final': 1.0 + nrm(ks[9], (D_MODEL,), 0.02),
        'sgu_w_in': nrm(ks[10], (N_A_LAYERS, D_MODEL, 2 * D_SGU), D_MODEL ** -0.5),
        'sgu_b_in': nrm(ks[11], (N_A_LAYERS, 2 * D_SGU), 0.02),
        'sgu_v_norm': 1.0 + nrm(ks[12], (N_A_LAYERS, D_SGU), 0.02),
        'sgu_w_s': nrm(ks[13], (N_A_LAYERS, N_SGU_GROUPS, CHUNK, CHUNK), 0.5 * CHUNK ** -0.5),
        'sgu_b_s': 1.0 + nrm(ks[14], (N_A_LAYERS, N_SGU_GROUPS, CHUNK), 0.1),
        'sgu_w_out': nrm(ks[15], (N_A_LAYERS, D_SGU, D_MODEL), D_SGU ** -0.5),
        'nsa_w_in': nrm(ks[16], (N_B_LAYERS, D_MODEL, NSA_IN), D_MODEL ** -0.5),
        'nsa_cmp_pe': nrm(ks[17], (N_B_LAYERS, 2, CMP_LEN, HEAD_DIM), 0.5),
        'nsa_cmp_w': nrm(ks[18], (N_B_LAYERS, 2, CMP_LEN, HEAD_DIM, HEAD_DIM), (CMP_LEN * HEAD_DIM) ** -0.5),
        'nsa_w_out': nrm(ks[19], (N_B_LAYERS, D_MODEL, D_MODEL), D_MODEL ** -0.5),
        'ffn_w_in': nrm(ks[20], (DEPTH, D_MODEL, 2 * D_FF), D_MODEL ** -0.5),
        'ffn_conv_w': nrm(ks[21], (DEPTH, CONV_W, D_FF), CONV_W ** -0.5),
        'ffn_conv_b': nrm(ks[22], (DEPTH, D_FF), 0.02),
        'ffn_w_out': nrm(ks[23], (DEPTH, D_FF, D_MODEL), D_FF ** -0.5),
    }


def reference(x_prompt, x_sample, cache_cmp_kv, cache_slc_kv, state_win_kv, state_ffn_conv, page_table,
              norm_mix, norm_ffn, norm_final, sgu_w_in, sgu_b_in, sgu_v_norm, sgu_w_s, sgu_b_s, sgu_w_out,
              nsa_w_in, nsa_cmp_pe, nsa_cmp_w, nsa_w_out, ffn_w_in, ffn_conv_w, ffn_conv_b, ffn_w_out):
    p = {'norm_mix': norm_mix, 'norm_ffn': norm_ffn, 'norm_final': norm_final,
         'sgu_w_in': sgu_w_in, 'sgu_b_in': sgu_b_in, 'sgu_v_norm': sgu_v_norm, 'sgu_w_s': sgu_w_s,
         'sgu_b_s': sgu_b_s, 'sgu_w_out': sgu_w_out, 'nsa_w_in': nsa_w_in, 'nsa_cmp_pe': nsa_cmp_pe,
         'nsa_cmp_w': nsa_cmp_w, 'nsa_w_out': nsa_w_out, 'ffn_w_in': ffn_w_in, 'ffn_conv_w': ffn_conv_w,
         'ffn_conv_b': ffn_conv_b, 'ffn_w_out': ffn_w_out}
    y_prompt, prompt_cmp_kv, prompt_slc_kv, prompt_win_kv, prompt_ffn_conv, _ = run_trunk(
        x_prompt, 0, None, None, None, None, p)
    past_cmp = [cache_cmp_kv[b, page_table].reshape(DEC_BATCH, -1, 2, N_KV_HEADS, HEAD_DIM)
                for b in range(N_B_LAYERS)]
    past_slc = [cache_slc_kv[b, page_table].reshape(DEC_BATCH, -1, 2, N_KV_HEADS, HEAD_DIM)
                for b in range(N_B_LAYERS)]
    y_sample, sample_cmp_kv, sample_slc_kv, sample_win_kv, sample_ffn_conv, sample_sgu_v = run_trunk(
        x_sample, PAST_LEN, past_cmp, past_slc, state_win_kv, state_ffn_conv, p)
    return (y_prompt, y_sample, prompt_cmp_kv, prompt_slc_kv, prompt_win_kv, prompt_ffn_conv,
            sample_cmp_kv, sample_slc_kv, sample_win_kv, sample_ffn_conv, sample_sgu_v)
```

```python
import functools

import numpy as np
import jax
import jax.numpy as jnp
from jax import lax
from jax.experimental import pallas as pl
from jax.experimental.pallas import tpu as pltpu

F32 = jnp.float32
BF16 = jnp.bfloat16

EPS = 1e-6
NEG = -1e30
FORCE = 1e9
BELOW_NEG = -3e38

CHUNK = 128
N_SGU_GROUPS = 8
HEAD_DIM = 64
N_KV_HEADS = 4
HEADS_PER_KV = 4
N_HEADS = N_KV_HEADS * HEADS_PER_KV
N_BRANCH = 3
CMP_LEN = 32
CMP_STRIDE = 16
SLC_BLOCK = 64
SLC_SHIFT = SLC_BLOCK.bit_length() - 1
SLC_TOP_N = 16
WINDOW = 512
CONV_W = 3
PAGE = 128
KV_ROW = 2 * N_KV_HEADS * HEAD_DIM
SEG_W = CMP_STRIDE * KV_ROW

LANES = 128
SUBLANES = 8
BF16_SUBLANES = 16
SLOT = LANES
V7X_VMEM_BYTES = 64 * 1024 * 1024
VMEM_LIMIT = (V7X_VMEM_BYTES * 3) // 4


def _cparams(sem):
    return pltpu.CompilerParams(dimension_semantics=sem, vmem_limit_bytes=VMEM_LIMIT)


def _const_spec(shape):
    nd = len(shape)
    return pl.BlockSpec(shape, lambda *_: (0,) * nd)


def _rms(x, g):
    return x * lax.rsqrt(jnp.mean(x * x, axis=-1, keepdims=True) + EPS) * g


def _dot(a, b):
    return jnp.dot(a, b, preferred_element_type=F32)


def _dot_nt(a, b):
    return lax.dot_general(a, b, (((1,), (1,)), ((), ())), preferred_element_type=F32)


def _proj_body(x_ref, g_ref, w_ref, b_ref, *o_refs, segs, chunk, has_bias):
    xn = _rms(x_ref[...], g_ref[...]).astype(BF16)
    off = 0
    for (width, _, act), o_ref in zip(segs, o_refs):
        for c0 in range(0, width, chunk):
            cw = min(chunk, width - c0)
            r = _dot(xn, w_ref[:, off + c0:off + c0 + cw])
            if has_bias:
                r = r + b_ref[:, off + c0:off + c0 + cw]
            if act:
                r = jax.nn.gelu(r)
            o_ref[:, c0:c0 + cw] = r.astype(o_ref.dtype)
        off += width


def _proj(x, gamma, w, bias, segs, tm, name):
    m, d = x.shape
    n = w.shape[1]
    assert sum(s[0] for s in segs) == n and m % tm == 0
    has_bias = bias is not None
    if bias is None:
        bias = jnp.zeros((1, LANES), F32)
    body = functools.partial(_proj_body, segs=tuple(segs), chunk=512, has_bias=has_bias)
    return pl.pallas_call(
        body,
        grid=(m // tm,),
        in_specs=[pl.BlockSpec((tm, d), lambda i: (i, 0)),
                  _const_spec((1, d)), _const_spec(w.shape), _const_spec(bias.shape)],
        out_specs=[pl.BlockSpec((tm, s[0]), lambda i: (i, 0)) for s in segs],
        out_shape=[jax.ShapeDtypeStruct((m, s[0]), s[1]) for s in segs],
        compiler_params=_cparams(("parallel",)),
        name=name,
    )(x, gamma.reshape(1, d), w, bias)


def _sgu_body(x_ref, u_ref, v_ref, vg_ref, ws_ref, bs_ref, wo_ref, *rest, emit_v):
    if emit_v:
        o_ref, vn_ref, vn_scr, a_scr = rest
    else:
        o_ref, vn_scr, a_scr = rest
    tm = x_ref.shape[0]
    gw = v_ref.shape[1] // N_SGU_GROUPS
    vn = _rms(v_ref[...].astype(F32), vg_ref[...])
    if emit_v:
        vn_ref[...] = vn
    vn_scr[...] = vn.astype(BF16)
    row = lax.broadcasted_iota(jnp.int32, (CHUNK, CHUNK), 0)
    col = lax.broadcasted_iota(jnp.int32, (CHUNK, CHUNK), 1)
    for g in range(N_SGU_GROUPS):
        wc = jnp.where(col <= row, ws_ref[g], 0.0).astype(BF16)
        for c in range(tm // CHUNK):
            rs = slice(c * CHUNK, (c + 1) * CHUNK)
            cs = slice(g * gw, (g + 1) * gw)
            s = _dot(wc, vn_scr[rs, cs]) + bs_ref[:, cs]
            a_scr[rs, cs] = (u_ref[rs, cs].astype(F32) * s).astype(BF16)
    o_ref[...] = x_ref[...] + _dot(a_scr[...], wo_ref[...])


def _sgu(x, u, v, v_norm, w_s, bs_exp, w_out, tm, emit_v, name):
    m, d = x.shape
    ds = u.shape[1]
    out_shape = [jax.ShapeDtypeStruct((m, d), F32)]
    out_specs = [pl.BlockSpec((tm, d), lambda i: (i, 0))]
    if emit_v:
        out_shape.append(jax.ShapeDtypeStruct((m, ds), F32))
        out_specs.append(pl.BlockSpec((tm, ds), lambda i: (i, 0)))
    return pl.pallas_call(
        functools.partial(_sgu_body, emit_v=emit_v),
        grid=(m // tm,),
        in_specs=[pl.BlockSpec((tm, d), lambda i: (i, 0)),
                  pl.BlockSpec((tm, ds), lambda i: (i, 0)),
                  pl.BlockSpec((tm, ds), lambda i: (i, 0)),
                  _const_spec((1, ds)), _const_spec(w_s.shape), _const_spec(bs_exp.shape),
                  _const_spec(w_out.shape)],
        out_specs=out_specs,
        out_shape=out_shape,
        scratch_shapes=[pltpu.VMEM((tm, ds), BF16), pltpu.VMEM((tm, ds), BF16)],
        compiler_params=_cparams(("parallel",)),
        name=name,
    )(x, u, v, v_norm.reshape(1, ds), w_s, bs_exp, w_out)


def _ffn_body(*refs, sample, final_norm, period, tiles_per_seq, n_f):
    refs = list(refs)
    x_ref, g_ref, wh_ref, wg_ref, cw_ref, cb_ref, wo_ref = refs[:7]
    refs = refs[7:]
    if sample:
        p1_ref, p2_ref = refs[:2]
        refs = refs[2:]
    if final_norm:
        gf_ref = refs[0]
        refs = refs[1:]
    o_ref, h_ref, xn_scr, acc_scr = refs[:4]
    i = pl.program_id(0)
    f = pl.program_id(1)
    tm = x_ref.shape[0]

    @pl.when(f == 0)
    def _():
        xn_scr[...] = _rms(x_ref[...], g_ref[...]).astype(BF16)
        acc_scr[...] = jnp.zeros_like(acc_scr)

    xn = xn_scr[...]
    h = _dot(xn, wh_ref[...])
    gate = _dot(xn, wg_ref[...])
    r = lax.broadcasted_iota(jnp.int32, (tm, 1), 0)
    if sample:
        r = r & (period - 1)
        prev1 = p1_ref[...]
        prev2 = p2_ref[...]
        h_ref[...] = h
    else:
        carry_scr = refs[4]

        @pl.when(i % tiles_per_seq == 0)
        def _():
            carry_scr[f] = jnp.zeros(carry_scr.shape[1:], F32)

        prev = carry_scr[f]
        c0 = prev[SUBLANES - 2:SUBLANES - 1, :]
        c1 = prev[SUBLANES - 1:SUBLANES, :]
        prev1 = c1
        prev2 = jnp.where(r == 1, c1, c0)
        tail = h[tm - SUBLANES:tm, :]
        carry_scr[f] = tail
        h_ref[...] = tail
    hm1 = jnp.where(r >= 1, pltpu.roll(h, 1, 0), prev1)
    hm2 = jnp.where(r >= 2, pltpu.roll(h, 2, 0), prev2)
    hc = cb_ref[...] + cw_ref[0:1, :] * hm2
    hc = hc + cw_ref[1:2, :] * hm1
    hc = hc + cw_ref[2:3, :] * h
    act = (jax.nn.gelu(hc) * gate).astype(BF16)
    acc_scr[...] += _dot(act, wo_ref[...])

    @pl.when(f == n_f - 1)
    def _():
        y = x_ref[...] + acc_scr[...]
        if final_norm:
            y = _rms(y, gf_ref[...])
        o_ref[...] = y


def _ffn(x, gamma, w_in, conv_w, conv_b, w_out, seq_len, past, gamma_final, tm, tf, name):
    m, d = x.shape
    d_ff = w_out.shape[0]
    assert m % tm == 0 and d_ff % tf == 0
    n_f = d_ff // tf
    sample = past is not None
    final_norm = gamma_final is not None
    if sample:
        assert tm == m and (seq_len & (seq_len - 1)) == 0
    else:
        assert seq_len % tm == 0
    in_specs = [pl.BlockSpec((tm, d), lambda i, f: (i, 0)),
                _const_spec((1, d)),
                pl.BlockSpec((d, tf), lambda i, f: (0, f)),
                pl.BlockSpec((d, tf), lambda i, f: (0, f + n_f)),
                pl.BlockSpec((CONV_W, tf), lambda i, f: (0, f)),
                pl.BlockSpec((1, tf), lambda i, f: (0, f)),
                pl.BlockSpec((tf, d), lambda i, f: (f, 0))]
    args = [x, gamma.reshape(1, d), w_in, w_in, conv_w, conv_b.reshape(1, d_ff), w_out]
    if sample:
        in_specs += [pl.BlockSpec((tm, tf), lambda i, f: (i, f))] * 2
        args += list(past)
    if final_norm:
        in_specs.append(_const_spec((1, d)))
        args.append(gamma_final.reshape(1, d))
    scratch = [pltpu.VMEM((tm, d), BF16), pltpu.VMEM((tm, d), F32)]
    if sample:
        h_shape = jax.ShapeDtypeStruct((m, d_ff), F32)
        h_spec = pl.BlockSpec((tm, tf), lambda i, f: (i, f))
    else:
        h_shape = jax.ShapeDtypeStruct((m // tm, SUBLANES, d_ff), F32)
        h_spec = pl.BlockSpec((None, SUBLANES, tf), lambda i, f: (i, 0, f))
        scratch.append(pltpu.VMEM((n_f, SUBLANES, tf), F32))
    body = functools.partial(_ffn_body, sample=sample, final_norm=final_norm, period=seq_len,
                             tiles_per_seq=max(seq_len // tm, 1), n_f=n_f)
    return pl.pallas_call(
        body,
        grid=(m // tm, n_f),
        in_specs=in_specs,
        out_specs=[pl.BlockSpec((tm, d), lambda i, f: (i, 0)), h_spec],
        out_shape=[jax.ShapeDtypeStruct((m, d), F32), h_shape],
        scratch_shapes=scratch,
        compiler_params=_cparams(("arbitrary", "arbitrary")),
        name=name,
    )(*args)


def _compress_body(pt_ref, *refs, n_pages):
    page_refs = refs[:n_pages + 1]
    pe_ref, w_ref, ck_ref, cv_ref, xs = refs[n_pages + 1:]
    n_main = n_pages * SUBLANES
    for k in range(n_pages + 1):
        xs[k * SUBLANES:(k + 1) * SUBLANES, :] = page_refs[k][...]
    xs[n_main + SUBLANES:n_main + 2 * SUBLANES, :] = pe_ref[...]
    half = N_KV_HEADS * HEAD_DIM
    for kv, o_ref in ((0, ck_ref), (1, cv_ref)):
        acc = None
        for r in range(CMP_STRIDE):
            c0 = r * KV_ROW + kv * half
            t = _dot(xs[:, c0:c0 + half].astype(BF16), w_ref[r, kv])
            acc = t if acc is None else acc + t
        n_tot = acc.shape[0]
        first = acc[:, :half] + acc[n_main + SUBLANES:n_main + SUBLANES + 1, :half]
        second = acc[:, half:] + acc[n_main + SUBLANES + 1:n_main + SUBLANES + 2, half:]
        second = pltpu.roll(second, n_tot - 1, 0)
        o_ref[...] = (first + second)[:n_main, :]


def _compress(pages, page_table, pe_seg, w_big, n_pages_step, name):
    nb, n_p = page_table.shape
    assert n_p % n_pages_step == 0
    n_steps = n_p // n_pages_step
    half = N_KV_HEADS * HEAD_DIM
    n_main = n_pages_step * SUBLANES

    def page_map(k):
        def index(b, s, pt):
            return (pt[b * n_p + jnp.minimum(s * n_pages_step + k, n_p - 1)], 0, 0)
        return index

    in_specs = [pl.BlockSpec((None, SUBLANES, SEG_W), page_map(k)) for k in range(n_pages_step + 1)]
    in_specs += [pl.BlockSpec(pe_seg.shape, lambda b, s, pt: (0, 0)),
                 pl.BlockSpec(w_big.shape, lambda b, s, pt: (0, 0, 0, 0))]
    out_spec = pl.BlockSpec((None, n_main, half), lambda b, s, pt: (b, s, 0))
    grid_spec = pltpu.PrefetchScalarGridSpec(
        num_scalar_prefetch=1, grid=(nb, n_steps), in_specs=in_specs,
        out_specs=[out_spec, out_spec],
        scratch_shapes=[pltpu.VMEM((n_main + 2 * SUBLANES, SEG_W), F32)])
    return pl.pallas_call(
        functools.partial(_compress_body, n_pages=n_pages_step),
        grid_spec=grid_spec,
        out_shape=[jax.ShapeDtypeStruct((nb, n_p * SUBLANES, half), F32)] * 2,
        compiler_params=_cparams(("parallel", "parallel")),
        name=name,
    )(page_table.reshape(-1), *([pages] * (n_pages_step + 1)), pe_seg, w_big)


def _cmp_select_body(q_ref, ck_ref, cv_ref, ov_ref, slope_ref, oc_ref, sel_ref, *, q_off, n_sel):
    tq = q_ref.shape[0]
    rows = HEADS_PER_KV * tq
    nc = ck_ref.shape[0]
    ns = ov_ref.shape[1]
    scale = HEAD_DIM ** -0.5
    base = q_off + pl.program_id(1) * tq
    tok = lax.broadcasted_iota(jnp.int32, (rows, 1), 0) & (tq - 1)
    pos = base + tok
    cpos = lax.broadcasted_iota(jnp.int32, (1, nc), 1) * CMP_STRIDE + (CMP_LEN - 1)
    d_c = (pos - cpos).astype(F32)
    ok = d_c >= 0.0
    imps = []
    for g in range(N_KV_HEADS):
        qg = jnp.concatenate(
            [q_ref[:, (g * HEADS_PER_KV + h) * SLOT:(g * HEADS_PER_KV + h + 1) * SLOT]
             for h in range(HEADS_PER_KV)], axis=0)
        s = _dot_nt(qg, ck_ref[:, g * SLOT:(g + 1) * SLOT]) * scale - slope_ref[g] * d_c
        s = jnp.where(ok, s, NEG)
        e = jnp.where(ok, jnp.exp(s - jnp.max(s, axis=-1, keepdims=True)), 0.0)
        l = jnp.sum(e, axis=-1, keepdims=True)
        p = e * jnp.where(l > 0.0, 1.0 / l, 0.0)
        o = _dot(p.astype(BF16), cv_ref[:, g * SLOT:(g + 1) * SLOT])
        psum = p[0:tq]
        for h in range(1, HEADS_PER_KV):
            psum = psum + p[h * tq:(h + 1) * tq]
        for h in range(HEADS_PER_KV):
            c0 = (g * HEADS_PER_KV + h) * SLOT
            oc_ref[:, c0:c0 + SLOT] = o[h * tq:(h + 1) * tq].astype(oc_ref.dtype)
        hi = psum.astype(BF16)
        rem = psum - hi.astype(F32)
        mid = rem.astype(BF16)
        lo = (rem - mid.astype(F32)).astype(BF16)
        ov = ov_ref[...]
        imps.append(_dot(hi, ov) + _dot(mid, ov) + _dot(lo, ov))
    imp = jnp.concatenate(imps, axis=0)
    blk = lax.broadcasted_iota(jnp.int32, (1, ns), 1)
    cur = pos >> SLC_SHIFT
    forced = (blk == 0) | (blk == cur) | (blk == cur - 1)
    imp = jnp.where(forced, FORCE, imp)
    imp = jnp.where(blk > cur, NEG, imp)
    blk_f = blk.astype(F32)
    sel = jnp.zeros(imp.shape, F32)
    for _ in range(n_sel):
        top = jnp.max(imp, axis=-1, keepdims=True)
        first = jnp.min(jnp.where(imp == top, blk_f, float(ns)), axis=-1, keepdims=True)
        hit = blk_f == first
        sel = jnp.where(hit, 1.0, sel)
        imp = jnp.where(hit, BELOW_NEG, imp)
    for g in range(N_KV_HEADS):
        sel_ref[:, g * ns:(g + 1) * ns] = sel[g * tq:(g + 1) * tq].astype(sel_ref.dtype)


def _cmp_select(q_slot, ck_slot, cv_slot, overlap, slope_rows, tq, q_off, n_sel, name):
    m = q_slot.shape[0]
    nb, nc, _ = ck_slot.shape
    ns = overlap.shape[1]
    nq = m // (nb * tq)
    assert (tq & (tq - 1)) == 0
    return pl.pallas_call(
        functools.partial(_cmp_select_body, q_off=q_off, n_sel=n_sel),
        grid=(nb, nq),
        in_specs=[pl.BlockSpec((tq, N_HEADS * SLOT), lambda b, i: (b * nq + i, 0)),
                  pl.BlockSpec((None, nc, N_KV_HEADS * SLOT), lambda b, i: (b, 0, 0)),
                  pl.BlockSpec((None, nc, N_KV_HEADS * SLOT), lambda b, i: (b, 0, 0)),
                  _const_spec(overlap.shape), _const_spec(slope_rows.shape)],
        out_specs=[pl.BlockSpec((tq, N_HEADS * SLOT), lambda b, i: (b * nq + i, 0)),
                   pl.BlockSpec((tq, N_KV_HEADS * ns), lambda b, i: (b * nq + i, 0))],
        out_shape=[jax.ShapeDtypeStruct((m, N_HEADS * SLOT), BF16),
                   jax.ShapeDtypeStruct((m, N_KV_HEADS * ns), BF16)],
        compiler_params=_cparams(("parallel", "parallel")),
        name=name,
    )(q_slot, ck_slot, cv_slot, overlap, slope_rows)


def _slc_body(qi_ref, kj_ref, last_ref, q_ref, sel_ref, k_ref, v_ref, slope_ref, o_ref,
              m_scr, l_scr, acc_scr):
    t = pl.program_id(1)
    tq = q_ref.shape[0]
    tk = k_ref.shape[0]
    rows = HEADS_PER_KV * tq
    ns = sel_ref.shape[1] // N_KV_HEADS
    scale = HEAD_DIM ** -0.5
    qi = qi_ref[t]
    kj = kj_ref[t]

    @pl.when(kj == 0)
    def _():
        m_scr[...] = jnp.full_like(m_scr, NEG)
        l_scr[...] = jnp.zeros_like(l_scr)
        acc_scr[...] = jnp.zeros_like(acc_scr)

    tok = lax.broadcasted_iota(jnp.int32, (rows, 1), 0) & (tq - 1)
    pos = qi * tq + tok
    kpos = kj * tk + lax.broadcasted_iota(jnp.int32, (1, tk), 1)
    d_s = (pos - kpos).astype(F32)
    causal = d_s >= 0.0
    blk_row = lax.broadcasted_iota(jnp.int32, (ns, tk), 0)
    key_blk = kj * (tk // SLC_BLOCK) + (lax.broadcasted_iota(jnp.int32, (ns, tk), 1) >> SLC_SHIFT)
    expand = jnp.where(blk_row == key_blk, 1.0, 0.0).astype(BF16)
    for g in range(N_KV_HEADS):
        picked = _dot(sel_ref[:, g * ns:(g + 1) * ns], expand)
        picked = jnp.concatenate([picked] * HEADS_PER_KV, axis=0)
        valid = (picked > 0.5) & causal
        qg = jnp.concatenate(
            [q_ref[:, (g * HEADS_PER_KV + h) * SLOT:(g * HEADS_PER_KV + h + 1) * SLOT]
             for h in range(HEADS_PER_KV)], axis=0)
        s = _dot_nt(qg, k_ref[:, g * SLOT:(g + 1) * SLOT]) * scale - slope_ref[g] * d_s
        s = jnp.where(valid, s, NEG)
        m_old = m_scr[g]
        m_new = jnp.maximum(m_old, jnp.max(s, axis=-1, keepdims=True))
        p = jnp.where(valid, jnp.exp(s - m_new), 0.0)
        alpha = jnp.exp(m_old - m_new)
        l_scr[g] = alpha * l_scr[g] + jnp.sum(p, axis=-1, keepdims=True)
        acc_scr[g] = alpha * acc_scr[g] + _dot(p.astype(BF16), v_ref[:, g * SLOT:(g + 1) * SLOT])
        m_scr[g] = m_new

    @pl.when(last_ref[t] == 1)
    def _():
        for g in range(N_KV_HEADS):
            o = acc_scr[g] * (1.0 / l_scr[g])
            for h in range(HEADS_PER_KV):
                c0 = (g * HEADS_PER_KV + h) * SLOT
                o_ref[:, c0:c0 + SLOT] = o[h * tq:(h + 1) * tq].astype(o_ref.dtype)


def _slc_attention(q_slot, sel, k_slot, v_slot, slope_rows, nb, seq_len, tq, tk, name):
    nq = seq_len // tq
    nk = seq_len // tk
    qi, kj, last = [], [], []
    for i in range(nq):
        j_last = (i * tq + tq - 1) // tk
        for j in range(j_last + 1):
            qi.append(i)
            kj.append(j)
            last.append(1 if j == j_last else 0)
    n_pairs = len(qi)
    qi, kj, last = (jnp.asarray(np.asarray(a, np.int32)) for a in (qi, kj, last))
    rows = HEADS_PER_KV * tq
    in_specs = [pl.BlockSpec((tq, N_HEADS * SLOT), lambda b, t, qi, kj, la: (b * nq + qi[t], 0)),
                pl.BlockSpec((tq, sel.shape[1]), lambda b, t, qi, kj, la: (b * nq + qi[t], 0)),
                pl.BlockSpec((tk, N_KV_HEADS * SLOT), lambda b, t, qi, kj, la: (b * nk + kj[t], 0)),
                pl.BlockSpec((tk, N_KV_HEADS * SLOT), lambda b, t, qi, kj, la: (b * nk + kj[t], 0)),
                pl.BlockSpec(slope_rows.shape, lambda b, t, qi, kj, la: (0, 0, 0))]
    grid_spec = pltpu.PrefetchScalarGridSpec(
        num_scalar_prefetch=3, grid=(nb, n_pairs), in_specs=in_specs,
        out_specs=pl.BlockSpec((tq, N_HEADS * SLOT), lambda b, t, qi, kj, la: (b * nq + qi[t], 0)),
        scratch_shapes=[pltpu.VMEM((N_KV_HEADS, rows, 1), F32),
                        pltpu.VMEM((N_KV_HEADS, rows, 1), F32),
                        pltpu.VMEM((N_KV_HEADS, rows, SLOT), F32)])
    return pl.pallas_call(
        _slc_body,
        grid_spec=grid_spec,
        out_shape=jax.ShapeDtypeStruct(q_slot.shape, BF16),
        compiler_params=_cparams(("parallel", "arbitrary")),
        name=name,
    )(qi, kj, last, q_slot, sel, k_slot, v_slot, slope_rows)


def _win_body(q_ref, *refs, n_blk):
    k_refs = refs[:n_blk]
    v_refs = refs[n_blk:2 * n_blk]
    slope_ref, o_ref = refs[2 * n_blk:]
    tq = q_ref.shape[0]
    rows = HEADS_PER_KV * tq
    scale = HEAD_DIM ** -0.5
    i = pl.program_id(1)
    tok = lax.broadcasted_iota(jnp.int32, (rows, 1), 0) & (tq - 1)
    pos = i * tq + tok
    kpos = (i - (n_blk - 1)) * tq + lax.broadcasted_iota(jnp.int32, (1, n_blk * tq), 1)
    d_w = pos - kpos
    valid = (d_w >= 0) & (d_w < WINDOW) & (kpos >= 0)
    d_f = d_w.astype(F32)
    for g in range(N_KV_HEADS):
        cs = slice(g * SLOT, (g + 1) * SLOT)
        qg = jnp.concatenate(
            [q_ref[:, (g * HEADS_PER_KV + h) * SLOT:(g * HEADS_PER_KV + h + 1) * SLOT]
             for h in range(HEADS_PER_KV)], axis=0)
        kg = jnp.concatenate([k_refs[k][:, cs] for k in range(n_blk)], axis=0)
        vg = jnp.concatenate([v_refs[k][:, cs] for k in range(n_blk)], axis=0)
        s = _dot_nt(qg, kg) * scale - slope_ref[g] * d_f
        s = jnp.where(valid, s, NEG)
        e = jnp.where(valid, jnp.exp(s - jnp.max(s, axis=-1, keepdims=True)), 0.0)
        p = e * (1.0 / jnp.sum(e, axis=-1, keepdims=True))
        o = _dot(p.astype(BF16), vg)
        for h in range(HEADS_PER_KV):
            c0 = (g * HEADS_PER_KV + h) * SLOT
            o_ref[:, c0:c0 + SLOT] = o[h * tq:(h + 1) * tq].astype(o_ref.dtype)


def _win_attention(q_slot, k_slot, v_slot, slope_rows, nb, seq_len, tq, name):
    nq = seq_len // tq
    n_blk = WINDOW // tq + 1

    def kv_map(k):
        def index(b, i):
            return (b * nq + jnp.maximum(i - (n_blk - 1) + k, 0), 0)
        return index

    kv_specs = [pl.BlockSpec((tq, N_KV_HEADS * SLOT), kv_map(k)) for k in range(n_blk)]
    return pl.pallas_call(
        functools.partial(_win_body, n_blk=n_blk),
        grid=(nb, nq),
        in_specs=[pl.BlockSpec((tq, N_HEADS * SLOT), lambda b, i: (b * nq + i, 0))]
        + kv_specs + kv_specs + [_const_spec(slope_rows.shape)],
        out_specs=pl.BlockSpec((tq, N_HEADS * SLOT), lambda b, i: (b * nq + i, 0)),
        out_shape=jax.ShapeDtypeStruct(q_slot.shape, BF16),
        compiler_params=_cparams(("parallel", "parallel")),
        name=name,
    )(q_slot, *([k_slot] * n_blk), *([v_slot] * n_blk), slope_rows)


def _paged_attn_body(pt_ref, *refs, n_pages, n_steps, past_base, q_off, tokens, window, use_sel):
    page_refs = refs[:n_pages]
    refs = refs[n_pages:]
    new_ref, q_ref, slope_ref = refs[:3]
    refs = refs[3:]
    if use_sel:
        sel_ref = refs[0]
        refs = refs[1:]
    o_ref, m_scr, l_scr, acc_scr = refs
    s_id = pl.program_id(1)
    rows = q_ref.shape[0]
    half = N_KV_HEADS * HEAD_DIM
    scale = HEAD_DIM ** -0.5

    @pl.when(s_id == 0)
    def _():
        m_scr[...] = jnp.full_like(m_scr, NEG)
        l_scr[...] = jnp.zeros_like(l_scr)
        acc_scr[...] = jnp.zeros_like(acc_scr)

    pos = q_off + (lax.broadcasted_iota(jnp.int32, (rows, 1), 0) & (tokens - 1))
    q = q_ref[...]
    key = lax.broadcasted_iota(jnp.int32, (1, PAGE), 1)

    def update(pages, kpos0s, blk0s):
        ss, keeps = [], []
        for ref, kpos0, blk0 in zip(pages, kpos0s, blk0s):
            d = pos - (kpos0 + key)
            valid = d >= 0
            if window is not None:
                valid = valid & (d < window)
            if use_sel:
                ns = sel_ref.shape[1]
                blk_row = lax.broadcasted_iota(jnp.int32, (ns, PAGE), 0)
                key_blk = blk0 + (lax.broadcasted_iota(jnp.int32, (ns, PAGE), 1) >> SLC_SHIFT)
                expand = jnp.where(blk_row == key_blk, 1.0, 0.0).astype(BF16)
                valid = valid & (_dot(sel_ref[...], expand) > 0.5)
            s = _dot_nt(q, ref[:, :half].astype(BF16)) * scale - slope_ref[...] * d.astype(F32)
            ss.append(jnp.where(valid, s, NEG))
            keeps.append(jnp.where(valid, 1.0, 0.0))
        s = jnp.concatenate(ss, axis=1)
        keep = jnp.concatenate(keeps, axis=1)
        m_old = m_scr[...]
        m_new = jnp.maximum(m_old, jnp.max(s, axis=-1, keepdims=True))
        p = jnp.exp(s - m_new) * keep
        alpha = jnp.exp(m_old - m_new)
        l_scr[...] = alpha * l_scr[...] + jnp.sum(p, axis=-1, keepdims=True)
        p = p.astype(BF16)
        acc = alpha * acc_scr[...]
        for k, ref in enumerate(pages):
            acc = acc + _dot(p[:, k * PAGE:(k + 1) * PAGE], ref[:, half:].astype(BF16))
        acc_scr[...] = acc
        m_scr[...] = m_new

    first_page = s_id * n_pages
    update(page_refs,
           [past_base + (first_page + k) * PAGE for k in range(n_pages)],
           [(first_page + k) * (PAGE // SLC_BLOCK) for k in range(n_pages)])

    @pl.when(s_id == n_steps - 1)
    def _():
        update([new_ref], [q_off], [(q_off - past_base) // SLC_BLOCK])
        o_ref[...] = acc_scr[...] * (1.0 / l_scr[...])


def _paged_attention(pages, page_table, new_rows, q_bd, slope_rows, sel_rows, n_pages_step,
                     past_base, q_off, tokens, window, name):
    nb, n_p = page_table.shape
    assert n_p % n_pages_step == 0
    n_steps = n_p // n_pages_step
    rows = q_bd.shape[1]
    half = N_KV_HEADS * HEAD_DIM
    use_sel = sel_rows is not None

    def page_map(k):
        def index(b, s, pt):
            return (pt[b * n_p + s * n_pages_step + k], 0, 0)
        return index

    in_specs = [pl.BlockSpec((None, PAGE, KV_ROW), page_map(k)) for k in range(n_pages_step)]
    in_specs += [pl.BlockSpec((None, PAGE, KV_ROW), lambda b, s, pt: (b, 0, 0)),
                 pl.BlockSpec((None, rows, half), lambda b, s, pt: (b, 0, 0)),
                 pl.BlockSpec(slope_rows.shape, lambda b, s, pt: (0, 0))]
    args = [pages] * n_pages_step + [new_rows, q_bd, slope_rows]
    if use_sel:
        in_specs.append(pl.BlockSpec((None,) + sel_rows.shape[1:], lambda b, s, pt: (b, 0, 0)))
        args.append(sel_rows)
    grid_spec = pltpu.PrefetchScalarGridSpec(
        num_scalar_prefetch=1, grid=(nb, n_steps), in_specs=in_specs,
        out_specs=pl.BlockSpec((None, rows, half), lambda b, s, pt: (b, 0, 0)),
        scratch_shapes=[pltpu.VMEM((rows, 1), F32), pltpu.VMEM((rows, 1), F32),
                        pltpu.VMEM((rows, half), F32)])
    body = functools.partial(_paged_attn_body, n_pages=n_pages_step, n_steps=n_steps,
                             past_base=past_base, q_off=q_off, tokens=tokens, window=window,
                             use_sel=use_sel)
    return pl.pallas_call(
        body,
        grid_spec=grid_spec,
        out_shape=jax.ShapeDtypeStruct((nb, rows, half), F32),
        compiler_params=_cparams(("parallel", "arbitrary")),
        name=name,
    )(page_table.reshape(-1), *args)


def _combine_body(x_ref, gates_ref, oc_ref, os_ref, ow_ref, wo_ref, o_ref, mix_scr):
    gt = jax.nn.sigmoid(gates_ref[...])
    for hh in range(N_HEADS):
        cs = slice(hh * SLOT, (hh + 1) * SLOT)
        mix = gt[:, hh:hh + 1] * oc_ref[:, cs].astype(F32)
        mix = mix + gt[:, N_HEADS + hh:N_HEADS + hh + 1] * os_ref[:, cs].astype(F32)
        mix = mix + gt[:, 2 * N_HEADS + hh:2 * N_HEADS + hh + 1] * ow_ref[:, cs].astype(F32)
        mix_scr[:, cs] = mix.astype(BF16)
    o_ref[...] = x_ref[...] + _dot(mix_scr[...], wo_ref[...])


def _combine(x, gates, o_c, o_s, o_w, w_out_slot, tm, name):
    m, d = x.shape
    wide = N_HEADS * SLOT
    row_spec = lambda w: pl.BlockSpec((tm, w), lambda i: (i, 0))
    return pl.pallas_call(
        _combine_body,
        grid=(m // tm,),
        in_specs=[row_spec(d), row_spec(gates.shape[1]), row_spec(wide), row_spec(wide), row_spec(wide),
                  _const_spec(w_out_slot.shape)],
        out_specs=row_spec(d),
        out_shape=jax.ShapeDtypeStruct((m, d), F32),
        scratch_shapes=[pltpu.VMEM((tm, wide), BF16)],
        compiler_params=_cparams(("parallel",)),
        name=name,
    )(x, gates, o_c, o_s, o_w, w_out_slot)


def _slot_cols(w):
    d, n = w.shape
    w = w.reshape(d, n // HEAD_DIM, HEAD_DIM)
    return jnp.pad(w, ((0, 0), (0, 0), (0, SLOT - HEAD_DIM))).reshape(d, (n // HEAD_DIM) * SLOT)


def _slot_rows(w):
    n, d = w.shape
    w = w.reshape(n // HEAD_DIM, HEAD_DIM, d)
    return jnp.pad(w, ((0, 0), (0, SLOT - HEAD_DIM), (0, 0))).reshape((n // HEAD_DIM) * SLOT, d)


def _slot_last(a):
    lead = a.shape[:-1]
    n = a.shape[-1] // HEAD_DIM
    a = a.reshape(lead + (n, HEAD_DIM))
    a = jnp.pad(a, [(0, 0)] * len(lead) + [(0, 0), (0, SLOT - HEAD_DIM)])
    return a.reshape(lead + (n * SLOT,))


def _alibi_slopes():
    h = np.arange(1, N_HEADS + 1, dtype=np.float32)
    return np.asarray(2.0 ** (-8.0 * h / N_HEADS), dtype=np.float32).reshape(N_KV_HEADS, HEADS_PER_KV)


def _slope_rows_by_group(tq):
    return jnp.asarray(np.repeat(_alibi_slopes(), tq, axis=1)[:, :, None])


def _overlap(n_cmp, n_slc, nc, ns):
    st = np.arange(nc) * CMP_STRIDE
    bs = np.arange(ns) * SLC_BLOCK
    m = (st[:, None] < bs[None, :] + SLC_BLOCK) & (st[:, None] + CMP_LEN > bs[None, :])
    m = m & (np.arange(nc)[:, None] < n_cmp) & (np.arange(ns)[None, :] < n_slc)
    return jnp.asarray(m.astype(np.float32), dtype=BF16)


def _compress_weights(cmp_pe, cmp_w):
    eye = jnp.eye(N_KV_HEADS, dtype=F32)
    w = cmp_w.reshape(2, 2, CMP_STRIDE, HEAD_DIM, HEAD_DIM)
    bd = jnp.einsum('kmrde,gh->rkgdmhe', w, eye)
    half = N_KV_HEADS * HEAD_DIM
    w_big = bd.reshape(CMP_STRIDE, 2, half, 2 * half).astype(BF16)
    pe = cmp_pe.reshape(2, 2, CMP_STRIDE, HEAD_DIM)
    pe = jnp.broadcast_to(pe.transpose(1, 2, 0, 3)[:, :, :, None, :],
                          (2, CMP_STRIDE, 2, N_KV_HEADS, HEAD_DIM)).reshape(2, SEG_W)
    pe_seg = jnp.pad(pe, ((0, SUBLANES - 2), (0, 0)))
    return pe_seg, w_big


def _nsa_weights(w_in, segs_slot_kv):
    q_dim = N_HEADS * HEAD_DIM
    half = N_KV_HEADS * HEAD_DIM
    w_q = w_in[:, :q_dim]
    w_gate = w_in[:, q_dim + N_BRANCH * KV_ROW:]
    parts = [_slot_cols(w_q), w_in[:, q_dim:q_dim + N_BRANCH * KV_ROW],
             jnp.pad(w_gate, ((0, 0), (0, LANES - w_gate.shape[1])))]
    segs = [(N_HEADS * SLOT, BF16, False), (KV_ROW, F32, False), (KV_ROW, F32, False),
            (KV_ROW, F32, False), (LANES, F32, False)]
    if segs_slot_kv:
        for br in (1, 2):
            o = q_dim + br * KV_ROW
            parts += [_slot_cols(w_in[:, o:o + half]), _slot_cols(w_in[:, o + half:o + KV_ROW])]
            segs += [(N_KV_HEADS * SLOT, BF16, False)] * 2
    return jnp.concatenate(parts, axis=1).astype(BF16), segs


def _sgu_layer(x, p, a, seq_len, tm, emit_v, tag):
    d = x.shape[1]
    d_sgu = p['sgu_w_out'].shape[1]
    gw = d_sgu // N_SGU_GROUPS
    segs = [(d_sgu, BF16, True), (d_sgu, BF16, True)]
    u, v = _proj(x, p['norm_mix'][0], p['sgu_w_in'][a].astype(BF16), p['sgu_b_in'][a].reshape(1, -1),
                 segs, tm, f"sgu_in_{tag}")
    w_s, b_s = p['sgu_w_s'][a], p['sgu_b_s'][a]
    if seq_len % CHUNK == 0:
        bs_rows = b_s.T
    else:
        assert CHUNK % seq_len == 0
        rep = CHUNK // seq_len
        w_s = jnp.einsum('ab,gts->gatbs', jnp.eye(rep, dtype=F32), w_s[:, :seq_len, :seq_len])
        w_s = w_s.reshape(N_SGU_GROUPS, CHUNK, CHUNK)
        bs_rows = jnp.tile(b_s[:, :seq_len].T, (rep, 1))
    bs_exp = jnp.repeat(bs_rows, gw, axis=1)
    return _sgu(x, u, v, p['sgu_v_norm'][a], w_s, bs_exp, p['sgu_w_out'][a].astype(BF16), tm, emit_v,
                f"sgu_{tag}")


def _ffn_layer(x, p, layer, seq_len, past, final, tm, tag):
    gamma_final = p['norm_final'] if final else None
    return _ffn(x, p['norm_ffn'][layer], p['ffn_w_in'][layer].astype(BF16), p['ffn_conv_w'][layer],
                p['ffn_conv_b'][layer], p['ffn_w_out'][layer].astype(BF16), seq_len, past, gamma_final,
                tm, 256, f"ffn{layer}_{tag}")


def _nsa_prompt(x, p, b_idx, nb, seq_len, tm):
    w_ext, segs = _nsa_weights(p['nsa_w_in'][b_idx], True)
    q_slot, kv_cmp, kv_slc, kv_win, gates, ks_slot, vs_slot, kw_slot, vw_slot = _proj(
        x, p['norm_mix'][1], w_ext, None, segs, tm, "nsa_in_prompt")
    n_pages = seq_len // PAGE
    pe_seg, w_big = _compress_weights(p['nsa_cmp_pe'][b_idx], p['nsa_cmp_w'][b_idx])
    table = jnp.arange(nb * n_pages, dtype=jnp.int32).reshape(nb, n_pages)
    ck, cv = _compress(kv_cmp.reshape(nb * n_pages, SUBLANES, SEG_W), table, pe_seg, w_big,
                       min(n_pages, 32), "compress_prompt")
    n_seg = seq_len // CMP_STRIDE
    n_cmp = n_seg - CMP_LEN // CMP_STRIDE + 1
    n_slc = -(-seq_len // SLC_BLOCK)
    ns = -(-n_slc // LANES) * LANES
    tq = 128
    slope_rows = _slope_rows_by_group(tq)
    o_c, sel = _cmp_select(q_slot, _slot_last(ck).astype(BF16), _slot_last(cv).astype(BF16),
                           _overlap(n_cmp, n_slc, n_seg, ns), slope_rows, tq, 0,
                           min(SLC_TOP_N, n_slc), "cmp_select_prompt")
    o_s = _slc_attention(q_slot, sel, ks_slot, vs_slot, slope_rows, nb, seq_len, tq,
                         min(512, seq_len), "slc_prompt")
    o_w = _win_attention(q_slot, kw_slot, vw_slot, slope_rows, nb, seq_len, tq, "win_prompt")
    x = _combine(x, gates, o_c, o_s, o_w, _slot_rows(p['nsa_w_out'][b_idx]).astype(BF16), tm,
                 "nsa_out_prompt")
    kv_shape = (nb, seq_len, 2, N_KV_HEADS, HEAD_DIM)
    kv_win = kv_win.reshape(kv_shape)
    return x, kv_cmp.reshape(kv_shape), kv_slc.reshape(kv_shape), kv_win[:, -min(WINDOW, seq_len):]


def _rows_ght(a, tokens_pad):
    nb, t, _, w = a.shape
    a = a.reshape(nb, t, N_KV_HEADS, HEADS_PER_KV, w).transpose(0, 2, 3, 1, 4)
    a = jnp.pad(a, ((0, 0), (0, 0), (0, 0), (0, tokens_pad - t), (0, 0)))
    return a.reshape(nb, N_HEADS * tokens_pad, w)


def _nsa_sample(x, p, b_idx, nb, t_new, past_len, cache_cmp, cache_slc, state_win, page_table):
    m = x.shape[0]
    half = N_KV_HEADS * HEAD_DIM
    t_pad = BF16_SUBLANES
    assert t_new <= t_pad and past_len % PAGE == 0
    w_ext, segs = _nsa_weights(p['nsa_w_in'][b_idx], False)
    q_slot, kv_cmp, kv_slc, kv_win, gates = _proj(x, p['norm_mix'][1], w_ext, None, segs, m, "nsa_in_sample")
    n_pool = cache_cmp.shape[0]
    n_pages = past_len // PAGE
    total = past_len + t_new
    n_seg = total // CMP_STRIDE
    assert n_seg == past_len // CMP_STRIDE, "new rows must not complete a compression segment"
    n_cmp = n_seg - CMP_LEN // CMP_STRIDE + 1
    pe_seg, w_big = _compress_weights(p['nsa_cmp_pe'][b_idx], p['nsa_cmp_w'][b_idx])
    ck, cv = _compress(cache_cmp.reshape(n_pool, SUBLANES, SEG_W), page_table, pe_seg, w_big,
                       min(n_pages, 32), "compress_sample")
    n_slc = -(-total // SLC_BLOCK)
    ns = -(-n_slc // LANES) * LANES
    q4 = q_slot.reshape(nb, t_new, N_HEADS * SLOT)
    q_pad = jnp.pad(q4, ((0, 0), (0, t_pad - t_new), (0, 0))).reshape(nb * t_pad, N_HEADS * SLOT)
    o_c, sel = _cmp_select(q_pad, _slot_last(ck).astype(BF16), _slot_last(cv).astype(BF16),
                           _overlap(n_cmp, n_slc, n_seg, ns), _slope_rows_by_group(t_pad), t_pad,
                           past_len, min(SLC_TOP_N, n_slc), "cmp_select_sample")
    o_c = o_c.reshape(nb, t_pad, N_HEADS * SLOT)[:, :t_new].reshape(m, N_HEADS * SLOT)
    qh = q4.reshape(nb, t_new, N_HEADS, SLOT)[..., :HEAD_DIM]
    q_rows = _rows_ght(qh, t_pad).reshape(nb, N_KV_HEADS, HEADS_PER_KV * t_pad, HEAD_DIM)
    q_bd = jnp.einsum('bgrd,gk->bgrkd', q_rows, jnp.eye(N_KV_HEADS, dtype=BF16))
    q_bd = q_bd.reshape(nb, N_HEADS * t_pad, half)
    slope_rows = jnp.asarray(np.repeat(_alibi_slopes().reshape(-1), t_pad)[:, None])
    sel_rows = sel.reshape(nb, t_pad, N_KV_HEADS, 1, ns).transpose(0, 2, 3, 1, 4)
    sel_rows = jnp.broadcast_to(sel_rows, (nb, N_KV_HEADS, HEADS_PER_KV, t_pad, ns))
    sel_rows = sel_rows.reshape(nb, N_HEADS * t_pad, ns)

    def new_page(kv):
        return jnp.pad(kv.reshape(nb, t_new, KV_ROW), ((0, 0), (0, PAGE - t_new), (0, 0)))

    o_s = _paged_attention(cache_slc.reshape(n_pool, PAGE, KV_ROW), page_table, new_page(kv_slc), q_bd,
                           slope_rows, sel_rows, min(n_pages, 8), 0, past_len, t_pad, None, "slc_sample")
    n_win = state_win.shape[1]
    assert n_win % PAGE == 0
    win_pages = n_win // PAGE
    win_table = jnp.arange(nb * win_pages, dtype=jnp.int32).reshape(nb, win_pages)
    o_w = _paged_attention(state_win.reshape(nb * win_pages, PAGE, KV_ROW), win_table, new_page(kv_win), q_bd,
                           slope_rows, None, win_pages, past_len - n_win, past_len, t_pad, WINDOW,
                           "win_sample")

    def to_slot(o):
        o = o.reshape(nb, N_KV_HEADS, HEADS_PER_KV, t_pad, N_KV_HEADS, HEAD_DIM)
        o = jnp.stack([o[:, g, :, :t_new, g, :] for g in range(N_KV_HEADS)], axis=1)
        o = o.transpose(0, 3, 1, 2, 4).reshape(m, N_HEADS * HEAD_DIM)
        return _slot_last(o).astype(BF16)

    x = _combine(x, gates, o_c, to_slot(o_s), to_slot(o_w), _slot_rows(p['nsa_w_out'][b_idx]).astype(BF16),
                 m, "nsa_out_sample")
    kv_shape = (nb, t_new, 2, N_KV_HEADS, HEAD_DIM)
    kv_win5 = kv_win.reshape(kv_shape)
    all_win = jnp.concatenate([state_win.reshape((nb, n_win) + kv_shape[2:]), kv_win5], axis=1)
    return x, kv_cmp.reshape(kv_shape), kv_slc.reshape(kv_shape), all_win[:, -min(WINDOW, n_win + t_new):]


def _run_prompt(x_prompt, p):
    nb, seq_len, d = x_prompt.shape
    x = x_prompt.reshape(nb * seq_len, d)
    tm = min(512, seq_len)
    tm_ffn = min(1024, seq_len)
    x = _sgu_layer(x, p, 0, seq_len, tm, False, "prompt")[0]
    x, h0 = _ffn_layer(x, p, 0, seq_len, None, False, tm_ffn, "prompt")
    x, kc, ks, kw = _nsa_prompt(x, p, 0, nb, seq_len, tm)
    x, h1 = _ffn_layer(x, p, 1, seq_len, None, True, tm_ffn, "prompt")
    tiles = seq_len // tm_ffn

    def conv_state(h):
        return h.reshape(nb, tiles, SUBLANES, -1)[:, -1, SUBLANES - (CONV_W - 1):]

    return (x.reshape(nb, seq_len, d), kc[None], ks[None], kw[None],
            jnp.stack([conv_state(h0), conv_state(h1)]))


def _run_sample(x_sample, cache_cmp, cache_slc, state_win, state_conv, page_table, p):
    nb, t_new, d = x_sample.shape
    m = nb * t_new
    past_len = page_table.shape[1] * PAGE
    x = x_sample.reshape(m, d)
    d_ff = state_conv.shape[-1]

    def conv_past(layer):
        st = state_conv[layer]
        zeros = jnp.zeros((nb, t_new, d_ff), F32)
        prev1 = zeros.at[:, 0].set(st[:, 1])
        prev2 = zeros.at[:, 0].set(st[:, 0]).at[:, 1].set(st[:, 1])
        return prev1.reshape(m, d_ff), prev2.reshape(m, d_ff)

    def conv_state(h):
        return h.reshape(nb, t_new, d_ff)[:, -(CONV_W - 1):]

    x, vn = _sgu_layer(x, p, 0, t_new, m, True, "sample")
    x, h0 = _ffn_layer(x, p, 0, t_new, conv_past(0), False, m, "sample")
    x, kc, ks, kw = _nsa_sample(x, p, 0, nb, t_new, past_len, cache_cmp[0], cache_slc[0], state_win[0],
                                page_table)
    x, h1 = _ffn_layer(x, p, 1, t_new, conv_past(1), True, m, "sample")
    return (x.reshape(nb, t_new, d), kc[None], ks[None], kw[None],
            jnp.stack([conv_state(h0), conv_state(h1)]), vn.reshape(nb, t_new, -1)[None])


def kernel(x_prompt, x_sample, cache_cmp_kv, cache_slc_kv, state_win_kv, state_ffn_conv, page_table,
           norm_mix, norm_ffn, norm_final, sgu_w_in, sgu_b_in, sgu_v_norm, sgu_w_s, sgu_b_s, sgu_w_out,
           nsa_w_in, nsa_cmp_pe, nsa_cmp_w, nsa_w_out, ffn_w_in, ffn_conv_w, ffn_conv_b, ffn_w_out):
    p = {'norm_mix': norm_mix, 'norm_ffn': norm_ffn, 'norm_final': norm_final,
         'sgu_w_in': sgu_w_in, 'sgu_b_in': sgu_b_in, 'sgu_v_norm': sgu_v_norm, 'sgu_w_s': sgu_w_s,
         'sgu_b_s': sgu_b_s, 'sgu_w_out': sgu_w_out, 'nsa_w_in': nsa_w_in, 'nsa_cmp_pe': nsa_cmp_pe,
         'nsa_cmp_w': nsa_cmp_w, 'nsa_w_out': nsa_w_out, 'ffn_w_in': ffn_w_in, 'ffn_conv_w': ffn_conv_w,
         'ffn_conv_b': ffn_conv_b, 'ffn_w_out': ffn_w_out}
    y_p, p_cmp, p_slc, p_win, p_conv = _run_prompt(x_prompt, p)
    y_s, s_cmp, s_slc, s_win, s_conv, s_v = _run_sample(
        x_sample, cache_cmp_kv, cache_slc_kv, state_win_kv, state_ffn_conv, page_table, p)
    return (y_p, y_s, p_cmp, p_slc, p_win, p_conv, s_cmp, s_slc, s_win, s_conv, s_v)
```

```python
import functools

import numpy as np
import jax
import jax.numpy as jnp
from jax import lax
from jax.experimental import pallas as pl
from jax.experimental.pallas import tpu as pltpu

F32 = jnp.float32
BF16 = jnp.bfloat16

EPS = 1e-6
NEG = -1e30
FORCE = 1e9
BELOW_NEG = -3e38
MASK_BIG = 2.0 ** 30

CHUNK = 128
N_SGU_GROUPS = 8
HEAD_DIM = 64
N_KV_HEADS = 4
HEADS_PER_KV = 4
N_HEADS = N_KV_HEADS * HEADS_PER_KV
N_BRANCH = 3
CMP_LEN = 32
CMP_STRIDE = 16
SLC_BLOCK = 64
SLC_SHIFT = SLC_BLOCK.bit_length() - 1
SLC_TOP_N = 16
WINDOW = 512
CONV_W = 3
PAGE = 128
KV_ROW = 2 * N_KV_HEADS * HEAD_DIM
SEG_W = CMP_STRIDE * KV_ROW

LANES = 128
SUBLANES = 8
BF16_SUBLANES = 16
SLOT = LANES
ALIBI_LANE = HEAD_DIM
SUM_ROW = HEAD_DIM
V7X_VMEM_BYTES = 64 * 1024 * 1024
VMEM_LIMIT = (V7X_VMEM_BYTES * 3) // 4


def _cparams(sem):
    return pltpu.CompilerParams(dimension_semantics=sem, vmem_limit_bytes=VMEM_LIMIT)


def _const_spec(shape):
    nd = len(shape)
    return pl.BlockSpec(shape, lambda *_: (0,) * nd)


def _rms(x, g):
    return x * lax.rsqrt(jnp.mean(x * x, axis=-1, keepdims=True) + EPS) * g


def _dot(a, b):
    return jnp.dot(a, b, preferred_element_type=F32)


def _dot_nt(a, b):
    return lax.dot_general(a, b, (((1,), (1,)), ((), ())), preferred_element_type=F32)


def _proj_body(x_ref, g_ref, w_ref, b_ref, *o_refs, segs, chunk, has_bias, has_wt):
    xn = _rms(x_ref[...], g_ref[...]).astype(BF16)
    if has_wt:
        wt_ref, ot_ref = o_refs[0], o_refs[-1]
        o_refs = o_refs[1:-1]
        ot_ref[...] = _dot_nt(wt_ref[...], xn).astype(ot_ref.dtype)
    off = 0
    for (width, _, act), o_ref in zip(segs, o_refs):
        for c0 in range(0, width, chunk):
            cw = min(chunk, width - c0)
            r = _dot(xn, w_ref[:, off + c0:off + c0 + cw])
            if has_bias:
                r = r + b_ref[:, off + c0:off + c0 + cw]
            if act:
                r = jax.nn.gelu(r)
            o_ref[:, c0:c0 + cw] = r.astype(o_ref.dtype)
        off += width


def _proj(x, gamma, w, bias, segs, tm, name, w_t=None):
    m, d = x.shape
    n = w.shape[1]
    assert sum(s[0] for s in segs) == n and m % tm == 0
    has_bias = bias is not None
    if bias is None:
        bias = jnp.zeros((1, LANES), F32)
    in_specs = [pl.BlockSpec((tm, d), lambda i: (i, 0)),
                _const_spec((1, d)), _const_spec(w.shape), _const_spec(bias.shape)]
    out_specs = [pl.BlockSpec((tm, s[0]), lambda i: (i, 0)) for s in segs]
    out_shape = [jax.ShapeDtypeStruct((m, s[0]), s[1]) for s in segs]
    args = [x, gamma.reshape(1, d), w, bias]
    if w_t is not None:
        in_specs.append(_const_spec(w_t.shape))
        args.append(w_t)
        out_specs.append(pl.BlockSpec((None, w_t.shape[0], tm), lambda i: (i, 0, 0)))
        out_shape.append(jax.ShapeDtypeStruct((m // tm, w_t.shape[0], tm), BF16))
    body = functools.partial(_proj_body, segs=tuple(segs), chunk=512, has_bias=has_bias,
                             has_wt=w_t is not None)
    return pl.pallas_call(
        body,
        grid=(m // tm,),
        in_specs=in_specs,
        out_specs=out_specs,
        out_shape=out_shape,
        compiler_params=_cparams(("parallel",)),
        name=name,
    )(*args)


def _sgu_body(x_ref, u_ref, v_ref, vg_ref, ws_ref, bs_ref, wo_ref, *rest, emit_v):
    if emit_v:
        o_ref, vn_ref, vn_scr, a_scr = rest
    else:
        o_ref, vn_scr, a_scr = rest
    tm = x_ref.shape[0]
    gw = v_ref.shape[1] // N_SGU_GROUPS
    vn = _rms(v_ref[...].astype(F32), vg_ref[...])
    if emit_v:
        vn_ref[...] = vn
    vn_scr[...] = vn.astype(BF16)
    row = lax.broadcasted_iota(jnp.int32, (CHUNK, CHUNK), 0)
    col = lax.broadcasted_iota(jnp.int32, (CHUNK, CHUNK), 1)
    for g in range(N_SGU_GROUPS):
        wc = jnp.where(col <= row, ws_ref[g], 0.0).astype(BF16)
        for c in range(tm // CHUNK):
            rs = slice(c * CHUNK, (c + 1) * CHUNK)
            cs = slice(g * gw, (g + 1) * gw)
            s = _dot(wc, vn_scr[rs, cs]) + bs_ref[:, cs]
            a_scr[rs, cs] = (u_ref[rs, cs].astype(F32) * s).astype(BF16)
    o_ref[...] = x_ref[...] + _dot(a_scr[...], wo_ref[...])


def _sgu(x, u, v, v_norm, w_s, bs_exp, w_out, tm, emit_v, name):
    m, d = x.shape
    ds = u.shape[1]
    out_shape = [jax.ShapeDtypeStruct((m, d), F32)]
    out_specs = [pl.BlockSpec((tm, d), lambda i: (i, 0))]
    if emit_v:
        out_shape.append(jax.ShapeDtypeStruct((m, ds), F32))
        out_specs.append(pl.BlockSpec((tm, ds), lambda i: (i, 0)))
    return pl.pallas_call(
        functools.partial(_sgu_body, emit_v=emit_v),
        grid=(m // tm,),
        in_specs=[pl.BlockSpec((tm, d), lambda i: (i, 0)),
                  pl.BlockSpec((tm, ds), lambda i: (i, 0)),
                  pl.BlockSpec((tm, ds), lambda i: (i, 0)),
                  _const_spec((1, ds)), _const_spec(w_s.shape), _const_spec(bs_exp.shape),
                  _const_spec(w_out.shape)],
        out_specs=out_specs,
        out_shape=out_shape,
        scratch_shapes=[pltpu.VMEM((tm, ds), BF16), pltpu.VMEM((tm, ds), BF16)],
        compiler_params=_cparams(("parallel",)),
        name=name,
    )(x, u, v, v_norm.reshape(1, ds), w_s, bs_exp, w_out)


def _ffn_body(*refs, sample, final_norm, period, tiles_per_seq, n_f):
    refs = list(refs)
    x_ref, g_ref, wh_ref, wg_ref, cw_ref, cb_ref, wo_ref = refs[:7]
    refs = refs[7:]
    if sample:
        p1_ref, p2_ref = refs[:2]
        refs = refs[2:]
    if final_norm:
        gf_ref = refs[0]
        refs = refs[1:]
    o_ref, h_ref, xn_scr, acc_scr = refs[:4]
    i = pl.program_id(0)
    f = pl.program_id(1)
    tm = x_ref.shape[0]

    @pl.when(f == 0)
    def _():
        xn_scr[...] = _rms(x_ref[...], g_ref[...]).astype(BF16)
        acc_scr[...] = jnp.zeros_like(acc_scr)

    xn = xn_scr[...]
    h = _dot(xn, wh_ref[...])
    gate = _dot(xn, wg_ref[...])
    r = lax.broadcasted_iota(jnp.int32, (tm, 1), 0)
    if sample:
        r = r & (period - 1)
        prev1 = p1_ref[...]
        prev2 = p2_ref[...]
        h_ref[...] = h
    else:
        carry_scr = refs[4]

        @pl.when(i % tiles_per_seq == 0)
        def _():
            carry_scr[f] = jnp.zeros(carry_scr.shape[1:], F32)

        prev = carry_scr[f]
        c0 = prev[SUBLANES - 2:SUBLANES - 1, :]
        c1 = prev[SUBLANES - 1:SUBLANES, :]
        prev1 = c1
        prev2 = jnp.where(r == 1, c1, c0)
        tail = h[tm - SUBLANES:tm, :]
        carry_scr[f] = tail
        h_ref[...] = tail
    hm1 = jnp.where(r >= 1, pltpu.roll(h, 1, 0), prev1)
    hm2 = jnp.where(r >= 2, pltpu.roll(h, 2, 0), prev2)
    hc = cb_ref[...] + cw_ref[0:1, :] * hm2
    hc = hc + cw_ref[1:2, :] * hm1
    hc = hc + cw_ref[2:3, :] * h
    act = (jax.nn.gelu(hc) * gate).astype(BF16)
    acc_scr[...] += _dot(act, wo_ref[...])

    @pl.when(f == n_f - 1)
    def _():
        y = x_ref[...] + acc_scr[...]
        if final_norm:
            y = _rms(y, gf_ref[...])
        o_ref[...] = y


def _ffn(x, gamma, w_in, conv_w, conv_b, w_out, seq_len, past, gamma_final, tm, tf, name):
    m, d = x.shape
    d_ff = w_out.shape[0]
    assert m % tm == 0 and d_ff % tf == 0
    n_f = d_ff // tf
    sample = past is not None
    final_norm = gamma_final is not None
    if sample:
        assert tm == m and (seq_len & (seq_len - 1)) == 0
    else:
        assert seq_len % tm == 0
    in_specs = [pl.BlockSpec((tm, d), lambda i, f: (i, 0)),
                _const_spec((1, d)),
                pl.BlockSpec((d, tf), lambda i, f: (0, f)),
                pl.BlockSpec((d, tf), lambda i, f: (0, f + n_f)),
                pl.BlockSpec((CONV_W, tf), lambda i, f: (0, f)),
                pl.BlockSpec((1, tf), lambda i, f: (0, f)),
                pl.BlockSpec((tf, d), lambda i, f: (f, 0))]
    args = [x, gamma.reshape(1, d), w_in, w_in, conv_w, conv_b.reshape(1, d_ff), w_out]
    if sample:
        in_specs += [pl.BlockSpec((tm, tf), lambda i, f: (i, f))] * 2
        args += list(past)
    if final_norm:
        in_specs.append(_const_spec((1, d)))
        args.append(gamma_final.reshape(1, d))
    scratch = [pltpu.VMEM((tm, d), BF16), pltpu.VMEM((tm, d), F32)]
    if sample:
        h_shape = jax.ShapeDtypeStruct((m, d_ff), F32)
        h_spec = pl.BlockSpec((tm, tf), lambda i, f: (i, f))
    else:
        h_shape = jax.ShapeDtypeStruct((m // tm, SUBLANES, d_ff), F32)
        h_spec = pl.BlockSpec((None, SUBLANES, tf), lambda i, f: (i, 0, f))
        scratch.append(pltpu.VMEM((n_f, SUBLANES, tf), F32))
    body = functools.partial(_ffn_body, sample=sample, final_norm=final_norm, period=seq_len,
                             tiles_per_seq=max(seq_len // tm, 1), n_f=n_f)
    return pl.pallas_call(
        body,
        grid=(m // tm, n_f),
        in_specs=in_specs,
        out_specs=[pl.BlockSpec((tm, d), lambda i, f: (i, 0)), h_spec],
        out_shape=[jax.ShapeDtypeStruct((m, d), F32), h_shape],
        scratch_shapes=scratch,
        compiler_params=_cparams(("arbitrary", "arbitrary")),
        name=name,
    )(*args)


def _compress_body(*refs, n_pages, paged):
    n_main = n_pages * SUBLANES
    half = N_KV_HEADS * HEAD_DIM
    n_cb = KV_ROW // LANES
    if paged:
        page_refs = refs[1:n_pages + 2]
        pe_ref, w_ref, ck_ref, cv_ref, xs = refs[n_pages + 2:]
        for k in range(n_pages + 1):
            for kv in range(2):
                for pair in range(N_KV_HEADS // 2):
                    blk = page_refs[k][kv, 2 * pair:2 * pair + 2].reshape(PAGE, PAGE)
                    xs[kv * 2 + pair, k * PAGE:(k + 1) * PAGE, :] = blk.T

        def seg_rows(r, cb):
            return xs[cb, pl.ds(r, n_main + SUBLANES, stride=CMP_STRIDE), :]
    else:
        main_refs = refs[:n_cb]
        next_refs = refs[n_cb:2 * n_cb]
        pe_ref, w_ref, ck_ref, cv_ref = refs[2 * n_cb:]

        def seg_rows(r, cb):
            return jnp.concatenate(
                [main_refs[cb][pl.ds(r, n_main, stride=CMP_STRIDE), :],
                 next_refs[cb][pl.ds(r, SUBLANES, stride=CMP_STRIDE), :]], axis=0)

    for kv, o_ref in ((0, ck_ref), (1, cv_ref)):
        acc = None
        for r in range(CMP_STRIDE):
            seg = jnp.concatenate([seg_rows(r, 2 * kv), seg_rows(r, 2 * kv + 1)], axis=1)
            lhs = jnp.concatenate(
                [seg, pe_ref[:, r * KV_ROW + kv * half:r * KV_ROW + (kv + 1) * half]], axis=0)
            t = _dot(lhs.astype(BF16), w_ref[r, kv])
            acc = t if acc is None else acc + t
        n_tot = acc.shape[0]
        first = acc[:, :half] + acc[n_main + SUBLANES:n_main + SUBLANES + 1, :half]
        second = acc[:, half:] + acc[n_main + SUBLANES + 1:n_main + SUBLANES + 2, half:]
        second = pltpu.roll(second, n_tot - 1, 0)
        o_ref[...] = (first + second)[:n_main, :]


def _compress_rows(rows, nb, pe_seg, w_big, n_pages_step, name):
    n_p = rows.shape[0] // (nb * PAGE)
    assert n_p % n_pages_step == 0
    n_steps = n_p // n_pages_step
    half = N_KV_HEADS * HEAD_DIM
    n_main = n_pages_step * SUBLANES
    out_spec = pl.BlockSpec((None, n_main, half), lambda b, s: (b, s, 0))
    n_cb = KV_ROW // LANES
    main_specs = [pl.BlockSpec((n_pages_step * PAGE, LANES), lambda b, s, cb=cb: (b * n_steps + s, cb))
                  for cb in range(n_cb)]
    next_specs = [pl.BlockSpec((PAGE, LANES), lambda b, s, cb=cb:
                               (b * n_p + jnp.minimum((s + 1) * n_pages_step, n_p - 1), cb))
                  for cb in range(n_cb)]
    return pl.pallas_call(
        functools.partial(_compress_body, n_pages=n_pages_step, paged=False),
        grid=(nb, n_steps),
        in_specs=main_specs + next_specs + [_const_spec(pe_seg.shape), _const_spec(w_big.shape)],
        out_specs=[out_spec, out_spec],
        out_shape=[jax.ShapeDtypeStruct((nb, n_p * SUBLANES, half), F32)] * 2,
        compiler_params=_cparams(("parallel", "parallel")),
        name=name,
    )(*([rows] * (2 * n_cb)), pe_seg, w_big)


def _compress_paged(pages_t, page_table, pe_seg, w_big, n_pages_step, name):
    nb, n_p = page_table.shape
    assert n_p % n_pages_step == 0
    n_steps = n_p // n_pages_step
    half = N_KV_HEADS * HEAD_DIM
    n_main = n_pages_step * SUBLANES
    page_block = (None, 2, N_KV_HEADS, HEAD_DIM, PAGE)

    def page_map(k):
        def index(b, s, pt):
            return (pt[b * n_p + jnp.minimum(s * n_pages_step + k, n_p - 1)], 0, 0, 0, 0)
        return index

    in_specs = [pl.BlockSpec(page_block, page_map(k)) for k in range(n_pages_step + 1)]
    in_specs += [pl.BlockSpec(pe_seg.shape, lambda b, s, pt: (0, 0)),
                 pl.BlockSpec(w_big.shape, lambda b, s, pt: (0, 0, 0, 0))]
    out_spec = pl.BlockSpec((None, n_main, half), lambda b, s, pt: (b, s, 0))
    grid_spec = pltpu.PrefetchScalarGridSpec(
        num_scalar_prefetch=1, grid=(nb, n_steps), in_specs=in_specs,
        out_specs=[out_spec, out_spec],
        scratch_shapes=[pltpu.VMEM((KV_ROW // LANES, (n_pages_step + 1) * PAGE, LANES), F32)])
    return pl.pallas_call(
        functools.partial(_compress_body, n_pages=n_pages_step, paged=True),
        grid_spec=grid_spec,
        out_shape=[jax.ShapeDtypeStruct((nb, n_p * SUBLANES, half), F32)] * 2,
        compiler_params=_cparams(("parallel", "parallel")),
        name=name,
    )(page_table.reshape(-1), *([pages_t] * (n_pages_step + 1)), pe_seg, w_big)


def _cmp_select_body(q_ref, ck_ref, cv_ref, ov_ref, slope_ref, oc_ref, sel_ref, *, q_off, n_sel):
    tq = q_ref.shape[0]
    rows = HEADS_PER_KV * tq
    nc = ck_ref.shape[0]
    ns = ov_ref.shape[1]
    base = q_off + pl.program_id(1) * tq
    tok = lax.broadcasted_iota(jnp.int32, (rows, 1), 0) & (tq - 1)
    pos = base + tok
    cpos = lax.broadcasted_iota(jnp.int32, (1, nc), 1) * CMP_STRIDE + (CMP_LEN - 1)
    d_c = (pos - cpos).astype(F32)
    ok = d_c >= 0.0
    imps = []
    for g in range(N_KV_HEADS):
        qg = jnp.concatenate(
            [q_ref[:, (g * HEADS_PER_KV + h) * SLOT:(g * HEADS_PER_KV + h + 1) * SLOT]
             for h in range(HEADS_PER_KV)], axis=0)
        s = _dot_nt(qg, ck_ref[:, g * SLOT:(g + 1) * SLOT]) - slope_ref[g] * d_c
        s = jnp.where(ok, s, NEG)
        e = jnp.where(ok, jnp.exp(s - jnp.max(s, axis=-1, keepdims=True)), 0.0)
        l = jnp.sum(e, axis=-1, keepdims=True)
        p = e * jnp.where(l > 0.0, 1.0 / l, 0.0)
        o = _dot(p.astype(BF16), cv_ref[:, g * SLOT:(g + 1) * SLOT])
        psum = p[0:tq]
        for h in range(1, HEADS_PER_KV):
            psum = psum + p[h * tq:(h + 1) * tq]
        for h in range(HEADS_PER_KV):
            c0 = (g * HEADS_PER_KV + h) * SLOT
            oc_ref[:, c0:c0 + SLOT] = o[h * tq:(h + 1) * tq].astype(oc_ref.dtype)
        hi = psum.astype(BF16)
        rem = psum - hi.astype(F32)
        mid = rem.astype(BF16)
        lo = (rem - mid.astype(F32)).astype(BF16)
        ov = ov_ref[...]
        imps.append(_dot(hi, ov) + _dot(mid, ov) + _dot(lo, ov))
    imp = jnp.concatenate(imps, axis=0)
    blk = lax.broadcasted_iota(jnp.int32, (1, ns), 1)
    cur = pos >> SLC_SHIFT
    forced = (blk == 0) | (blk == cur) | (blk == cur - 1)
    imp = jnp.where(forced, FORCE, imp)
    imp = jnp.where(blk > cur, NEG, imp)
    blk_f = blk.astype(F32)
    sel = jnp.zeros(imp.shape, F32)
    for _ in range(n_sel):
        top = jnp.max(imp, axis=-1, keepdims=True)
        first = jnp.min(jnp.where(imp == top, blk_f, float(ns)), axis=-1, keepdims=True)
        hit = blk_f == first
        sel = jnp.where(hit, 1.0, sel)
        imp = jnp.where(hit, BELOW_NEG, imp)
    for g in range(N_KV_HEADS):
        sel_ref[:, g * ns:(g + 1) * ns] = sel[g * tq:(g + 1) * tq].astype(sel_ref.dtype)


def _cmp_select(q_slot, ck_slot, cv_slot, overlap, slope_rows, tq, q_off, n_sel, name):
    m = q_slot.shape[0]
    nb, nc, _ = ck_slot.shape
    ns = overlap.shape[1]
    nq = m // (nb * tq)
    assert (tq & (tq - 1)) == 0
    return pl.pallas_call(
        functools.partial(_cmp_select_body, q_off=q_off, n_sel=n_sel),
        grid=(nb, nq),
        in_specs=[pl.BlockSpec((tq, N_HEADS * SLOT), lambda b, i: (b * nq + i, 0)),
                  pl.BlockSpec((None, nc, N_KV_HEADS * SLOT), lambda b, i: (b, 0, 0)),
                  pl.BlockSpec((None, nc, N_KV_HEADS * SLOT), lambda b, i: (b, 0, 0)),
                  _const_spec(overlap.shape), _const_spec(slope_rows.shape)],
        out_specs=[pl.BlockSpec((tq, N_HEADS * SLOT), lambda b, i: (b * nq + i, 0)),
                   pl.BlockSpec((tq, N_KV_HEADS * ns), lambda b, i: (b * nq + i, 0))],
        out_shape=[jax.ShapeDtypeStruct((m, N_HEADS * SLOT), BF16),
                   jax.ShapeDtypeStruct((m, N_KV_HEADS * ns), BF16)],
        compiler_params=_cparams(("parallel", "parallel")),
        name=name,
    )(q_slot, ck_slot, cv_slot, overlap, slope_rows)


def _slc_body(q_ref, sel_ref, k_ref, vt_ref, sext_ref, pext_ref, hot_ref, o_ref,
              qa_scr, m_scr, acc_scr, *, tk):
    i = pl.program_id(1)
    tq = q_ref.shape[0]
    rows = HEADS_PER_KV * tq
    ns = hot_ref.shape[1]
    for g in range(N_KV_HEADS):
        qg = jnp.concatenate(
            [q_ref[:, (g * HEADS_PER_KV + h) * SLOT:(g * HEADS_PER_KV + h + 1) * SLOT]
             for h in range(HEADS_PER_KV)], axis=0) + sext_ref[g]
        bias = ((sel_ref[:, g * ns:(g + 1) * ns].astype(F32) - 1.0) * MASK_BIG).astype(BF16)
        qa_scr[g] = jnp.concatenate([qg, jnp.concatenate([bias] * HEADS_PER_KV, axis=0)], axis=1)
    m_scr[...] = jnp.full_like(m_scr, NEG)
    acc_scr[...] = jnp.zeros_like(acc_scr)

    def step(j, diagonal):
        k0 = pl.multiple_of(j * tk, tk)
        kext = pext_ref[pl.ds(k0, tk), :]
        hot = hot_ref[pl.ds(k0, tk), :]
        if diagonal:
            pos = i * tq + (lax.broadcasted_iota(jnp.int32, (1, rows), 1) & (tq - 1))
            causal = (k0 + lax.broadcasted_iota(jnp.int32, (tk, 1), 0)) <= pos
        ones_row = lax.broadcasted_iota(jnp.int32, (SLOT, 1), 0) == SUM_ROW

        def scores(g):
            cs = slice(g * SLOT, (g + 1) * SLOT)
            ka = jnp.concatenate([k_ref[pl.ds(k0, tk), cs] + kext, hot], axis=1)
            s = _dot_nt(ka, qa_scr[g])
            return jnp.where(causal, s, NEG) if diagonal else s

        s_next = scores(0)
        for g in range(N_KV_HEADS):
            s = s_next
            if g + 1 < N_KV_HEADS:
                s_next = scores(g + 1)
            cs = slice(g * SLOT, (g + 1) * SLOT)
            m_old = m_scr[g]
            m_new = jnp.maximum(m_old, jnp.max(s, axis=0, keepdims=True))
            p = jnp.exp(s - m_new).astype(BF16)
            vt = jnp.where(ones_row, 1.0, vt_ref[j, cs, :]).astype(BF16)
            acc_scr[g] = jnp.exp(m_old - m_new) * acc_scr[g] + _dot(vt, p)
            m_scr[g] = m_new

    n_full = (i * tq) // tk

    def full_step(j, carry):
        step(j, False)
        return carry

    lax.fori_loop(0, n_full, full_step, 0)
    step(n_full, True)
    for g in range(N_KV_HEADS):
        acc = acc_scr[g]
        o = (acc * (1.0 / acc[SUM_ROW:SUM_ROW + 1, :])).T
        for h in range(HEADS_PER_KV):
            c0 = (g * HEADS_PER_KV + h) * SLOT
            o_ref[:, c0:c0 + SLOT] = o[h * tq:(h + 1) * tq].astype(o_ref.dtype)


def _slc_attention(q_slot, sel, k_slot, vt_tiles, nb, seq_len, tq, tk, name):
    nq = seq_len // tq
    nk = seq_len // tk
    rows = HEADS_PER_KV * tq
    ns = sel.shape[1] // N_KV_HEADS
    assert seq_len % tk == 0 and tk % tq == 0 and seq_len <= ns * SLC_BLOCK
    assert vt_tiles.shape == (nb * nk, N_KV_HEADS * SLOT, tk)
    sext, pext, hot = _slc_constants(seq_len, tq, ns)
    once = pl.Buffered(1)
    return pl.pallas_call(
        functools.partial(_slc_body, tk=tk),
        grid=(nb, nq),
        in_specs=[pl.BlockSpec((tq, N_HEADS * SLOT), lambda b, i: (b * nq + i, 0)),
                  pl.BlockSpec((tq, N_KV_HEADS * ns), lambda b, i: (b * nq + i, 0)),
                  pl.BlockSpec((seq_len, N_KV_HEADS * SLOT), lambda b, i: (b, 0), pipeline_mode=once),
                  pl.BlockSpec((nk, N_KV_HEADS * SLOT, tk), lambda b, i: (b, 0, 0), pipeline_mode=once),
                  pl.BlockSpec(sext.shape, lambda b, i: (0, 0, 0), pipeline_mode=once),
                  pl.BlockSpec(pext.shape, lambda b, i: (0, 0), pipeline_mode=once),
                  pl.BlockSpec(hot.shape, lambda b, i: (0, 0), pipeline_mode=once)],
        out_specs=pl.BlockSpec((tq, N_HEADS * SLOT), lambda b, i: (b * nq + i, 0)),
        out_shape=jax.ShapeDtypeStruct(q_slot.shape, BF16),
        scratch_shapes=[pltpu.VMEM((N_KV_HEADS, rows, SLOT + ns), BF16),
                        pltpu.VMEM((N_KV_HEADS, 1, rows), F32),
                        pltpu.VMEM((N_KV_HEADS, SLOT, rows), F32)],
        compiler_params=_cparams(("parallel", "arbitrary")),
        name=name,
    )(q_slot, sel, k_slot, vt_tiles, sext, pext, hot)


def _win_body(q_ref, *refs, n_blk):
    k_refs = refs[:n_blk]
    v_refs = refs[n_blk:2 * n_blk]
    slope_ref, o_ref = refs[2 * n_blk:]
    tq = q_ref.shape[0]
    rows = HEADS_PER_KV * tq
    i = pl.program_id(1)
    tok = lax.broadcasted_iota(jnp.int32, (rows, 1), 0) & (tq - 1)
    pos = i * tq + tok
    kpos = (i - (n_blk - 1)) * tq + lax.broadcasted_iota(jnp.int32, (1, n_blk * tq), 1)
    d_w = pos - kpos
    valid = (d_w >= 0) & (d_w < WINDOW) & (kpos >= 0)
    d_f = d_w.astype(F32)
    for g in range(N_KV_HEADS):
        cs = slice(g * SLOT, (g + 1) * SLOT)
        qg = jnp.concatenate(
            [q_ref[:, (g * HEADS_PER_KV + h) * SLOT:(g * HEADS_PER_KV + h + 1) * SLOT]
             for h in range(HEADS_PER_KV)], axis=0)
        kg = jnp.concatenate([k_refs[k][:, cs] for k in range(n_blk)], axis=0)
        vg = jnp.concatenate([v_refs[k][:, cs] for k in range(n_blk)], axis=0)
        s = _dot_nt(qg, kg) - slope_ref[g] * d_f
        s = jnp.where(valid, s, NEG)
        e = jnp.where(valid, jnp.exp(s - jnp.max(s, axis=-1, keepdims=True)), 0.0)
        p = e * (1.0 / jnp.sum(e, axis=-1, keepdims=True))
        o = _dot(p.astype(BF16), vg)
        for h in range(HEADS_PER_KV):
            c0 = (g * HEADS_PER_KV + h) * SLOT
            o_ref[:, c0:c0 + SLOT] = o[h * tq:(h + 1) * tq].astype(o_ref.dtype)


def _win_attention(q_slot, k_slot, v_slot, slope_rows, nb, seq_len, tq, name):
    nq = seq_len // tq
    n_blk = WINDOW // tq + 1

    def kv_map(k):
        def index(b, i):
            return (b * nq + jnp.maximum(i - (n_blk - 1) + k, 0), 0)
        return index

    kv_specs = [pl.BlockSpec((tq, N_KV_HEADS * SLOT), kv_map(k)) for k in range(n_blk)]
    return pl.pallas_call(
        functools.partial(_win_body, n_blk=n_blk),
        grid=(nb, nq),
        in_specs=[pl.BlockSpec((tq, N_HEADS * SLOT), lambda b, i: (b * nq + i, 0))]
        + kv_specs + kv_specs + [_const_spec(slope_rows.shape)],
        out_specs=pl.BlockSpec((tq, N_HEADS * SLOT), lambda b, i: (b * nq + i, 0)),
        out_shape=jax.ShapeDtypeStruct(q_slot.shape, BF16),
        compiler_params=_cparams(("parallel", "parallel")),
        name=name,
    )(q_slot, *([k_slot] * n_blk), *([v_slot] * n_blk), slope_rows)


def _paged_attn_body(pt_ref, *refs, n_pages, n_steps, past_base, q_off, tokens, window, use_sel):
    page_refs = refs[:n_pages]
    refs = refs[n_pages:]
    new_ref, q_ref, slope_ref = refs[:3]
    refs = refs[3:]
    if use_sel:
        sel_ref = refs[0]
        refs = refs[1:]
    o_ref, m_scr, l_scr, acc_scr = refs
    s_id = pl.program_id(1)
    rows = q_ref.shape[0]
    half = N_KV_HEADS * HEAD_DIM

    @pl.when(s_id == 0)
    def _():
        m_scr[...] = jnp.full_like(m_scr, NEG)
        l_scr[...] = jnp.zeros_like(l_scr)
        acc_scr[...] = jnp.zeros_like(acc_scr)

    pos = q_off + (lax.broadcasted_iota(jnp.int32, (rows, 1), 0) & (tokens - 1))
    q = q_ref[...]
    key = lax.broadcasted_iota(jnp.int32, (1, PAGE), 1)

    def update(pages, kpos0s, blk0s):
        ss, keeps = [], []
        for ref, kpos0, blk0 in zip(pages, kpos0s, blk0s):
            d = pos - (kpos0 + key)
            valid = d >= 0
            if window is not None:
                valid = valid & (d < window)
            if use_sel:
                ns = sel_ref.shape[1]
                blk_row = lax.broadcasted_iota(jnp.int32, (ns, PAGE), 0)
                key_blk = blk0 + (lax.broadcasted_iota(jnp.int32, (ns, PAGE), 1) >> SLC_SHIFT)
                expand = jnp.where(blk_row == key_blk, 1.0, 0.0).astype(BF16)
                valid = valid & (_dot(sel_ref[...], expand) > 0.5)
            k_t = ref[0].reshape(half, PAGE).astype(BF16)
            s = _dot(q, k_t) - slope_ref[...] * d.astype(F32)
            ss.append(jnp.where(valid, s, NEG))
            keeps.append(jnp.where(valid, 1.0, 0.0))
        s = jnp.concatenate(ss, axis=1)
        keep = jnp.concatenate(keeps, axis=1)
        m_old = m_scr[...]
        m_new = jnp.maximum(m_old, jnp.max(s, axis=-1, keepdims=True))
        p = jnp.exp(s - m_new) * keep
        alpha = jnp.exp(m_old - m_new)
        l_scr[...] = alpha * l_scr[...] + jnp.sum(p, axis=-1, keepdims=True)
        p = p.astype(BF16)
        acc = alpha * acc_scr[...]
        for k, ref in enumerate(pages):
            v_t = ref[1].reshape(half, PAGE).astype(BF16)
            acc = acc + _dot_nt(p[:, k * PAGE:(k + 1) * PAGE], v_t)
        acc_scr[...] = acc
        m_scr[...] = m_new

    first_page = s_id * n_pages
    update(page_refs,
           [past_base + (first_page + k) * PAGE for k in range(n_pages)],
           [(first_page + k) * (PAGE // SLC_BLOCK) for k in range(n_pages)])

    @pl.when(s_id == n_steps - 1)
    def _():
        update([new_ref], [q_off], [(q_off - past_base) // SLC_BLOCK])
        o_ref[...] = acc_scr[...] * (1.0 / l_scr[...])


def _paged_attention(pages_t, page_index, n_p, new_t, q_bd, slope_rows, sel_rows, n_pages_step,
                     past_base, q_off, tokens, window, table, name):
    nb, rows, half = q_bd.shape
    assert n_p % n_pages_step == 0 and (tokens & (tokens - 1)) == 0
    n_steps = n_p // n_pages_step
    use_sel = sel_rows is not None
    page_block = (None,) * (pages_t.ndim - 4) + (2, N_KV_HEADS, HEAD_DIM, PAGE)

    def page_map(k):
        def index(b, s, pt):
            return page_index(b, s * n_pages_step + k, pt)
        return index

    in_specs = [pl.BlockSpec(page_block, page_map(k)) for k in range(n_pages_step)]
    in_specs += [pl.BlockSpec((None, 2, N_KV_HEADS, HEAD_DIM, PAGE), lambda b, s, pt: (b, 0, 0, 0, 0)),
                 pl.BlockSpec((None, rows, half), lambda b, s, pt: (b, 0, 0)),
                 pl.BlockSpec(slope_rows.shape, lambda b, s, pt: (0, 0))]
    args = [pages_t] * n_pages_step + [new_t, q_bd, slope_rows]
    if use_sel:
        in_specs.append(pl.BlockSpec((None,) + sel_rows.shape[1:], lambda b, s, pt: (b, 0, 0)))
        args.append(sel_rows)
    grid_spec = pltpu.PrefetchScalarGridSpec(
        num_scalar_prefetch=1, grid=(nb, n_steps), in_specs=in_specs,
        out_specs=pl.BlockSpec((None, rows, half), lambda b, s, pt: (b, 0, 0)),
        scratch_shapes=[pltpu.VMEM((rows, 1), F32), pltpu.VMEM((rows, 1), F32),
                        pltpu.VMEM((rows, half), F32)])
    body = functools.partial(_paged_attn_body, n_pages=n_pages_step, n_steps=n_steps,
                             past_base=past_base, q_off=q_off, tokens=tokens, window=window,
                             use_sel=use_sel)
    return pl.pallas_call(
        body,
        grid_spec=grid_spec,
        out_shape=jax.ShapeDtypeStruct((nb, rows, half), F32),
        compiler_params=_cparams(("parallel", "arbitrary")),
        name=name,
    )(table.reshape(-1), *args)


def _combine_body(x_ref, gates_ref, oc_ref, os_ref, ow_ref, wo_ref, o_ref, mix_scr):
    gt = jax.nn.sigmoid(gates_ref[...])
    for hh in range(N_HEADS):
        cs = slice(hh * SLOT, (hh + 1) * SLOT)
        mix = gt[:, hh:hh + 1] * oc_ref[:, cs].astype(F32)
        mix = mix + gt[:, N_HEADS + hh:N_HEADS + hh + 1] * os_ref[:, cs].astype(F32)
        mix = mix + gt[:, 2 * N_HEADS + hh:2 * N_HEADS + hh + 1] * ow_ref[:, cs].astype(F32)
        mix_scr[:, cs] = mix.astype(BF16)
    o_ref[...] = x_ref[...] + _dot(mix_scr[...], wo_ref[...])


def _combine(x, gates, o_c, o_s, o_w, w_out_slot, tm, name):
    m, d = x.shape
    wide = N_HEADS * SLOT
    row_spec = lambda w: pl.BlockSpec((tm, w), lambda i: (i, 0))
    return pl.pallas_call(
        _combine_body,
        grid=(m // tm,),
        in_specs=[row_spec(d), row_spec(gates.shape[1]), row_spec(wide), row_spec(wide), row_spec(wide),
                  _const_spec(w_out_slot.shape)],
        out_specs=row_spec(d),
        out_shape=jax.ShapeDtypeStruct((m, d), F32),
        scratch_shapes=[pltpu.VMEM((tm, wide), BF16)],
        compiler_params=_cparams(("parallel",)),
        name=name,
    )(x, gates, o_c, o_s, o_w, w_out_slot)


def _slot_cols(w):
    d, n = w.shape
    w = w.reshape(d, n // HEAD_DIM, HEAD_DIM)
    return jnp.pad(w, ((0, 0), (0, 0), (0, SLOT - HEAD_DIM))).reshape(d, (n // HEAD_DIM) * SLOT)


def _slot_rows(w):
    n, d = w.shape
    w = w.reshape(n // HEAD_DIM, HEAD_DIM, d)
    return jnp.pad(w, ((0, 0), (0, SLOT - HEAD_DIM), (0, 0))).reshape((n // HEAD_DIM) * SLOT, d)


def _slot_last(a):
    lead = a.shape[:-1]
    n = a.shape[-1] // HEAD_DIM
    a = a.reshape(lead + (n, HEAD_DIM))
    a = jnp.pad(a, [(0, 0)] * len(lead) + [(0, 0), (0, SLOT - HEAD_DIM)])
    return a.reshape(lead + (n * SLOT,))


def _alibi_slopes():
    h = np.arange(1, N_HEADS + 1, dtype=np.float32)
    return np.asarray(2.0 ** (-8.0 * h / N_HEADS), dtype=np.float32).reshape(N_KV_HEADS, HEADS_PER_KV)


def _slope_rows_by_group(tq):
    return jnp.asarray(np.repeat(_alibi_slopes(), tq, axis=1)[:, :, None])


def _bf16_parts(x):
    x = np.asarray(x, np.float32)
    parts = []
    for _ in range(3):
        part = x.astype(jnp.bfloat16)
        parts.append(part)
        x = x - part.astype(np.float32)
    assert not np.any(x)
    return parts


def _slc_constants(seq_len, tq, ns):
    assert seq_len <= 256 * LANES
    parts = _bf16_parts(np.repeat(_alibi_slopes(), tq, axis=1))
    sext = np.zeros((N_KV_HEADS, HEADS_PER_KV * tq, SLOT), jnp.bfloat16)
    kpos = np.arange(seq_len)
    pext = np.zeros((seq_len, SLOT), np.float32)
    for n, part in enumerate(parts):
        sext[:, :, ALIBI_LANE + 2 * n] = part
        sext[:, :, ALIBI_LANE + 2 * n + 1] = part
        pext[:, ALIBI_LANE + 2 * n] = (kpos // LANES) * LANES
        pext[:, ALIBI_LANE + 2 * n + 1] = kpos % LANES
    hot = (kpos[:, None] // SLC_BLOCK) == np.arange(ns)[None, :]
    return jnp.asarray(sext), jnp.asarray(pext, dtype=BF16), jnp.asarray(hot.astype(np.float32), dtype=BF16)


def _overlap(n_cmp, n_slc, nc, ns):
    st = np.arange(nc) * CMP_STRIDE
    bs = np.arange(ns) * SLC_BLOCK
    m = (st[:, None] < bs[None, :] + SLC_BLOCK) & (st[:, None] + CMP_LEN > bs[None, :])
    m = m & (np.arange(nc)[:, None] < n_cmp) & (np.arange(ns)[None, :] < n_slc)
    return jnp.asarray(m.astype(np.float32), dtype=BF16)


def _compress_weights(cmp_pe, cmp_w):
    eye = jnp.eye(N_KV_HEADS, dtype=F32)
    w = cmp_w.reshape(2, 2, CMP_STRIDE, HEAD_DIM, HEAD_DIM)
    bd = jnp.einsum('kmrde,gh->rkgdmhe', w, eye)
    half = N_KV_HEADS * HEAD_DIM
    w_big = bd.reshape(CMP_STRIDE, 2, half, 2 * half).astype(BF16)
    pe = cmp_pe.reshape(2, 2, CMP_STRIDE, HEAD_DIM)
    pe = jnp.broadcast_to(pe.transpose(1, 2, 0, 3)[:, :, :, None, :],
                          (2, CMP_STRIDE, 2, N_KV_HEADS, HEAD_DIM)).reshape(2, SEG_W)
    pe_seg = jnp.pad(pe, ((0, SUBLANES - 2), (0, 0)))
    return pe_seg, w_big


def _nsa_weights(w_in, segs_slot_kv):
    q_dim = N_HEADS * HEAD_DIM
    half = N_KV_HEADS * HEAD_DIM
    w_q = w_in[:, :q_dim] * (HEAD_DIM ** -0.5)
    w_gate = w_in[:, q_dim + N_BRANCH * KV_ROW:]
    parts = [_slot_cols(w_q), w_in[:, q_dim:q_dim + N_BRANCH * KV_ROW],
             jnp.pad(w_gate, ((0, 0), (0, LANES - w_gate.shape[1])))]
    segs = [(N_HEADS * SLOT, BF16, False), (KV_ROW, F32, False), (KV_ROW, F32, False),
            (KV_ROW, F32, False), (LANES, F32, False)]
    w_t = None
    if segs_slot_kv:
        o_slc = q_dim + KV_ROW
        o_win = q_dim + 2 * KV_ROW
        parts += [_slot_cols(w_in[:, o_slc:o_slc + half]), _slot_cols(w_in[:, o_win:o_win + half]),
                  _slot_cols(w_in[:, o_win + half:o_win + KV_ROW])]
        segs += [(N_KV_HEADS * SLOT, BF16, False)] * 3
        w_t = _slot_cols(w_in[:, o_slc + half:o_slc + KV_ROW]).T.astype(BF16)
    return jnp.concatenate(parts, axis=1).astype(BF16), segs, w_t


def _sgu_layer(x, p, a, seq_len, tm, emit_v, tag):
    d = x.shape[1]
    d_sgu = p['sgu_w_out'].shape[1]
    gw = d_sgu // N_SGU_GROUPS
    segs = [(d_sgu, BF16, True), (d_sgu, BF16, True)]
    u, v = _proj(x, p['norm_mix'][0], p['sgu_w_in'][a].astype(BF16), p['sgu_b_in'][a].reshape(1, -1),
                 segs, tm, f"sgu_in_{tag}")
    w_s, b_s = p['sgu_w_s'][a], p['sgu_b_s'][a]
    if seq_len % CHUNK == 0:
        bs_rows = b_s.T
    else:
        assert CHUNK % seq_len == 0
        rep = CHUNK // seq_len
        w_s = jnp.einsum('ab,gts->gatbs', jnp.eye(rep, dtype=F32), w_s[:, :seq_len, :seq_len])
        w_s = w_s.reshape(N_SGU_GROUPS, CHUNK, CHUNK)
        bs_rows = jnp.tile(b_s[:, :seq_len].T, (rep, 1))
    bs_exp = jnp.repeat(bs_rows, gw, axis=1)
    return _sgu(x, u, v, p['sgu_v_norm'][a], w_s, bs_exp, p['sgu_w_out'][a].astype(BF16), tm, emit_v,
                f"sgu_{tag}")


def _ffn_layer(x, p, layer, seq_len, past, final, tm, tag):
    gamma_final = p['norm_final'] if final else None
    return _ffn(x, p['norm_ffn'][layer], p['ffn_w_in'][layer].astype(BF16), p['ffn_conv_w'][layer],
                p['ffn_conv_b'][layer], p['ffn_w_out'][layer].astype(BF16), seq_len, past, gamma_final,
                tm, 256, f"ffn{layer}_{tag}")


def _nsa_prompt(x, p, b_idx, nb, seq_len, tm):
    w_ext, segs, w_t = _nsa_weights(p['nsa_w_in'][b_idx], True)
    q_slot, kv_cmp, kv_slc, kv_win, gates, ks_slot, kw_slot, vw_slot, vs_tiles = _proj(
        x, p['norm_mix'][1], w_ext, None, segs, tm, "nsa_in_prompt", w_t)
    n_pages = seq_len // PAGE
    pe_seg, w_big = _compress_weights(p['nsa_cmp_pe'][b_idx], p['nsa_cmp_w'][b_idx])
    ck, cv = _compress_rows(kv_cmp, nb, pe_seg, w_big, min(n_pages, 16), "compress_prompt")
    n_seg = seq_len // CMP_STRIDE
    n_cmp = n_seg - CMP_LEN // CMP_STRIDE + 1
    n_slc = -(-seq_len // SLC_BLOCK)
    ns = -(-n_slc // LANES) * LANES
    tq = 128
    slope_rows = _slope_rows_by_group(tq)
    o_c, sel = _cmp_select(q_slot, _slot_last(ck).astype(BF16), _slot_last(cv).astype(BF16),
                           _overlap(n_cmp, n_slc, n_seg, ns), slope_rows, tq, 0,
                           min(SLC_TOP_N, n_slc), "cmp_select_prompt")
    o_s = _slc_attention(q_slot, sel, ks_slot, vs_tiles, nb, seq_len, tq, tm, "slc_prompt")
    o_w = _win_attention(q_slot, kw_slot, vw_slot, slope_rows, nb, seq_len, tq, "win_prompt")
    x = _combine(x, gates, o_c, o_s, o_w, _slot_rows(p['nsa_w_out'][b_idx]).astype(BF16), tm,
                 "nsa_out_prompt")
    kv_shape = (nb, seq_len, 2, N_KV_HEADS, HEAD_DIM)
    kv_win = kv_win.reshape(kv_shape)
    return x, kv_cmp.reshape(kv_shape), kv_slc.reshape(kv_shape), kv_win[:, -min(WINDOW, seq_len):]


def _rows_ght(a, tokens_pad):
    nb, t, _, w = a.shape
    a = a.reshape(nb, t, N_KV_HEADS, HEADS_PER_KV, w).transpose(0, 2, 3, 1, 4)
    a = jnp.pad(a, ((0, 0), (0, 0), (0, 0), (0, tokens_pad - t), (0, 0)))
    return a.reshape(nb, N_HEADS * tokens_pad, w)


def _nsa_sample(x, p, b_idx, nb, t_new, past_len, cache_cmp, cache_slc, state_win, page_table):
    m = x.shape[0]
    half = N_KV_HEADS * HEAD_DIM
    t_pad = BF16_SUBLANES
    assert t_new <= t_pad and past_len % PAGE == 0
    w_ext, segs, _ = _nsa_weights(p['nsa_w_in'][b_idx], False)
    q_slot, kv_cmp, kv_slc, kv_win, gates = _proj(x, p['norm_mix'][1], w_ext, None, segs, m, "nsa_in_sample")
    n_pool = cache_cmp.shape[0]
    n_pages = past_len // PAGE
    total = past_len + t_new
    n_seg = total // CMP_STRIDE
    assert n_seg == past_len // CMP_STRIDE, "new rows must not complete a compression segment"
    n_cmp = n_seg - CMP_LEN // CMP_STRIDE + 1
    pe_seg, w_big = _compress_weights(p['nsa_cmp_pe'][b_idx], p['nsa_cmp_w'][b_idx])
    rows_last = (0, 2, 3, 4, 1)
    ck, cv = _compress_paged(cache_cmp.transpose(rows_last), page_table, pe_seg, w_big,
                             min(n_pages, 16), "compress_sample")
    n_slc = -(-total // SLC_BLOCK)
    ns = -(-n_slc // LANES) * LANES
    q4 = q_slot.reshape(nb, t_new, N_HEADS * SLOT)
    q_pad = jnp.pad(q4, ((0, 0), (0, t_pad - t_new), (0, 0))).reshape(nb * t_pad, N_HEADS * SLOT)
    o_c, sel = _cmp_select(q_pad, _slot_last(ck).astype(BF16), _slot_last(cv).astype(BF16),
                           _overlap(n_cmp, n_slc, n_seg, ns), _slope_rows_by_group(t_pad), t_pad,
                           past_len, min(SLC_TOP_N, n_slc), "cmp_select_sample")
    o_c = o_c.reshape(nb, t_pad, N_HEADS * SLOT)[:, :t_new].reshape(m, N_HEADS * SLOT)
    assert (t_new & (t_new - 1)) == 0
    qh = q4.reshape(nb, t_new, N_HEADS, SLOT)[..., :HEAD_DIM]
    q_rows = _rows_ght(qh, t_new).reshape(nb, N_KV_HEADS, HEADS_PER_KV * t_new, HEAD_DIM)
    q_bd = jnp.einsum('bgrd,gk->bgrkd', q_rows, jnp.eye(N_KV_HEADS, dtype=BF16))
    q_bd = q_bd.reshape(nb, N_HEADS * t_new, half)
    slope_rows = jnp.asarray(np.repeat(_alibi_slopes().reshape(-1), t_new)[:, None])
    sel_rows = sel.reshape(nb, t_pad, N_KV_HEADS, 1, ns)[:, :t_new].transpose(0, 2, 3, 1, 4)
    sel_rows = jnp.broadcast_to(sel_rows, (nb, N_KV_HEADS, HEADS_PER_KV, t_new, ns))
    sel_rows = sel_rows.reshape(nb, N_HEADS * t_new, ns)

    def new_page(kv):
        kv = kv.reshape(nb, t_new, 2, N_KV_HEADS, HEAD_DIM).transpose(rows_last)
        return jnp.pad(kv, ((0, 0),) * 4 + ((0, PAGE - t_new),))

    o_s = _paged_attention(cache_slc.transpose(rows_last), lambda b, pg, pt: (pt[b * n_pages + pg], 0, 0, 0, 0),
                           n_pages, new_page(kv_slc), q_bd, slope_rows, sel_rows, min(n_pages, 16),
                           0, past_len, t_new, None, page_table, "slc_sample")
    n_win = state_win.shape[1]
    assert n_win % PAGE == 0
    win_pages = n_win // PAGE
    o_w = _paged_attention(state_win.transpose(rows_last), lambda b, pg, pt: (b, 0, 0, 0, pg),
                           win_pages, new_page(kv_win), q_bd, slope_rows, None, win_pages,
                           past_len - n_win, past_len, t_new, WINDOW, page_table, "win_sample")

    def to_slot(o):
        o = o.reshape(nb, N_KV_HEADS, HEADS_PER_KV, t_new, N_KV_HEADS, HEAD_DIM)
        o = jnp.stack([o[:, g, :, :, g, :] for g in range(N_KV_HEADS)], axis=1)
        o = o.transpose(0, 3, 1, 2, 4).reshape(m, N_HEADS * HEAD_DIM)
        return _slot_last(o).astype(BF16)

    x = _combine(x, gates, o_c, to_slot(o_s), to_slot(o_w), _slot_rows(p['nsa_w_out'][b_idx]).astype(BF16),
                 m, "nsa_out_sample")
    kv_shape = (nb, t_new, 2, N_KV_HEADS, HEAD_DIM)
    kv_win5 = kv_win.reshape(kv_shape)
    all_win = jnp.concatenate([state_win.reshape((nb, n_win) + kv_shape[2:]), kv_win5], axis=1)
    return x, kv_cmp.reshape(kv_shape), kv_slc.reshape(kv_shape), all_win[:, -min(WINDOW, n_win + t_new):]


def _run_prompt(x_prompt, p):
    nb, seq_len, d = x_prompt.shape
    x = x_prompt.reshape(nb * seq_len, d)
    tm = min(512, seq_len)
    tm_ffn = min(1024, seq_len)
    x = _sgu_layer(x, p, 0, seq_len, tm, False, "prompt")[0]
    x, h0 = _ffn_layer(x, p, 0, seq_len, None, False, tm_ffn, "prompt")
    x, kc, ks, kw = _nsa_prompt(x, p, 0, nb, seq_len, tm)
    x, h1 = _ffn_layer(x, p, 1, seq_len, None, True, tm_ffn, "prompt")
    tiles = seq_len // tm_ffn

    def conv_state(h):
        return h.reshape(nb, tiles, SUBLANES, -1)[:, -1, SUBLANES - (CONV_W - 1):]

    return (x.reshape(nb, seq_len, d), kc[None], ks[None], kw[None],
            jnp.stack([conv_state(h0), conv_state(h1)]))


def _run_sample(x_sample, cache_cmp, cache_slc, state_win, state_conv, page_table, p):
    nb, t_new, d = x_sample.shape
    m = nb * t_new
    past_len = page_table.shape[1] * PAGE
    x = x_sample.reshape(m, d)
    d_ff = state_conv.shape[-1]

    def conv_past(layer):
        st = state_conv[layer]
        zeros = jnp.zeros((nb, t_new, d_ff), F32)
        prev1 = zeros.at[:, 0].set(st[:, 1])
        prev2 = zeros.at[:, 0].set(st[:, 0]).at[:, 1].set(st[:, 1])
        return prev1.reshape(m, d_ff), prev2.reshape(m, d_ff)

    def conv_state(h):
        return h.reshape(nb, t_new, d_ff)[:, -(CONV_W - 1):]

    x, vn = _sgu_layer(x, p, 0, t_new, m, True, "sample")
    x, h0 = _ffn_layer(x, p, 0, t_new, conv_past(0), False, m, "sample")
    x, kc, ks, kw = _nsa_sample(x, p, 0, nb, t_new, past_len, cache_cmp[0], cache_slc[0], state_win[0],
                                page_table)
    x, h1 = _ffn_layer(x, p, 1, t_new, conv_past(1), True, m, "sample")
    return (x.reshape(nb, t_new, d), kc[None], ks[None], kw[None],
            jnp.stack([conv_state(h0), conv_state(h1)]), vn.reshape(nb, t_new, -1)[None])


def kernel(x_prompt, x_sample, cache_cmp_kv, cache_slc_kv, state_win_kv, state_ffn_conv, page_table,
           norm_mix, norm_ffn, norm_final, sgu_w_in, sgu_b_in, sgu_v_norm, sgu_w_s, sgu_b_s, sgu_w_out,
           nsa_w_in, nsa_cmp_pe, nsa_cmp_w, nsa_w_out, ffn_w_in, ffn_conv_w, ffn_conv_b, ffn_w_out):
    p = {'norm_mix': norm_mix, 'norm_ffn': norm_ffn, 'norm_final': norm_final,
         'sgu_w_in': sgu_w_in, 'sgu_b_in': sgu_b_in, 'sgu_v_norm': sgu_v_norm, 'sgu_w_s': sgu_w_s,
         'sgu_b_s': sgu_b_s, 'sgu_w_out': sgu_w_out, 'nsa_w_in': nsa_w_in, 'nsa_cmp_pe': nsa_cmp_pe,
         'nsa_cmp_w': nsa_cmp_w, 'nsa_w_out': nsa_w_out, 'ffn_w_in': ffn_w_in, 'ffn_conv_w': ffn_conv_w,
         'ffn_conv_b': ffn_conv_b, 'ffn_w_out': ffn_w_out}
    y_p, p_cmp, p_slc, p_win, p_conv = _run_prompt(x_prompt, p)
    y_s, s_cmp, s_slc, s_win, s_conv, s_v = _run_sample(
        x_sample, cache_cmp_kv, cache_slc_kv, state_win_kv, state_ffn_conv, page_table, p)
    return (y_p, y_s, p_cmp, p_slc, p_win, p_conv, s_cmp, s_slc, s_win, s_conv, s_v)
```

```python
import functools

import numpy as np
import jax
import jax.numpy as jnp
from jax import lax
from jax.experimental import pallas as pl
from jax.experimental.pallas import tpu as pltpu

F32 = jnp.float32
BF16 = jnp.bfloat16

EPS = 1e-6
NEG = -1e30
FORCE = 1e9
BELOW_NEG = -3e38
MASK_BIG = 2.0 ** 30

CHUNK = 128
N_SGU_GROUPS = 8
HEAD_DIM = 64
N_KV_HEADS = 4
HEADS_PER_KV = 4
N_HEADS = N_KV_HEADS * HEADS_PER_KV
N_BRANCH = 3
CMP_LEN = 32
CMP_STRIDE = 16
SLC_BLOCK = 64
SLC_SHIFT = SLC_BLOCK.bit_length() - 1
SLC_TOP_N = 16
WINDOW = 512
CONV_W = 3
PAGE = 128
KV_ROW = 2 * N_KV_HEADS * HEAD_DIM
SEG_W = CMP_STRIDE * KV_ROW

LANES = 128
SUBLANES = 8
BF16_SUBLANES = 16
SLOT = LANES
ALIBI_LANE = HEAD_DIM
SUM_ROW = HEAD_DIM
V7X_VMEM_BYTES = 64 * 1024 * 1024
VMEM_LIMIT = (V7X_VMEM_BYTES * 3) // 4


def _cparams(sem):
    return pltpu.CompilerParams(dimension_semantics=sem, vmem_limit_bytes=VMEM_LIMIT)


def _const_spec(shape):
    nd = len(shape)
    return pl.BlockSpec(shape, lambda *_: (0,) * nd)


def _rms(x, g):
    return x * lax.rsqrt(jnp.mean(x * x, axis=-1, keepdims=True) + EPS) * g


def _dot(a, b):
    return jnp.dot(a, b, preferred_element_type=F32)


def _dot_nt(a, b):
    return lax.dot_general(a, b, (((1,), (1,)), ((), ())), preferred_element_type=F32)


def _proj_body(x_ref, g_ref, w_ref, b_ref, *o_refs, segs, chunk, has_bias, t_outs):
    xn = _rms(x_ref[...], g_ref[...]).astype(BF16)
    if t_outs:
        wt_ref = o_refs[0]
        ot_refs = o_refs[len(o_refs) - len(t_outs):]
        o_refs = o_refs[1:len(o_refs) - len(t_outs)]
        tm = xn.shape[0]
        for (r0, n_rows, tile), ot_ref in zip(t_outs, ot_refs):
            rt = _dot_nt(wt_ref[r0:r0 + n_rows, :], xn)
            for k in range(tm // tile):
                ot_ref[k] = rt[:, k * tile:(k + 1) * tile].astype(ot_ref.dtype)
    off = 0
    for (width, _, act), o_ref in zip(segs, o_refs):
        for c0 in range(0, width, chunk):
            cw = min(chunk, width - c0)
            r = _dot(xn, w_ref[:, off + c0:off + c0 + cw])
            if has_bias:
                r = r + b_ref[:, off + c0:off + c0 + cw]
            if act:
                r = jax.nn.gelu(r)
            o_ref[:, c0:c0 + cw] = r.astype(o_ref.dtype)
        off += width


def _proj(x, gamma, w, bias, segs, tm, name, w_t=None, t_outs=()):
    m, d = x.shape
    n = w.shape[1]
    assert sum(s[0] for s in segs) == n and m % tm == 0
    has_bias = bias is not None
    if bias is None:
        bias = jnp.zeros((1, LANES), F32)
    in_specs = [pl.BlockSpec((tm, d), lambda i: (i, 0)),
                _const_spec((1, d)), _const_spec(w.shape), _const_spec(bias.shape)]
    out_specs = [pl.BlockSpec((tm, s[0]), lambda i: (i, 0)) for s in segs]
    out_shape = [jax.ShapeDtypeStruct((m, s[0]), s[1]) for s in segs]
    args = [x, gamma.reshape(1, d), w, bias]
    if t_outs:
        in_specs.append(_const_spec(w_t.shape))
        args.append(w_t)
        for _, n_rows, tile in t_outs:
            assert tm % tile == 0
            out_specs.append(pl.BlockSpec((tm // tile, n_rows, tile), lambda i: (i, 0, 0)))
            out_shape.append(jax.ShapeDtypeStruct((m // tile, n_rows, tile), BF16))
    body = functools.partial(_proj_body, segs=tuple(segs), chunk=512, has_bias=has_bias,
                             t_outs=tuple(t_outs))
    return pl.pallas_call(
        body,
        grid=(m // tm,),
        in_specs=in_specs,
        out_specs=out_specs,
        out_shape=out_shape,
        compiler_params=_cparams(("parallel",)),
        name=name,
    )(*args)


def _sgu_body(x_ref, u_ref, v_ref, vg_ref, ws_ref, bs_ref, wo_ref, *rest, emit_v):
    if emit_v:
        o_ref, vn_ref, vn_scr, a_scr = rest
    else:
        o_ref, vn_scr, a_scr = rest
    tm = x_ref.shape[0]
    gw = v_ref.shape[1] // N_SGU_GROUPS
    vn = _rms(v_ref[...].astype(F32), vg_ref[...])
    if emit_v:
        vn_ref[...] = vn
    vn_scr[...] = vn.astype(BF16)
    row = lax.broadcasted_iota(jnp.int32, (CHUNK, CHUNK), 0)
    col = lax.broadcasted_iota(jnp.int32, (CHUNK, CHUNK), 1)
    for g in range(N_SGU_GROUPS):
        wc = jnp.where(col <= row, ws_ref[g], 0.0).astype(BF16)
        for c in range(tm // CHUNK):
            rs = slice(c * CHUNK, (c + 1) * CHUNK)
            cs = slice(g * gw, (g + 1) * gw)
            s = _dot(wc, vn_scr[rs, cs]) + bs_ref[:, cs]
            a_scr[rs, cs] = (u_ref[rs, cs].astype(F32) * s).astype(BF16)
    o_ref[...] = x_ref[...] + _dot(a_scr[...], wo_ref[...])


def _sgu(x, u, v, v_norm, w_s, bs_exp, w_out, tm, emit_v, name):
    m, d = x.shape
    ds = u.shape[1]
    out_shape = [jax.ShapeDtypeStruct((m, d), F32)]
    out_specs = [pl.BlockSpec((tm, d), lambda i: (i, 0))]
    if emit_v:
        out_shape.append(jax.ShapeDtypeStruct((m, ds), F32))
        out_specs.append(pl.BlockSpec((tm, ds), lambda i: (i, 0)))
    return pl.pallas_call(
        functools.partial(_sgu_body, emit_v=emit_v),
        grid=(m // tm,),
        in_specs=[pl.BlockSpec((tm, d), lambda i: (i, 0)),
                  pl.BlockSpec((tm, ds), lambda i: (i, 0)),
                  pl.BlockSpec((tm, ds), lambda i: (i, 0)),
                  _const_spec((1, ds)), _const_spec(w_s.shape), _const_spec(bs_exp.shape),
                  _const_spec(w_out.shape)],
        out_specs=out_specs,
        out_shape=out_shape,
        scratch_shapes=[pltpu.VMEM((tm, ds), BF16), pltpu.VMEM((tm, ds), BF16)],
        compiler_params=_cparams(("parallel",)),
        name=name,
    )(x, u, v, v_norm.reshape(1, ds), w_s, bs_exp, w_out)


def _ffn_body(*refs, sample, final_norm, period, tiles_per_seq, n_f):
    refs = list(refs)
    x_ref, g_ref, wh_ref, wg_ref, cw_ref, cb_ref, wo_ref = refs[:7]
    refs = refs[7:]
    if sample:
        p1_ref, p2_ref = refs[:2]
        refs = refs[2:]
    if final_norm:
        gf_ref = refs[0]
        refs = refs[1:]
    o_ref, h_ref, xn_scr, acc_scr = refs[:4]
    i = pl.program_id(0)
    f = pl.program_id(1)
    tm = x_ref.shape[0]

    @pl.when(f == 0)
    def _():
        xn_scr[...] = _rms(x_ref[...], g_ref[...]).astype(BF16)
        acc_scr[...] = jnp.zeros_like(acc_scr)

    xn = xn_scr[...]
    h = _dot(xn, wh_ref[...])
    gate = _dot(xn, wg_ref[...])
    r = lax.broadcasted_iota(jnp.int32, (tm, 1), 0)
    if sample:
        r = r & (period - 1)
        prev1 = p1_ref[...]
        prev2 = p2_ref[...]
        h_ref[...] = h
    else:
        carry_scr = refs[4]

        @pl.when(i % tiles_per_seq == 0)
        def _():
            carry_scr[f] = jnp.zeros(carry_scr.shape[1:], F32)

        prev = carry_scr[f]
        c0 = prev[SUBLANES - 2:SUBLANES - 1, :]
        c1 = prev[SUBLANES - 1:SUBLANES, :]
        prev1 = c1
        prev2 = jnp.where(r == 1, c1, c0)
        tail = h[tm - SUBLANES:tm, :]
        carry_scr[f] = tail
        h_ref[...] = tail
    hm1 = jnp.where(r >= 1, pltpu.roll(h, 1, 0), prev1)
    hm2 = jnp.where(r >= 2, pltpu.roll(h, 2, 0), prev2)
    hc = cb_ref[...] + cw_ref[0:1, :] * hm2
    hc = hc + cw_ref[1:2, :] * hm1
    hc = hc + cw_ref[2:3, :] * h
    act = (jax.nn.gelu(hc) * gate).astype(BF16)
    acc_scr[...] += _dot(act, wo_ref[...])

    @pl.when(f == n_f - 1)
    def _():
        y = x_ref[...] + acc_scr[...]
        if final_norm:
            y = _rms(y, gf_ref[...])
        o_ref[...] = y


def _ffn(x, gamma, w_in, conv_w, conv_b, w_out, seq_len, past, gamma_final, tm, tf, name):
    m, d = x.shape
    d_ff = w_out.shape[0]
    assert m % tm == 0 and d_ff % tf == 0
    n_f = d_ff // tf
    sample = past is not None
    final_norm = gamma_final is not None
    if sample:
        assert tm == m and (seq_len & (seq_len - 1)) == 0
    else:
        assert seq_len % tm == 0
    in_specs = [pl.BlockSpec((tm, d), lambda i, f: (i, 0)),
                _const_spec((1, d)),
                pl.BlockSpec((d, tf), lambda i, f: (0, f)),
                pl.BlockSpec((d, tf), lambda i, f: (0, f + n_f)),
                pl.BlockSpec((CONV_W, tf), lambda i, f: (0, f)),
                pl.BlockSpec((1, tf), lambda i, f: (0, f)),
                pl.BlockSpec((tf, d), lambda i, f: (f, 0))]
    args = [x, gamma.reshape(1, d), w_in, w_in, conv_w, conv_b.reshape(1, d_ff), w_out]
    if sample:
        in_specs += [pl.BlockSpec((tm, tf), lambda i, f: (i, f))] * 2
        args += list(past)
    if final_norm:
        in_specs.append(_const_spec((1, d)))
        args.append(gamma_final.reshape(1, d))
    scratch = [pltpu.VMEM((tm, d), BF16), pltpu.VMEM((tm, d), F32)]
    if sample:
        h_shape = jax.ShapeDtypeStruct((m, d_ff), F32)
        h_spec = pl.BlockSpec((tm, tf), lambda i, f: (i, f))
    else:
        h_shape = jax.ShapeDtypeStruct((m // tm, SUBLANES, d_ff), F32)
        h_spec = pl.BlockSpec((None, SUBLANES, tf), lambda i, f: (i, 0, f))
        scratch.append(pltpu.VMEM((n_f, SUBLANES, tf), F32))
    body = functools.partial(_ffn_body, sample=sample, final_norm=final_norm, period=seq_len,
                             tiles_per_seq=max(seq_len // tm, 1), n_f=n_f)
    return pl.pallas_call(
        body,
        grid=(m // tm, n_f),
        in_specs=in_specs,
        out_specs=[pl.BlockSpec((tm, d), lambda i, f: (i, 0)), h_spec],
        out_shape=[jax.ShapeDtypeStruct((m, d), F32), h_shape],
        scratch_shapes=scratch,
        compiler_params=_cparams(("arbitrary", "arbitrary")),
        name=name,
    )(*args)


def _compress_body(*refs, n_pages, paged):
    n_main = n_pages * SUBLANES
    half = N_KV_HEADS * HEAD_DIM
    n_cb = KV_ROW // LANES
    if paged:
        page_refs = refs[1:n_pages + 2]
        pe_ref, w_ref, ck_ref, cv_ref, xs = refs[n_pages + 2:]
        out_row = lax.broadcasted_iota(jnp.int32, (PAGE, PAGE), 0)
        src_row = lax.broadcasted_iota(jnp.int32, (PAGE, PAGE), 1)
        wanted = (out_row & (SUBLANES - 1)) * CMP_STRIDE + (out_row >> (SUBLANES.bit_length() - 1))
        perm = jnp.where(src_row == wanted, 1.0, 0.0).astype(BF16)
        for k in range(n_pages + 1):
            for kv in range(2):
                page_t = page_refs[k][kv].reshape(half, PAGE).astype(BF16)
                xs[kv, k * PAGE:(k + 1) * PAGE, :] = _dot_nt(perm, page_t)

        def seg_rows(r, kv):
            return jnp.concatenate(
                [xs[kv, k * PAGE + r * SUBLANES:k * PAGE + (r + 1) * SUBLANES, :]
                 for k in range(n_pages + 1)], axis=0)
    else:
        main_refs = refs[:n_cb]
        next_refs = refs[n_cb:2 * n_cb]
        pe_ref, w_ref, ck_ref, cv_ref = refs[2 * n_cb:]

        def seg_rows(r, kv):
            return jnp.concatenate(
                [jnp.concatenate([main_refs[cb][pl.ds(r, n_main, stride=CMP_STRIDE), :],
                                  next_refs[cb][pl.ds(r, SUBLANES, stride=CMP_STRIDE), :]], axis=0)
                 for cb in (2 * kv, 2 * kv + 1)], axis=1)

    for kv, o_ref in ((0, ck_ref), (1, cv_ref)):
        acc = None
        for r in range(CMP_STRIDE):
            lhs = jnp.concatenate(
                [seg_rows(r, kv), pe_ref[:, r * KV_ROW + kv * half:r * KV_ROW + (kv + 1) * half]], axis=0)
            t = _dot(lhs.astype(BF16), w_ref[r, kv])
            acc = t if acc is None else acc + t
        n_tot = acc.shape[0]
        first = acc[:, :half] + acc[n_main + SUBLANES:n_main + SUBLANES + 1, :half]
        second = acc[:, half:] + acc[n_main + SUBLANES + 1:n_main + SUBLANES + 2, half:]
        second = pltpu.roll(second, n_tot - 1, 0)
        o_ref[...] = (first + second)[:n_main, :]


def _compress_rows(rows, nb, pe_seg, w_big, n_pages_step, name):
    n_p = rows.shape[0] // (nb * PAGE)
    assert n_p % n_pages_step == 0
    n_steps = n_p // n_pages_step
    half = N_KV_HEADS * HEAD_DIM
    n_main = n_pages_step * SUBLANES
    out_spec = pl.BlockSpec((None, n_main, half), lambda b, s: (b, s, 0))
    n_cb = KV_ROW // LANES
    main_specs = [pl.BlockSpec((n_pages_step * PAGE, LANES), lambda b, s, cb=cb: (b * n_steps + s, cb))
                  for cb in range(n_cb)]
    next_specs = [pl.BlockSpec((PAGE, LANES), lambda b, s, cb=cb:
                               (b * n_p + jnp.minimum((s + 1) * n_pages_step, n_p - 1), cb))
                  for cb in range(n_cb)]
    return pl.pallas_call(
        functools.partial(_compress_body, n_pages=n_pages_step, paged=False),
        grid=(nb, n_steps),
        in_specs=main_specs + next_specs + [_const_spec(pe_seg.shape), _const_spec(w_big.shape)],
        out_specs=[out_spec, out_spec],
        out_shape=[jax.ShapeDtypeStruct((nb, n_p * SUBLANES, half), F32)] * 2,
        compiler_params=_cparams(("parallel", "parallel")),
        name=name,
    )(*([rows] * (2 * n_cb)), pe_seg, w_big)


def _compress_paged(pages_t, page_table, pe_seg, w_big, n_pages_step, name):
    nb, n_p = page_table.shape
    assert n_p % n_pages_step == 0
    n_steps = n_p // n_pages_step
    half = N_KV_HEADS * HEAD_DIM
    n_main = n_pages_step * SUBLANES
    page_block = (None, 2, N_KV_HEADS, HEAD_DIM, PAGE)

    def page_map(k):
        def index(b, s, pt):
            return (pt[b * n_p + jnp.minimum(s * n_pages_step + k, n_p - 1)], 0, 0, 0, 0)
        return index

    in_specs = [pl.BlockSpec(page_block, page_map(k)) for k in range(n_pages_step + 1)]
    in_specs += [pl.BlockSpec(pe_seg.shape, lambda b, s, pt: (0, 0), pipeline_mode=pl.Buffered(1)),
                 pl.BlockSpec(w_big.shape, lambda b, s, pt: (0, 0, 0, 0), pipeline_mode=pl.Buffered(1))]
    out_spec = pl.BlockSpec((None, n_main, half), lambda b, s, pt: (b, s, 0))
    grid_spec = pltpu.PrefetchScalarGridSpec(
        num_scalar_prefetch=1, grid=(nb, n_steps), in_specs=in_specs,
        out_specs=[out_spec, out_spec],
        scratch_shapes=[pltpu.VMEM((2, (n_pages_step + 1) * PAGE, half), F32)])
    return pl.pallas_call(
        functools.partial(_compress_body, n_pages=n_pages_step, paged=True),
        grid_spec=grid_spec,
        out_shape=[jax.ShapeDtypeStruct((nb, n_p * SUBLANES, half), F32)] * 2,
        compiler_params=_cparams(("parallel", "parallel")),
        name=name,
    )(page_table.reshape(-1), *([pages_t] * (n_pages_step + 1)), pe_seg, w_big)


def _cmp_select_body(q_ref, ck_ref, cv_ref, ov_ref, slope_ref, oc_ref, sel_ref, *, q_off, n_sel):
    tq = q_ref.shape[0]
    rows = HEADS_PER_KV * tq
    nc = ck_ref.shape[0]
    ns = ov_ref.shape[1]
    base = q_off + pl.program_id(1) * tq
    tok = lax.broadcasted_iota(jnp.int32, (rows, 1), 0) & (tq - 1)
    pos = base + tok
    cpos = lax.broadcasted_iota(jnp.int32, (1, nc), 1) * CMP_STRIDE + (CMP_LEN - 1)
    d_c = (pos - cpos).astype(F32)
    ok = d_c >= 0.0
    imps = []
    for g in range(N_KV_HEADS):
        qg = jnp.concatenate(
            [q_ref[:, (g * HEADS_PER_KV + h) * SLOT:(g * HEADS_PER_KV + h + 1) * SLOT]
             for h in range(HEADS_PER_KV)], axis=0)
        s = _dot_nt(qg, ck_ref[:, g * SLOT:(g + 1) * SLOT]) - slope_ref[g] * d_c
        s = jnp.where(ok, s, NEG)
        e = jnp.where(ok, jnp.exp(s - jnp.max(s, axis=-1, keepdims=True)), 0.0)
        l = jnp.sum(e, axis=-1, keepdims=True)
        p = e * jnp.where(l > 0.0, 1.0 / l, 0.0)
        o = _dot(p.astype(BF16), cv_ref[:, g * SLOT:(g + 1) * SLOT])
        psum = p[0:tq]
        for h in range(1, HEADS_PER_KV):
            psum = psum + p[h * tq:(h + 1) * tq]
        for h in range(HEADS_PER_KV):
            c0 = (g * HEADS_PER_KV + h) * SLOT
            oc_ref[:, c0:c0 + SLOT] = o[h * tq:(h + 1) * tq].astype(oc_ref.dtype)
        hi = psum.astype(BF16)
        rem = psum - hi.astype(F32)
        mid = rem.astype(BF16)
        lo = (rem - mid.astype(F32)).astype(BF16)
        ov = ov_ref[...]
        imps.append(_dot(hi, ov) + _dot(mid, ov) + _dot(lo, ov))
    imp = jnp.concatenate(imps, axis=0)
    blk = lax.broadcasted_iota(jnp.int32, (1, ns), 1)
    cur = pos >> SLC_SHIFT
    forced = (blk == 0) | (blk == cur) | (blk == cur - 1)
    imp = jnp.where(forced, FORCE, imp)
    imp = jnp.where(blk > cur, NEG, imp)
    blk_f = blk.astype(F32)
    sel = jnp.zeros(imp.shape, F32)
    for _ in range(n_sel):
        top = jnp.max(imp, axis=-1, keepdims=True)
        first = jnp.min(jnp.where(imp == top, blk_f, float(ns)), axis=-1, keepdims=True)
        hit = blk_f == first
        sel = jnp.where(hit, 1.0, sel)
        imp = jnp.where(hit, BELOW_NEG, imp)
    for g in range(N_KV_HEADS):
        sel_ref[:, g * ns:(g + 1) * ns] = sel[g * tq:(g + 1) * tq].astype(sel_ref.dtype)


def _cmp_select(q_slot, ck_slot, cv_slot, overlap, slope_rows, tq, q_off, n_sel, name):
    m = q_slot.shape[0]
    nb, nc, _ = ck_slot.shape
    ns = overlap.shape[1]
    nq = m // (nb * tq)
    assert (tq & (tq - 1)) == 0
    return pl.pallas_call(
        functools.partial(_cmp_select_body, q_off=q_off, n_sel=n_sel),
        grid=(nb, nq),
        in_specs=[pl.BlockSpec((tq, N_HEADS * SLOT), lambda b, i: (b * nq + i, 0)),
                  pl.BlockSpec((None, nc, N_KV_HEADS * SLOT), lambda b, i: (b, 0, 0)),
                  pl.BlockSpec((None, nc, N_KV_HEADS * SLOT), lambda b, i: (b, 0, 0)),
                  _const_spec(overlap.shape), _const_spec(slope_rows.shape)],
        out_specs=[pl.BlockSpec((tq, N_HEADS * SLOT), lambda b, i: (b * nq + i, 0)),
                   pl.BlockSpec((tq, N_KV_HEADS * ns), lambda b, i: (b * nq + i, 0))],
        out_shape=[jax.ShapeDtypeStruct((m, N_HEADS * SLOT), BF16),
                   jax.ShapeDtypeStruct((m, N_KV_HEADS * ns), BF16)],
        compiler_params=_cparams(("parallel", "parallel")),
        name=name,
    )(q_slot, ck_slot, cv_slot, overlap, slope_rows)


def _cmp_select_t_body(q_ref, ck_ref, cvt_ref, ovt_ref, sext_ref, pext_ref, oc_ref, sel_ref, *, q_off, n_sel):
    tq = q_ref.shape[0]
    rows = HEADS_PER_KV * tq
    nc = ck_ref.shape[0]
    ns = ovt_ref.shape[0]
    base = q_off + pl.program_id(1) * tq
    pos = base + (lax.broadcasted_iota(jnp.int32, (1, rows), 1) & (tq - 1))
    cpos = lax.broadcasted_iota(jnp.int32, (nc, 1), 0) * CMP_STRIDE + (CMP_LEN - 1)
    ok = cpos <= pos
    imps = []
    for g in range(N_KV_HEADS):
        cs = slice(g * SLOT, (g + 1) * SLOT)
        qg = jnp.concatenate(
            [q_ref[:, (g * HEADS_PER_KV + h) * SLOT:(g * HEADS_PER_KV + h + 1) * SLOT]
             for h in range(HEADS_PER_KV)], axis=0) + sext_ref[g]
        s = jnp.where(ok, _dot_nt(ck_ref[:, cs] + pext_ref[...], qg), NEG)
        e = jnp.where(ok, jnp.exp(s - jnp.max(s, axis=0, keepdims=True)), 0.0)
        l = jnp.sum(e, axis=0, keepdims=True)
        p = e * jnp.where(l > 0.0, 1.0 / l, 0.0)
        o = _dot(cvt_ref[cs, :], p.astype(BF16)).T
        for h in range(HEADS_PER_KV):
            c0 = (g * HEADS_PER_KV + h) * SLOT
            oc_ref[:, c0:c0 + SLOT] = o[h * tq:(h + 1) * tq].astype(oc_ref.dtype)
        psum = p[:, 0:tq]
        for h in range(1, HEADS_PER_KV):
            psum = psum + p[:, h * tq:(h + 1) * tq]
        hi = psum.astype(BF16)
        rem = psum - hi.astype(F32)
        mid = rem.astype(BF16)
        lo = (rem - mid.astype(F32)).astype(BF16)
        ovt = ovt_ref[...]
        imps.append(_dot(ovt, hi) + _dot(ovt, mid) + _dot(ovt, lo))
    imp = jnp.concatenate(imps, axis=1)
    blk = lax.broadcasted_iota(jnp.int32, (ns, 1), 0)
    cur = pos >> SLC_SHIFT
    forced = (blk == 0) | (blk == cur) | (blk == cur - 1)
    imp = jnp.where(forced, FORCE, imp)
    imp = jnp.where(blk > cur, NEG, imp)
    blk_f = blk.astype(F32)
    sel = jnp.zeros(imp.shape, F32)
    for _ in range(n_sel):
        top = jnp.max(imp, axis=0, keepdims=True)
        first = jnp.min(jnp.where(imp == top, blk_f, float(ns)), axis=0, keepdims=True)
        hit = blk_f == first
        sel = jnp.where(hit, 1.0, sel)
        imp = jnp.where(hit, BELOW_NEG, imp)
    for g in range(N_KV_HEADS):
        sel_ref[:, g * ns:(g + 1) * ns] = sel[:, g * tq:(g + 1) * tq].T.astype(sel_ref.dtype)


def _cmp_select_t(q_slot, ck_slot, cvt_slot, overlap_t, tq, q_off, n_sel, name):
    m = q_slot.shape[0]
    nb, nc, _ = ck_slot.shape
    ns = overlap_t.shape[0]
    nq = m // (nb * tq)
    assert tq % LANES == 0 and (tq & (tq - 1)) == 0
    sext, pext = _alibi_lanes(np.arange(nc) * CMP_STRIDE + (CMP_LEN - 1), tq)
    return pl.pallas_call(
        functools.partial(_cmp_select_t_body, q_off=q_off, n_sel=n_sel),
        grid=(nb, nq),
        in_specs=[pl.BlockSpec((tq, N_HEADS * SLOT), lambda b, i: (b * nq + i, 0)),
                  pl.BlockSpec((None, nc, N_KV_HEADS * SLOT), lambda b, i: (b, 0, 0)),
                  pl.BlockSpec((None, N_KV_HEADS * SLOT, nc), lambda b, i: (b, 0, 0)),
                  _const_spec(overlap_t.shape), _const_spec(sext.shape), _const_spec(pext.shape)],
        out_specs=[pl.BlockSpec((tq, N_HEADS * SLOT), lambda b, i: (b * nq + i, 0)),
                   pl.BlockSpec((tq, N_KV_HEADS * ns), lambda b, i: (b * nq + i, 0))],
        out_shape=[jax.ShapeDtypeStruct((m, N_HEADS * SLOT), BF16),
                   jax.ShapeDtypeStruct((m, N_KV_HEADS * ns), BF16)],
        compiler_params=_cparams(("parallel", "parallel")),
        name=name,
    )(q_slot, ck_slot, cvt_slot, overlap_t, sext, pext)


def _slc_body(q_ref, sel_ref, k_ref, vt_ref, sext_ref, pext_ref, hot_ref, o_ref,
              qa_scr, m_scr, acc_scr, *, tk):
    i = pl.program_id(1)
    tq = q_ref.shape[0]
    rows = HEADS_PER_KV * tq
    ns = hot_ref.shape[1]
    for g in range(N_KV_HEADS):
        qg = jnp.concatenate(
            [q_ref[:, (g * HEADS_PER_KV + h) * SLOT:(g * HEADS_PER_KV + h + 1) * SLOT]
             for h in range(HEADS_PER_KV)], axis=0) + sext_ref[g]
        bias = ((sel_ref[:, g * ns:(g + 1) * ns].astype(F32) - 1.0) * MASK_BIG).astype(BF16)
        qa_scr[g] = jnp.concatenate([qg, jnp.concatenate([bias] * HEADS_PER_KV, axis=0)], axis=1)
    m_scr[...] = jnp.full_like(m_scr, NEG)
    acc_scr[...] = jnp.zeros_like(acc_scr)

    def step(j, diagonal):
        k0 = pl.multiple_of(j * tk, tk)
        kext = pext_ref[pl.ds(k0, tk), :]
        hot = hot_ref[pl.ds(k0, tk), :]
        if diagonal:
            pos = i * tq + (lax.broadcasted_iota(jnp.int32, (1, rows), 1) & (tq - 1))
            causal = (k0 + lax.broadcasted_iota(jnp.int32, (tk, 1), 0)) <= pos
        ones_row = lax.broadcasted_iota(jnp.int32, (SLOT, 1), 0) == SUM_ROW

        def scores(g):
            cs = slice(g * SLOT, (g + 1) * SLOT)
            ka = jnp.concatenate([k_ref[pl.ds(k0, tk), cs] + kext, hot], axis=1)
            s = _dot_nt(ka, qa_scr[g])
            return jnp.where(causal, s, NEG) if diagonal else s

        s_next = scores(0)
        for g in range(N_KV_HEADS):
            s = s_next
            if g + 1 < N_KV_HEADS:
                s_next = scores(g + 1)
            cs = slice(g * SLOT, (g + 1) * SLOT)
            m_old = m_scr[g]
            m_new = jnp.maximum(m_old, jnp.max(s, axis=0, keepdims=True))
            p = jnp.exp(s - m_new).astype(BF16)
            vt = jnp.where(ones_row, 1.0, vt_ref[j, cs, :]).astype(BF16)
            acc_scr[g] = jnp.exp(m_old - m_new) * acc_scr[g] + _dot(vt, p)
            m_scr[g] = m_new

    n_full = (i * tq) // tk

    def full_step(j, carry):
        step(j, False)
        return carry

    lax.fori_loop(0, n_full, full_step, 0)
    step(n_full, True)
    for g in range(N_KV_HEADS):
        acc = acc_scr[g]
        o = (acc * (1.0 / acc[SUM_ROW:SUM_ROW + 1, :])).T
        for h in range(HEADS_PER_KV):
            c0 = (g * HEADS_PER_KV + h) * SLOT
            o_ref[:, c0:c0 + SLOT] = o[h * tq:(h + 1) * tq].astype(o_ref.dtype)


def _slc_attention(q_slot, sel, k_slot, vt_tiles, nb, seq_len, tq, tk, name):
    nq = seq_len // tq
    nk = seq_len // tk
    rows = HEADS_PER_KV * tq
    ns = sel.shape[1] // N_KV_HEADS
    assert seq_len % tk == 0 and tk % tq == 0 and seq_len <= ns * SLC_BLOCK
    assert vt_tiles.shape == (nb * nk, N_KV_HEADS * SLOT, tk)
    sext, pext, hot = _slc_constants(seq_len, tq, ns)
    once = pl.Buffered(1)
    return pl.pallas_call(
        functools.partial(_slc_body, tk=tk),
        grid=(nb, nq),
        in_specs=[pl.BlockSpec((tq, N_HEADS * SLOT), lambda b, i: (b * nq + i, 0)),
                  pl.BlockSpec((tq, N_KV_HEADS * ns), lambda b, i: (b * nq + i, 0)),
                  pl.BlockSpec((seq_len, N_KV_HEADS * SLOT), lambda b, i: (b, 0), pipeline_mode=once),
                  pl.BlockSpec((nk, N_KV_HEADS * SLOT, tk), lambda b, i: (b, 0, 0), pipeline_mode=once),
                  pl.BlockSpec(sext.shape, lambda b, i: (0, 0, 0), pipeline_mode=once),
                  pl.BlockSpec(pext.shape, lambda b, i: (0, 0), pipeline_mode=once),
                  pl.BlockSpec(hot.shape, lambda b, i: (0, 0), pipeline_mode=once)],
        out_specs=pl.BlockSpec((tq, N_HEADS * SLOT), lambda b, i: (b * nq + i, 0)),
        out_shape=jax.ShapeDtypeStruct(q_slot.shape, BF16),
        scratch_shapes=[pltpu.VMEM((N_KV_HEADS, rows, SLOT + ns), BF16),
                        pltpu.VMEM((N_KV_HEADS, 1, rows), F32),
                        pltpu.VMEM((N_KV_HEADS, SLOT, rows), F32)],
        compiler_params=_cparams(("parallel", "arbitrary")),
        name=name,
    )(q_slot, sel, k_slot, vt_tiles, sext, pext, hot)


def _win_body(q_ref, *refs, n_blk):
    k_refs = refs[:n_blk]
    vt_refs = refs[n_blk:2 * n_blk]
    sext_ref, pext_ref, band_ref, o_ref = refs[2 * n_blk:]
    tq = q_ref.shape[0]
    i = pl.program_id(1)
    n_keys = n_blk * tq
    key = lax.broadcasted_iota(jnp.int32, (n_keys, 1), 0)
    bias = band_ref[...] + jnp.where(key >= ((n_blk - 1) - i) * tq, 0.0, NEG)
    ones_row = lax.broadcasted_iota(jnp.int32, (SLOT, 1), 0) == SUM_ROW
    for g in range(N_KV_HEADS):
        cs = slice(g * SLOT, (g + 1) * SLOT)
        qg = jnp.concatenate(
            [q_ref[:, (g * HEADS_PER_KV + h) * SLOT:(g * HEADS_PER_KV + h + 1) * SLOT]
             for h in range(HEADS_PER_KV)], axis=0) + sext_ref[g]
        kg = jnp.concatenate([k_refs[k][:, cs] for k in range(n_blk)], axis=0) + pext_ref[...]
        s = _dot_nt(kg, qg) + bias
        p = jnp.exp(s - jnp.max(s, axis=0, keepdims=True)).astype(BF16)
        vt = jnp.concatenate([vt_refs[k][cs, :] for k in range(n_blk)], axis=1)
        vt = jnp.where(ones_row, 1.0, vt).astype(BF16)
        acc = _dot(vt, p)
        o = (acc * (1.0 / acc[SUM_ROW:SUM_ROW + 1, :])).T
        for h in range(HEADS_PER_KV):
            c0 = (g * HEADS_PER_KV + h) * SLOT
            o_ref[:, c0:c0 + SLOT] = o[h * tq:(h + 1) * tq].astype(o_ref.dtype)


def _win_attention(q_slot, k_slot, vt_tiles, nb, seq_len, tq, name):
    nq = seq_len // tq
    n_blk = WINDOW // tq + 1
    n_keys = n_blk * tq
    sext, pext, _ = _slc_constants(n_keys, tq, LANES)
    dist = WINDOW + (np.arange(HEADS_PER_KV * tq) % tq)[None, :] - np.arange(n_keys)[:, None]
    band = jnp.asarray(np.where((dist >= 0) & (dist < WINDOW), 0.0, NEG).astype(np.float32))

    def block_map(k, transposed):
        def index(b, i):
            blk = b * nq + jnp.maximum(i - (n_blk - 1) + k, 0)
            return (blk, 0, 0) if transposed else (blk, 0)
        return index

    k_specs = [pl.BlockSpec((tq, N_KV_HEADS * SLOT), block_map(k, False)) for k in range(n_blk)]
    vt_specs = [pl.BlockSpec((None, N_KV_HEADS * SLOT, tq), block_map(k, True)) for k in range(n_blk)]
    return pl.pallas_call(
        functools.partial(_win_body, n_blk=n_blk),
        grid=(nb, nq),
        in_specs=[pl.BlockSpec((tq, N_HEADS * SLOT), lambda b, i: (b * nq + i, 0))]
        + k_specs + vt_specs + [_const_spec(sext.shape), _const_spec(pext.shape), _const_spec(band.shape)],
        out_specs=pl.BlockSpec((tq, N_HEADS * SLOT), lambda b, i: (b * nq + i, 0)),
        out_shape=jax.ShapeDtypeStruct(q_slot.shape, BF16),
        compiler_params=_cparams(("parallel", "parallel")),
        name=name,
    )(q_slot, *([k_slot] * n_blk), *([vt_tiles] * n_blk), sext, pext, band)


def _paged_attn_body(pt_ref, *refs, n_pages, n_steps, past_base, q_off, tokens, window, use_sel):
    page_refs = refs[:n_pages]
    refs = refs[n_pages:]
    new_ref, q_ref, slope_ref = refs[:3]
    refs = refs[3:]
    if use_sel:
        sel_ref = refs[0]
        refs = refs[1:]
    o_ref, m_scr, l_scr, acc_scr = refs
    s_id = pl.program_id(1)
    rows = q_ref.shape[0]
    half = N_KV_HEADS * HEAD_DIM

    @pl.when(s_id == 0)
    def _():
        m_scr[...] = jnp.full_like(m_scr, NEG)
        l_scr[...] = jnp.zeros_like(l_scr)
        acc_scr[...] = jnp.zeros_like(acc_scr)

    pos = q_off + (lax.broadcasted_iota(jnp.int32, (rows, 1), 0) & (tokens - 1))
    q = q_ref[...]
    key = lax.broadcasted_iota(jnp.int32, (1, PAGE), 1)

    def update(pages, kpos0s, blk0s):
        ss, keeps = [], []
        for ref, kpos0, blk0 in zip(pages, kpos0s, blk0s):
            d = pos - (kpos0 + key)
            valid = d >= 0
            if window is not None:
                valid = valid & (d < window)
            if use_sel:
                ns = sel_ref.shape[1]
                blk_row = lax.broadcasted_iota(jnp.int32, (ns, PAGE), 0)
                key_blk = blk0 + (lax.broadcasted_iota(jnp.int32, (ns, PAGE), 1) >> SLC_SHIFT)
                expand = jnp.where(blk_row == key_blk, 1.0, 0.0).astype(BF16)
                valid = valid & (_dot(sel_ref[...], expand) > 0.5)
            k_t = ref[0].reshape(half, PAGE).astype(BF16)
            s = _dot(q, k_t) - slope_ref[...] * d.astype(F32)
            ss.append(jnp.where(valid, s, NEG))
            keeps.append(jnp.where(valid, 1.0, 0.0))
        s = jnp.concatenate(ss, axis=1)
        keep = jnp.concatenate(keeps, axis=1)
        m_old = m_scr[...]
        m_new = jnp.maximum(m_old, jnp.max(s, axis=-1, keepdims=True))
        p = jnp.exp(s - m_new) * keep
        alpha = jnp.exp(m_old - m_new)
        l_scr[...] = alpha * l_scr[...] + jnp.sum(p, axis=-1, keepdims=True)
        p = p.astype(BF16)
        acc = alpha * acc_scr[...]
        for k, ref in enumerate(pages):
            v_t = ref[1].reshape(half, PAGE).astype(BF16)
            acc = acc + _dot_nt(p[:, k * PAGE:(k + 1) * PAGE], v_t)
        acc_scr[...] = acc
        m_scr[...] = m_new

    first_page = s_id * n_pages
    update(page_refs,
           [past_base + (first_page + k) * PAGE for k in range(n_pages)],
           [(first_page + k) * (PAGE // SLC_BLOCK) for k in range(n_pages)])

    @pl.when(s_id == n_steps - 1)
    def _():
        update([new_ref], [q_off], [(q_off - past_base) // SLC_BLOCK])
        o_ref[...] = acc_scr[...] * (1.0 / l_scr[...])


def _paged_attention(pages_t, page_index, n_p, new_t, q_bd, slope_rows, sel_rows, n_pages_step,
                     past_base, q_off, tokens, window, table, name):
    nb, rows, half = q_bd.shape
    assert n_p % n_pages_step == 0 and (tokens & (tokens - 1)) == 0
    n_steps = n_p // n_pages_step
    use_sel = sel_rows is not None
    page_block = (None,) * (pages_t.ndim - 4) + (2, N_KV_HEADS, HEAD_DIM, PAGE)

    def page_map(k):
        def index(b, s, pt):
            return page_index(b, s * n_pages_step + k, pt)
        return index

    in_specs = [pl.BlockSpec(page_block, page_map(k)) for k in range(n_pages_step)]
    in_specs += [pl.BlockSpec((None, 2, N_KV_HEADS, HEAD_DIM, PAGE), lambda b, s, pt: (b, 0, 0, 0, 0)),
                 pl.BlockSpec((None, rows, half), lambda b, s, pt: (b, 0, 0)),
                 pl.BlockSpec(slope_rows.shape, lambda b, s, pt: (0, 0))]
    args = [pages_t] * n_pages_step + [new_t, q_bd, slope_rows]
    if use_sel:
        in_specs.append(pl.BlockSpec((None,) + sel_rows.shape[1:], lambda b, s, pt: (b, 0, 0)))
        args.append(sel_rows)
    grid_spec = pltpu.PrefetchScalarGridSpec(
        num_scalar_prefetch=1, grid=(nb, n_steps), in_specs=in_specs,
        out_specs=pl.BlockSpec((None, rows, half), lambda b, s, pt: (b, 0, 0)),
        scratch_shapes=[pltpu.VMEM((rows, 1), F32), pltpu.VMEM((rows, 1), F32),
                        pltpu.VMEM((rows, half), F32)])
    body = functools.partial(_paged_attn_body, n_pages=n_pages_step, n_steps=n_steps,
                             past_base=past_base, q_off=q_off, tokens=tokens, window=window,
                             use_sel=use_sel)
    return pl.pallas_call(
        body,
        grid_spec=grid_spec,
        out_shape=jax.ShapeDtypeStruct((nb, rows, half), F32),
        compiler_params=_cparams(("parallel", "arbitrary")),
        name=name,
    )(table.reshape(-1), *args)


def _combine_body(x_ref, gates_ref, oc_ref, os_ref, ow_ref, wo_ref, o_ref, mix_scr):
    gt = jax.nn.sigmoid(gates_ref[...])
    for hh in range(N_HEADS):
        cs = slice(hh * SLOT, (hh + 1) * SLOT)
        mix = gt[:, hh:hh + 1] * oc_ref[:, cs].astype(F32)
        mix = mix + gt[:, N_HEADS + hh:N_HEADS + hh + 1] * os_ref[:, cs].astype(F32)
        mix = mix + gt[:, 2 * N_HEADS + hh:2 * N_HEADS + hh + 1] * ow_ref[:, cs].astype(F32)
        mix_scr[:, cs] = mix.astype(BF16)
    o_ref[...] = x_ref[...] + _dot(mix_scr[...], wo_ref[...])


def _combine(x, gates, o_c, o_s, o_w, w_out_slot, tm, name):
    m, d = x.shape
    wide = N_HEADS * SLOT
    row_spec = lambda w: pl.BlockSpec((tm, w), lambda i: (i, 0))
    return pl.pallas_call(
        _combine_body,
        grid=(m // tm,),
        in_specs=[row_spec(d), row_spec(gates.shape[1]), row_spec(wide), row_spec(wide), row_spec(wide),
                  _const_spec(w_out_slot.shape)],
        out_specs=row_spec(d),
        out_shape=jax.ShapeDtypeStruct((m, d), F32),
        scratch_shapes=[pltpu.VMEM((tm, wide), BF16)],
        compiler_params=_cparams(("parallel",)),
        name=name,
    )(x, gates, o_c, o_s, o_w, w_out_slot)


def _slot_cols(w):
    d, n = w.shape
    w = w.reshape(d, n // HEAD_DIM, HEAD_DIM)
    return jnp.pad(w, ((0, 0), (0, 0), (0, SLOT - HEAD_DIM))).reshape(d, (n // HEAD_DIM) * SLOT)


def _slot_rows(w):
    n, d = w.shape
    w = w.reshape(n // HEAD_DIM, HEAD_DIM, d)
    return jnp.pad(w, ((0, 0), (0, SLOT - HEAD_DIM), (0, 0))).reshape((n // HEAD_DIM) * SLOT, d)


def _slot_last(a):
    lead = a.shape[:-1]
    n = a.shape[-1] // HEAD_DIM
    a = a.reshape(lead + (n, HEAD_DIM))
    a = jnp.pad(a, [(0, 0)] * len(lead) + [(0, 0), (0, SLOT - HEAD_DIM)])
    return a.reshape(lead + (n * SLOT,))


def _alibi_slopes():
    h = np.arange(1, N_HEADS + 1, dtype=np.float32)
    return np.asarray(2.0 ** (-8.0 * h / N_HEADS), dtype=np.float32).reshape(N_KV_HEADS, HEADS_PER_KV)


def _slope_rows_by_group(tq):
    return jnp.asarray(np.repeat(_alibi_slopes(), tq, axis=1)[:, :, None])


def _bf16_parts(x):
    x = np.asarray(x, np.float32)
    parts = []
    for _ in range(3):
        part = x.astype(jnp.bfloat16)
        parts.append(part)
        x = x - part.astype(np.float32)
    assert not np.any(x)
    return parts


def _alibi_lanes(kpos, tq):
    kpos = np.asarray(kpos)
    assert kpos.min() >= 0 and kpos.max() < 256 * LANES
    parts = _bf16_parts(np.repeat(_alibi_slopes(), tq, axis=1))
    sext = np.zeros((N_KV_HEADS, HEADS_PER_KV * tq, SLOT), jnp.bfloat16)
    pext = np.zeros((len(kpos), SLOT), np.float32)
    for n, part in enumerate(parts):
        sext[:, :, ALIBI_LANE + 2 * n] = part
        sext[:, :, ALIBI_LANE + 2 * n + 1] = part
        pext[:, ALIBI_LANE + 2 * n] = (kpos // LANES) * LANES
        pext[:, ALIBI_LANE + 2 * n + 1] = kpos % LANES
    return jnp.asarray(sext), jnp.asarray(pext, dtype=BF16)


def _slc_constants(seq_len, tq, ns):
    kpos = np.arange(seq_len)
    sext, pext = _alibi_lanes(kpos, tq)
    hot = (kpos[:, None] // SLC_BLOCK) == np.arange(ns)[None, :]
    return sext, pext, jnp.asarray(hot.astype(np.float32), dtype=BF16)


def _overlap(n_cmp, n_slc, nc, ns):
    st = np.arange(nc) * CMP_STRIDE
    bs = np.arange(ns) * SLC_BLOCK
    m = (st[:, None] < bs[None, :] + SLC_BLOCK) & (st[:, None] + CMP_LEN > bs[None, :])
    m = m & (np.arange(nc)[:, None] < n_cmp) & (np.arange(ns)[None, :] < n_slc)
    return jnp.asarray(m.astype(np.float32), dtype=BF16)


def _compress_weights(cmp_pe, cmp_w):
    eye = jnp.eye(N_KV_HEADS, dtype=F32)
    w = cmp_w.reshape(2, 2, CMP_STRIDE, HEAD_DIM, HEAD_DIM)
    bd = jnp.einsum('kmrde,gh->rkgdmhe', w, eye)
    half = N_KV_HEADS * HEAD_DIM
    w_big = bd.reshape(CMP_STRIDE, 2, half, 2 * half).astype(BF16)
    pe = cmp_pe.reshape(2, 2, CMP_STRIDE, HEAD_DIM)
    pe = jnp.broadcast_to(pe.transpose(1, 2, 0, 3)[:, :, :, None, :],
                          (2, CMP_STRIDE, 2, N_KV_HEADS, HEAD_DIM)).reshape(2, SEG_W)
    pe_seg = jnp.pad(pe, ((0, SUBLANES - 2), (0, 0)))
    return pe_seg, w_big


def _nsa_weights(w_in, segs_slot_kv):
    q_dim = N_HEADS * HEAD_DIM
    half = N_KV_HEADS * HEAD_DIM
    w_q = w_in[:, :q_dim] * (HEAD_DIM ** -0.5)
    w_gate = w_in[:, q_dim + N_BRANCH * KV_ROW:]
    parts = [_slot_cols(w_q), w_in[:, q_dim:q_dim + N_BRANCH * KV_ROW],
             jnp.pad(w_gate, ((0, 0), (0, LANES - w_gate.shape[1])))]
    segs = [(N_HEADS * SLOT, BF16, False), (KV_ROW, F32, False), (KV_ROW, F32, False),
            (KV_ROW, F32, False), (LANES, F32, False)]
    w_t = None
    if segs_slot_kv:
        o_slc = q_dim + KV_ROW
        o_win = q_dim + 2 * KV_ROW
        parts += [_slot_cols(w_in[:, o_slc:o_slc + half]), _slot_cols(w_in[:, o_win:o_win + half])]
        segs += [(N_KV_HEADS * SLOT, BF16, False)] * 2
        w_t = jnp.concatenate([_slot_cols(w_in[:, o_slc + half:o_slc + KV_ROW]),
                               _slot_cols(w_in[:, o_win + half:o_win + KV_ROW])], axis=1).T.astype(BF16)
    return jnp.concatenate(parts, axis=1).astype(BF16), segs, w_t


def _sgu_layer(x, p, a, seq_len, tm, emit_v, tag):
    d = x.shape[1]
    d_sgu = p['sgu_w_out'].shape[1]
    gw = d_sgu // N_SGU_GROUPS
    segs = [(d_sgu, BF16, True), (d_sgu, BF16, True)]
    u, v = _proj(x, p['norm_mix'][0], p['sgu_w_in'][a].astype(BF16), p['sgu_b_in'][a].reshape(1, -1),
                 segs, tm, f"sgu_in_{tag}")
    w_s, b_s = p['sgu_w_s'][a], p['sgu_b_s'][a]
    if seq_len % CHUNK == 0:
        bs_rows = b_s.T
    else:
        assert CHUNK % seq_len == 0
        rep = CHUNK // seq_len
        w_s = jnp.einsum('ab,gts->gatbs', jnp.eye(rep, dtype=F32), w_s[:, :seq_len, :seq_len])
        w_s = w_s.reshape(N_SGU_GROUPS, CHUNK, CHUNK)
        bs_rows = jnp.tile(b_s[:, :seq_len].T, (rep, 1))
    bs_exp = jnp.repeat(bs_rows, gw, axis=1)
    return _sgu(x, u, v, p['sgu_v_norm'][a], w_s, bs_exp, p['sgu_w_out'][a].astype(BF16), tm, emit_v,
                f"sgu_{tag}")


def _ffn_layer(x, p, layer, seq_len, past, final, tm, tag):
    gamma_final = p['norm_final'] if final else None
    return _ffn(x, p['norm_ffn'][layer], p['ffn_w_in'][layer].astype(BF16), p['ffn_conv_w'][layer],
                p['ffn_conv_b'][layer], p['ffn_w_out'][layer].astype(BF16), seq_len, past, gamma_final,
                tm, 256, f"ffn{layer}_{tag}")


def _nsa_prompt(x, p, b_idx, nb, seq_len, tm):
    w_ext, segs, w_t = _nsa_weights(p['nsa_w_in'][b_idx], True)
    tq = 128
    wide = N_KV_HEADS * SLOT
    q_slot, kv_cmp, kv_slc, kv_win, gates, ks_slot, kw_slot, vs_tiles, vw_tiles = _proj(
        x, p['norm_mix'][1], w_ext, None, segs, tm, "nsa_in_prompt", w_t,
        t_outs=((0, wide, tm), (wide, wide, tq)))
    n_pages = seq_len // PAGE
    pe_seg, w_big = _compress_weights(p['nsa_cmp_pe'][b_idx], p['nsa_cmp_w'][b_idx])
    ck, cv = _compress_rows(kv_cmp, nb, pe_seg, w_big, min(n_pages, 16), "compress_prompt")
    n_seg = seq_len // CMP_STRIDE
    n_cmp = n_seg - CMP_LEN // CMP_STRIDE + 1
    n_slc = -(-seq_len // SLC_BLOCK)
    ns = -(-n_slc // LANES) * LANES
    o_c, sel = _cmp_select_t(q_slot, _slot_last(ck).astype(BF16),
                             _slot_last(cv).astype(BF16).transpose(0, 2, 1),
                             _overlap(n_cmp, n_slc, n_seg, ns).T, tq, 0,
                             min(SLC_TOP_N, n_slc), "cmp_select_prompt")
    o_s = _slc_attention(q_slot, sel, ks_slot, vs_tiles, nb, seq_len, tq, tm, "slc_prompt")
    o_w = _win_attention(q_slot, kw_slot, vw_tiles, nb, seq_len, tq, "win_prompt")
    x = _combine(x, gates, o_c, o_s, o_w, _slot_rows(p['nsa_w_out'][b_idx]).astype(BF16), tm,
                 "nsa_out_prompt")
    kv_shape = (nb, seq_len, 2, N_KV_HEADS, HEAD_DIM)
    kv_win = kv_win.reshape(kv_shape)
    return x, kv_cmp.reshape(kv_shape), kv_slc.reshape(kv_shape), kv_win[:, -min(WINDOW, seq_len):]


def _rows_ght(a, tokens_pad):
    nb, t, _, w = a.shape
    a = a.reshape(nb, t, N_KV_HEADS, HEADS_PER_KV, w).transpose(0, 2, 3, 1, 4)
    a = jnp.pad(a, ((0, 0), (0, 0), (0, 0), (0, tokens_pad - t), (0, 0)))
    return a.reshape(nb, N_HEADS * tokens_pad, w)


def _nsa_sample(x, p, b_idx, nb, t_new, past_len, cache_cmp, cache_slc, state_win, page_table):
    m = x.shape[0]
    half = N_KV_HEADS * HEAD_DIM
    t_pad = BF16_SUBLANES
    assert t_new <= t_pad and past_len % PAGE == 0
    w_ext, segs, _ = _nsa_weights(p['nsa_w_in'][b_idx], False)
    q_slot, kv_cmp, kv_slc, kv_win, gates = _proj(x, p['norm_mix'][1], w_ext, None, segs, m, "nsa_in_sample")
    n_pool = cache_cmp.shape[0]
    n_pages = past_len // PAGE
    total = past_len + t_new
    n_seg = total // CMP_STRIDE
    assert n_seg == past_len // CMP_STRIDE, "new rows must not complete a compression segment"
    n_cmp = n_seg - CMP_LEN // CMP_STRIDE + 1
    pe_seg, w_big = _compress_weights(p['nsa_cmp_pe'][b_idx], p['nsa_cmp_w'][b_idx])
    rows_last = (0, 2, 3, 4, 1)
    ck, cv = _compress_paged(cache_cmp.transpose(rows_last), page_table, pe_seg, w_big,
                             min(n_pages, 32), "compress_sample")
    n_slc = -(-total // SLC_BLOCK)
    ns = -(-n_slc // LANES) * LANES
    q4 = q_slot.reshape(nb, t_new, N_HEADS * SLOT)
    q_pad = jnp.pad(q4, ((0, 0), (0, t_pad - t_new), (0, 0))).reshape(nb * t_pad, N_HEADS * SLOT)
    o_c, sel = _cmp_select(q_pad, _slot_last(ck).astype(BF16), _slot_last(cv).astype(BF16),
                           _overlap(n_cmp, n_slc, n_seg, ns), _slope_rows_by_group(t_pad), t_pad,
                           past_len, min(SLC_TOP_N, n_slc), "cmp_select_sample")
    o_c = o_c.reshape(nb, t_pad, N_HEADS * SLOT)[:, :t_new].reshape(m, N_HEADS * SLOT)
    assert (t_new & (t_new - 1)) == 0
    qh = q4.reshape(nb, t_new, N_HEADS, SLOT)[..., :HEAD_DIM]
    q_rows = _rows_ght(qh, t_new).reshape(nb, N_KV_HEADS, HEADS_PER_KV * t_new, HEAD_DIM)
    q_bd = jnp.einsum('bgrd,gk->bgrkd', q_rows, jnp.eye(N_KV_HEADS, dtype=BF16))
    q_bd = q_bd.reshape(nb, N_HEADS * t_new, half)
    slope_rows = jnp.asarray(np.repeat(_alibi_slopes().reshape(-1), t_new)[:, None])
    sel_rows = sel.reshape(nb, t_pad, N_KV_HEADS, 1, ns)[:, :t_new].transpose(0, 2, 3, 1, 4)
    sel_rows = jnp.broadcast_to(sel_rows, (nb, N_KV_HEADS, HEADS_PER_KV, t_new, ns))
    sel_rows = sel_rows.reshape(nb, N_HEADS * t_new, ns)

    def new_page(kv):
        kv = kv.reshape(nb, t_new, 2, N_KV_HEADS, HEAD_DIM).transpose(rows_last)
        return jnp.pad(kv, ((0, 0),) * 4 + ((0, PAGE - t_new),))

    o_s = _paged_attention(cache_slc.transpose(rows_last), lambda b, pg, pt: (pt[b * n_pages + pg], 0, 0, 0, 0),
                           n_pages, new_page(kv_slc), q_bd, slope_rows, sel_rows, min(n_pages, 16),
                           0, past_len, t_new, None, page_table, "slc_sample")
    n_win = state_win.shape[1]
    assert n_win % PAGE == 0
    win_pages = n_win // PAGE
    o_w = _paged_attention(state_win.transpose(rows_last), lambda b, pg, pt: (b, 0, 0, 0, pg),
                           win_pages, new_page(kv_win), q_bd, slope_rows, None, win_pages,
                           past_len - n_win, past_len, t_new, WINDOW, page_table, "win_sample")

    def to_slot(o):
        o = o.reshape(nb, N_KV_HEADS, HEADS_PER_KV, t_new, N_KV_HEADS, HEAD_DIM)
        o = jnp.stack([o[:, g, :, :, g, :] for g in range(N_KV_HEADS)], axis=1)
        o = o.transpose(0, 3, 1, 2, 4).reshape(m, N_HEADS * HEAD_DIM)
        return _slot_last(o).astype(BF16)

    x = _combine(x, gates, o_c, to_slot(o_s), to_slot(o_w), _slot_rows(p['nsa_w_out'][b_idx]).astype(BF16),
                 m, "nsa_out_sample")
    kv_shape = (nb, t_new, 2, N_KV_HEADS, HEAD_DIM)
    kv_win5 = kv_win.reshape(kv_shape)
    all_win = jnp.concatenate([state_win.reshape((nb, n_win) + kv_shape[2:]), kv_win5], axis=1)
    return x, kv_cmp.reshape(kv_shape), kv_slc.reshape(kv_shape), all_win[:, -min(WINDOW, n_win + t_new):]


def _run_prompt(x_prompt, p):
    nb, seq_len, d = x_prompt.shape
    x = x_prompt.reshape(nb * seq_len, d)
    tm = min(512, seq_len)
    tm_ffn = min(1024, seq_len)
    x = _sgu_layer(x, p, 0, seq_len, tm, False, "prompt")[0]
    x, h0 = _ffn_layer(x, p, 0, seq_len, None, False, tm_ffn, "prompt")
    x, kc, ks, kw = _nsa_prompt(x, p, 0, nb, seq_len, tm)
    x, h1 = _ffn_layer(x, p, 1, seq_len, None, True, tm_ffn, "prompt")
    tiles = seq_len // tm_ffn

    def conv_state(h):
        return h.reshape(nb, tiles, SUBLANES, -1)[:, -1, SUBLANES - (CONV_W - 1):]

    return (x.reshape(nb, seq_len, d), kc[None], ks[None], kw[None],
            jnp.stack([conv_state(h0), conv_state(h1)]))


def _run_sample(x_sample, cache_cmp, cache_slc, state_win, state_conv, page_table, p):
    nb, t_new, d = x_sample.shape
    m = nb * t_new
    past_len = page_table.shape[1] * PAGE
    x = x_sample.reshape(m, d)
    d_ff = state_conv.shape[-1]

    def conv_past(layer):
        st = state_conv[layer]
        zeros = jnp.zeros((nb, t_new, d_ff), F32)
        prev1 = zeros.at[:, 0].set(st[:, 1])
        prev2 = zeros.at[:, 0].set(st[:, 0]).at[:, 1].set(st[:, 1])
        return prev1.reshape(m, d_ff), prev2.reshape(m, d_ff)

    def conv_state(h):
        return h.reshape(nb, t_new, d_ff)[:, -(CONV_W - 1):]

    x, vn = _sgu_layer(x, p, 0, t_new, m, True, "sample")
    x, h0 = _ffn_layer(x, p, 0, t_new, conv_past(0), False, m, "sample")
    x, kc, ks, kw = _nsa_sample(x, p, 0, nb, t_new, past_len, cache_cmp[0], cache_slc[0], state_win[0],
                                page_table)
    x, h1 = _ffn_layer(x, p, 1, t_new, conv_past(1), True, m, "sample")
    return (x.reshape(nb, t_new, d), kc[None], ks[None], kw[None],
            jnp.stack([conv_state(h0), conv_state(h1)]), vn.reshape(nb, t_new, -1)[None])


def kernel(x_prompt, x_sample, cache_cmp_kv, cache_slc_kv, state_win_kv, state_ffn_conv, page_table,
           norm_mix, norm_ffn, norm_final, sgu_w_in, sgu_b_in, sgu_v_norm, sgu_w_s, sgu_b_s, sgu_w_out,
           nsa_w_in, nsa_cmp_pe, nsa_cmp_w, nsa_w_out, ffn_w_in, ffn_conv_w, ffn_conv_b, ffn_w_out):
    p = {'norm_mix': norm_mix, 'norm_ffn': norm_ffn, 'norm_final': norm_final,
         'sgu_w_in': sgu_w_in, 'sgu_b_in': sgu_b_in, 'sgu_v_norm': sgu_v_norm, 'sgu_w_s': sgu_w_s,
         'sgu_b_s': sgu_b_s, 'sgu_w_out': sgu_w_out, 'nsa_w_in': nsa_w_in, 'nsa_cmp_pe': nsa_cmp_pe,
         'nsa_cmp_w': nsa_cmp_w, 'nsa_w_out': nsa_w_out, 'ffn_w_in': ffn_w_in, 'ffn_conv_w': ffn_conv_w,
         'ffn_conv_b': ffn_conv_b, 'ffn_w_out': ffn_w_out}
    y_p, p_cmp, p_slc, p_win, p_conv = _run_prompt(x_prompt, p)
    y_s, s_cmp, s_slc, s_win, s_conv, s_v = _run_sample(
        x_sample, cache_cmp_kv, cache_slc_kv, state_win_kv, state_ffn_conv, page_table, p)
    return (y_p, y_s, p_cmp, p_slc, p_win, p_conv, s_cmp, s_slc, s_win, s_conv, s_v)
```

```python
import functools

import numpy as np
import jax
import jax.numpy as jnp
from jax import lax
from jax.experimental import pallas as pl
from jax.experimental.pallas import tpu as pltpu

F32 = jnp.float32
BF16 = jnp.bfloat16

EPS = 1e-6
NEG = -1e30
FORCE = 1e9
BELOW_NEG = -3e38
MASK_BIG = 2.0 ** 30
LOG2E = 1.4426950408889634

CHUNK = 128
N_SGU_GROUPS = 8
HEAD_DIM = 64
N_KV_HEADS = 4
HEADS_PER_KV = 4
N_HEADS = N_KV_HEADS * HEADS_PER_KV
N_BRANCH = 3
CMP_LEN = 32
CMP_STRIDE = 16
SLC_BLOCK = 64
SLC_SHIFT = SLC_BLOCK.bit_length() - 1
SLC_TOP_N = 16
WINDOW = 512
CONV_W = 3
PAGE = 128
KV_ROW = 2 * N_KV_HEADS * HEAD_DIM
SEG_W = CMP_STRIDE * KV_ROW

LANES = 128
SUBLANES = 8
BF16_SUBLANES = 16
SLOT = LANES
ALIBI_LANE = HEAD_DIM
SUM_ROW = HEAD_DIM
V7X_VMEM_BYTES = 64 * 1024 * 1024
VMEM_LIMIT = (V7X_VMEM_BYTES * 3) // 4


def _cparams(sem):
    return pltpu.CompilerParams(dimension_semantics=sem, vmem_limit_bytes=VMEM_LIMIT)


def _const_spec(shape):
    nd = len(shape)
    return pl.BlockSpec(shape, lambda *_: (0,) * nd)


def _rms(x, g):
    return x * lax.rsqrt(jnp.mean(x * x, axis=-1, keepdims=True) + EPS) * g


def _dot(a, b):
    return jnp.dot(a, b, preferred_element_type=F32)


def _dot_nt(a, b):
    return lax.dot_general(a, b, (((1,), (1,)), ((), ())), preferred_element_type=F32)


def _proj_body(x_ref, g_ref, w_ref, b_ref, *o_refs, segs, chunk, has_bias, t_outs):
    xn = _rms(x_ref[...], g_ref[...]).astype(BF16)
    if t_outs:
        wt_ref = o_refs[0]
        ot_refs = o_refs[len(o_refs) - len(t_outs):]
        o_refs = o_refs[1:len(o_refs) - len(t_outs)]
        tm = xn.shape[0]
        for (r0, n_rows, tile, seq_len), ot_ref in zip(t_outs, ot_refs):
            rt = _dot_nt(wt_ref[r0:r0 + n_rows, :], xn)
            if seq_len is not None:
                ot_ref[...] = rt
            else:
                for k in range(tm // tile):
                    ot_ref[k] = rt[:, k * tile:(k + 1) * tile].astype(ot_ref.dtype)
    off = 0
    for (width, _, act), o_ref in zip(segs, o_refs):
        for c0 in range(0, width, chunk):
            cw = min(chunk, width - c0)
            r = _dot(xn, w_ref[:, off + c0:off + c0 + cw])
            if has_bias:
                r = r + b_ref[:, off + c0:off + c0 + cw]
            if act:
                r = jax.nn.gelu(r)
            o_ref[:, c0:c0 + cw] = r.astype(o_ref.dtype)
        off += width


def _proj(x, gamma, w, bias, segs, tm, name, w_t=None, t_outs=()):
    m, d = x.shape
    n = w.shape[1]
    assert sum(s[0] for s in segs) == n and m % tm == 0
    has_bias = bias is not None
    if bias is None:
        bias = jnp.zeros((1, LANES), F32)
    in_specs = [pl.BlockSpec((tm, d), lambda i: (i, 0)),
                _const_spec((1, d)), _const_spec(w.shape), _const_spec(bias.shape)]
    out_specs = [pl.BlockSpec((tm, s[0]), lambda i: (i, 0)) for s in segs]
    out_shape = [jax.ShapeDtypeStruct((m, s[0]), s[1]) for s in segs]
    args = [x, gamma.reshape(1, d), w, bias]
    if t_outs:
        in_specs.append(_const_spec(w_t.shape))
        args.append(w_t)
        for _, n_rows, tile, seq_len in t_outs:
            assert tm % tile == 0
            if seq_len is not None:
                assert tile == tm and seq_len % tm == 0
                tps = seq_len // tm
                out_specs.append(pl.BlockSpec((None, n_rows, tm), lambda i, tps=tps: (i // tps, 0, i % tps)))
                out_shape.append(jax.ShapeDtypeStruct((m // seq_len, n_rows, seq_len), F32))
            else:
                out_specs.append(pl.BlockSpec((tm // tile, n_rows, tile), lambda i: (i, 0, 0)))
                out_shape.append(jax.ShapeDtypeStruct((m // tile, n_rows, tile), BF16))
    body = functools.partial(_proj_body, segs=tuple(segs), chunk=512, has_bias=has_bias,
                             t_outs=tuple(t_outs))
    return pl.pallas_call(
        body,
        grid=(m // tm,),
        in_specs=in_specs,
        out_specs=out_specs,
        out_shape=out_shape,
        compiler_params=_cparams(("parallel",)),
        name=name,
    )(*args)


def _sgu_body(x_ref, u_ref, v_ref, vg_ref, ws_ref, bs_ref, wo_ref, *rest, emit_v):
    if emit_v:
        o_ref, vn_ref, vn_scr, a_scr = rest
    else:
        o_ref, vn_scr, a_scr = rest
    tm = x_ref.shape[0]
    gw = v_ref.shape[1] // N_SGU_GROUPS
    vn = _rms(v_ref[...].astype(F32), vg_ref[...])
    if emit_v:
        vn_ref[...] = vn
    vn_scr[...] = vn.astype(BF16)
    row = lax.broadcasted_iota(jnp.int32, (CHUNK, CHUNK), 0)
    col = lax.broadcasted_iota(jnp.int32, (CHUNK, CHUNK), 1)
    for g in range(N_SGU_GROUPS):
        wc = jnp.where(col <= row, ws_ref[g], 0.0).astype(BF16)
        for c in range(tm // CHUNK):
            rs = slice(c * CHUNK, (c + 1) * CHUNK)
            cs = slice(g * gw, (g + 1) * gw)
            s = _dot(wc, vn_scr[rs, cs]) + bs_ref[:, cs]
            a_scr[rs, cs] = (u_ref[rs, cs].astype(F32) * s).astype(BF16)
    o_ref[...] = x_ref[...] + _dot(a_scr[...], wo_ref[...])


def _sgu(x, u, v, v_norm, w_s, bs_exp, w_out, tm, emit_v, name):
    m, d = x.shape
    ds = u.shape[1]
    out_shape = [jax.ShapeDtypeStruct((m, d), F32)]
    out_specs = [pl.BlockSpec((tm, d), lambda i: (i, 0))]
    if emit_v:
        out_shape.append(jax.ShapeDtypeStruct((m, ds), F32))
        out_specs.append(pl.BlockSpec((tm, ds), lambda i: (i, 0)))
    return pl.pallas_call(
        functools.partial(_sgu_body, emit_v=emit_v),
        grid=(m // tm,),
        in_specs=[pl.BlockSpec((tm, d), lambda i: (i, 0)),
                  pl.BlockSpec((tm, ds), lambda i: (i, 0)),
                  pl.BlockSpec((tm, ds), lambda i: (i, 0)),
                  _const_spec((1, ds)), _const_spec(w_s.shape), _const_spec(bs_exp.shape),
                  _const_spec(w_out.shape)],
        out_specs=out_specs,
        out_shape=out_shape,
        scratch_shapes=[pltpu.VMEM((tm, ds), BF16), pltpu.VMEM((tm, ds), BF16)],
        compiler_params=_cparams(("parallel",)),
        name=name,
    )(x, u, v, v_norm.reshape(1, ds), w_s, bs_exp, w_out)


def _ffn_body(*refs, sample, final_norm, period, tiles_per_seq, n_f):
    refs = list(refs)
    x_ref, g_ref, wh_ref, wg_ref, cw_ref, cb_ref, wo_ref = refs[:7]
    refs = refs[7:]
    if sample:
        p1_ref, p2_ref = refs[:2]
        refs = refs[2:]
    if final_norm:
        gf_ref = refs[0]
        refs = refs[1:]
    o_ref, h_ref, xn_scr, acc_scr = refs[:4]
    i = pl.program_id(0)
    f = pl.program_id(1)
    tm = x_ref.shape[0]

    @pl.when(f == 0)
    def _():
        xn_scr[...] = _rms(x_ref[...], g_ref[...]).astype(BF16)
        acc_scr[...] = jnp.zeros_like(acc_scr)

    xn = xn_scr[...]
    h = _dot(xn, wh_ref[...])
    gate = _dot(xn, wg_ref[...])
    r = lax.broadcasted_iota(jnp.int32, (tm, 1), 0)
    if sample:
        r = r & (period - 1)
        prev1 = p1_ref[...]
        prev2 = p2_ref[...]
        h_ref[...] = h
    else:
        carry_scr = refs[4]

        @pl.when(i % tiles_per_seq == 0)
        def _():
            carry_scr[f] = jnp.zeros(carry_scr.shape[1:], F32)

        prev = carry_scr[f]
        c0 = prev[SUBLANES - 2:SUBLANES - 1, :]
        c1 = prev[SUBLANES - 1:SUBLANES, :]
        prev1 = c1
        prev2 = jnp.where(r == 1, c1, c0)
        tail = h[tm - SUBLANES:tm, :]
        carry_scr[f] = tail
        h_ref[...] = tail
    hm1 = jnp.where(r >= 1, pltpu.roll(h, 1, 0), prev1)
    hm2 = jnp.where(r >= 2, pltpu.roll(h, 2, 0), prev2)
    hc = cb_ref[...] + cw_ref[0:1, :] * hm2
    hc = hc + cw_ref[1:2, :] * hm1
    hc = hc + cw_ref[2:3, :] * h
    act = (jax.nn.gelu(hc) * gate).astype(BF16)
    acc_scr[...] += _dot(act, wo_ref[...])

    @pl.when(f == n_f - 1)
    def _():
        y = x_ref[...] + acc_scr[...]
        if final_norm:
            y = _rms(y, gf_ref[...])
        o_ref[...] = y


def _ffn(x, gamma, w_in, conv_w, conv_b, w_out, seq_len, past, gamma_final, tm, tf, name):
    m, d = x.shape
    d_ff = w_out.shape[0]
    assert m % tm == 0 and d_ff % tf == 0
    n_f = d_ff // tf
    sample = past is not None
    final_norm = gamma_final is not None
    if sample:
        assert tm == m and (seq_len & (seq_len - 1)) == 0
    else:
        assert seq_len % tm == 0
    in_specs = [pl.BlockSpec((tm, d), lambda i, f: (i, 0)),
                _const_spec((1, d)),
                pl.BlockSpec((d, tf), lambda i, f: (0, f)),
                pl.BlockSpec((d, tf), lambda i, f: (0, f + n_f)),
                pl.BlockSpec((CONV_W, tf), lambda i, f: (0, f)),
                pl.BlockSpec((1, tf), lambda i, f: (0, f)),
                pl.BlockSpec((tf, d), lambda i, f: (f, 0))]
    args = [x, gamma.reshape(1, d), w_in, w_in, conv_w, conv_b.reshape(1, d_ff), w_out]
    if sample:
        in_specs += [pl.BlockSpec((tm, tf), lambda i, f: (i, f))] * 2
        args += list(past)
    if final_norm:
        in_specs.append(_const_spec((1, d)))
        args.append(gamma_final.reshape(1, d))
    scratch = [pltpu.VMEM((tm, d), BF16), pltpu.VMEM((tm, d), F32)]
    if sample:
        h_shape = jax.ShapeDtypeStruct((m, d_ff), F32)
        h_spec = pl.BlockSpec((tm, tf), lambda i, f: (i, f))
    else:
        h_shape = jax.ShapeDtypeStruct((m // tm, SUBLANES, d_ff), F32)
        h_spec = pl.BlockSpec((None, SUBLANES, tf), lambda i, f: (i, 0, f))
        scratch.append(pltpu.VMEM((n_f, SUBLANES, tf), F32))
    body = functools.partial(_ffn_body, sample=sample, final_norm=final_norm, period=seq_len,
                             tiles_per_seq=max(seq_len // tm, 1), n_f=n_f)
    return pl.pallas_call(
        body,
        grid=(m // tm, n_f),
        in_specs=in_specs,
        out_specs=[pl.BlockSpec((tm, d), lambda i, f: (i, 0)), h_spec],
        out_shape=[jax.ShapeDtypeStruct((m, d), F32), h_shape],
        scratch_shapes=scratch,
        compiler_params=_cparams(("arbitrary", "arbitrary")),
        name=name,
    )(*args)


def _compress_body(*refs, n_pages, paged):
    n_main = n_pages * SUBLANES
    half = N_KV_HEADS * HEAD_DIM
    n_cb = KV_ROW // LANES
    if paged:
        page_refs = refs[1:n_pages + 2]
        pe_ref, w_ref, ck_ref, cv_ref, xs = refs[n_pages + 2:]
        out_row = lax.broadcasted_iota(jnp.int32, (PAGE, PAGE), 0)
        src_row = lax.broadcasted_iota(jnp.int32, (PAGE, PAGE), 1)
        wanted = (out_row & (SUBLANES - 1)) * CMP_STRIDE + (out_row >> (SUBLANES.bit_length() - 1))
        perm = jnp.where(src_row == wanted, 1.0, 0.0).astype(BF16)
        for k in range(n_pages + 1):
            for kv in range(2):
                page_t = page_refs[k][kv].reshape(half, PAGE).astype(BF16)
                xs[kv, k * PAGE:(k + 1) * PAGE, :] = _dot_nt(perm, page_t)

        def seg_rows(r, kv):
            return jnp.concatenate(
                [xs[kv, k * PAGE + r * SUBLANES:k * PAGE + (r + 1) * SUBLANES, :]
                 for k in range(n_pages + 1)], axis=0)
    else:
        main_refs = refs[:n_cb]
        next_refs = refs[n_cb:2 * n_cb]
        pe_ref, w_ref, ck_ref, cv_ref = refs[2 * n_cb:]

        def seg_rows(r, kv):
            return jnp.concatenate(
                [jnp.concatenate([main_refs[cb][pl.ds(r, n_main, stride=CMP_STRIDE), :],
                                  next_refs[cb][pl.ds(r, SUBLANES, stride=CMP_STRIDE), :]], axis=0)
                 for cb in (2 * kv, 2 * kv + 1)], axis=1)

    for kv, o_ref in ((0, ck_ref), (1, cv_ref)):
        acc = None
        for r in range(CMP_STRIDE):
            lhs = jnp.concatenate(
                [seg_rows(r, kv), pe_ref[:, r * KV_ROW + kv * half:r * KV_ROW + (kv + 1) * half]], axis=0)
            t = _dot(lhs.astype(BF16), w_ref[r, kv])
            acc = t if acc is None else acc + t
        n_tot = acc.shape[0]
        first = acc[:, :half] + acc[n_main + SUBLANES:n_main + SUBLANES + 1, :half]
        second = acc[:, half:] + acc[n_main + SUBLANES + 1:n_main + SUBLANES + 2, half:]
        second = pltpu.roll(second, n_tot - 1, 0)
        o_ref[...] = (first + second)[:n_main, :]


def _compress_rows(rows, nb, pe_seg, w_big, n_pages_step, name):
    n_p = rows.shape[0] // (nb * PAGE)
    assert n_p % n_pages_step == 0
    n_steps = n_p // n_pages_step
    half = N_KV_HEADS * HEAD_DIM
    n_main = n_pages_step * SUBLANES
    out_spec = pl.BlockSpec((None, n_main, half), lambda b, s: (b, s, 0))
    n_cb = KV_ROW // LANES
    main_specs = [pl.BlockSpec((n_pages_step * PAGE, LANES), lambda b, s, cb=cb: (b * n_steps + s, cb))
                  for cb in range(n_cb)]
    next_specs = [pl.BlockSpec((PAGE, LANES), lambda b, s, cb=cb:
                               (b * n_p + jnp.minimum((s + 1) * n_pages_step, n_p - 1), cb))
                  for cb in range(n_cb)]
    return pl.pallas_call(
        functools.partial(_compress_body, n_pages=n_pages_step, paged=False),
        grid=(nb, n_steps),
        in_specs=main_specs + next_specs + [_const_spec(pe_seg.shape), _const_spec(w_big.shape)],
        out_specs=[out_spec, out_spec],
        out_shape=[jax.ShapeDtypeStruct((nb, n_p * SUBLANES, half), F32)] * 2,
        compiler_params=_cparams(("parallel", "parallel")),
        name=name,
    )(*([rows] * (2 * n_cb)), pe_seg, w_big)


def _compress_paged(pages_t, page_index, nb, n_p, page_table, pe_seg, w_big, n_pages_step, name):
    assert n_p % n_pages_step == 0
    n_steps = n_p // n_pages_step
    half = N_KV_HEADS * HEAD_DIM
    n_main = n_pages_step * SUBLANES
    page_block = (None, 2, N_KV_HEADS, HEAD_DIM, PAGE)

    def page_map(k):
        def index(b, s, pt):
            return page_index(b, jnp.minimum(s * n_pages_step + k, n_p - 1), pt)
        return index

    in_specs = [pl.BlockSpec(page_block, page_map(k)) for k in range(n_pages_step + 1)]
    in_specs += [pl.BlockSpec(pe_seg.shape, lambda b, s, pt: (0, 0), pipeline_mode=pl.Buffered(1)),
                 pl.BlockSpec(w_big.shape, lambda b, s, pt: (0, 0, 0, 0), pipeline_mode=pl.Buffered(1))]
    out_spec = pl.BlockSpec((None, n_main, half), lambda b, s, pt: (b, s, 0))
    grid_spec = pltpu.PrefetchScalarGridSpec(
        num_scalar_prefetch=1, grid=(nb, n_steps), in_specs=in_specs,
        out_specs=[out_spec, out_spec],
        scratch_shapes=[pltpu.VMEM((2, (n_pages_step + 1) * PAGE, half), F32)])
    return pl.pallas_call(
        functools.partial(_compress_body, n_pages=n_pages_step, paged=True),
        grid_spec=grid_spec,
        out_shape=[jax.ShapeDtypeStruct((nb, n_p * SUBLANES, half), F32)] * 2,
        compiler_params=_cparams(("parallel", "parallel")),
        name=name,
    )(page_table.reshape(-1), *([pages_t] * (n_pages_step + 1)), pe_seg, w_big)


def _cmp_select_body(q_ref, ck_ref, cv_ref, ov_ref, slope_ref, oc_ref, sel_ref, *, q_off, n_sel):
    tq = q_ref.shape[0]
    rows = HEADS_PER_KV * tq
    nc = ck_ref.shape[0]
    ns = ov_ref.shape[1]
    base = q_off + pl.program_id(1) * tq
    tok = lax.broadcasted_iota(jnp.int32, (rows, 1), 0) & (tq - 1)
    pos = base + tok
    cpos = lax.broadcasted_iota(jnp.int32, (1, nc), 1) * CMP_STRIDE + (CMP_LEN - 1)
    d_c = (pos - cpos).astype(F32)
    ok = d_c >= 0.0
    imps = []
    for g in range(N_KV_HEADS):
        qg = jnp.concatenate(
            [q_ref[:, (g * HEADS_PER_KV + h) * SLOT:(g * HEADS_PER_KV + h + 1) * SLOT]
             for h in range(HEADS_PER_KV)], axis=0)
        s = _dot_nt(qg, ck_ref[:, g * SLOT:(g + 1) * SLOT]) - slope_ref[g] * d_c
        s = jnp.where(ok, s, NEG)
        e = jnp.where(ok, jnp.exp2(s - jnp.max(s, axis=-1, keepdims=True)), 0.0)
        l = jnp.sum(e, axis=-1, keepdims=True)
        p = e * jnp.where(l > 0.0, 1.0 / l, 0.0)
        o = _dot(p.astype(BF16), cv_ref[:, g * SLOT:(g + 1) * SLOT])
        psum = p[0:tq]
        for h in range(1, HEADS_PER_KV):
            psum = psum + p[h * tq:(h + 1) * tq]
        for h in range(HEADS_PER_KV):
            c0 = (g * HEADS_PER_KV + h) * SLOT
            oc_ref[:, c0:c0 + SLOT] = o[h * tq:(h + 1) * tq].astype(oc_ref.dtype)
        hi = psum.astype(BF16)
        rem = psum - hi.astype(F32)
        mid = rem.astype(BF16)
        lo = (rem - mid.astype(F32)).astype(BF16)
        ov = ov_ref[...]
        imps.append(_dot(hi, ov) + _dot(mid, ov) + _dot(lo, ov))
    imp = jnp.concatenate(imps, axis=0)
    blk = lax.broadcasted_iota(jnp.int32, (1, ns), 1)
    cur = pos >> SLC_SHIFT
    forced = (blk == 0) | (blk == cur) | (blk == cur - 1)
    imp = jnp.where(forced, FORCE, imp)
    imp = jnp.where(blk > cur, NEG, imp)
    blk_f = blk.astype(F32)
    sel = jnp.zeros(imp.shape, F32)
    for _ in range(n_sel):
        top = jnp.max(imp, axis=-1, keepdims=True)
        first = jnp.min(jnp.where(imp == top, blk_f, float(ns)), axis=-1, keepdims=True)
        hit = blk_f == first
        sel = jnp.where(hit, 1.0, sel)
        imp = jnp.where(hit, BELOW_NEG, imp)
    for g in range(N_KV_HEADS):
        sel_ref[:, g * ns:(g + 1) * ns] = sel[g * tq:(g + 1) * tq].astype(sel_ref.dtype)


def _cmp_select(q_slot, ck_slot, cv_slot, overlap, slope_rows, tq, q_off, n_sel, name):
    m = q_slot.shape[0]
    nb, nc, _ = ck_slot.shape
    ns = overlap.shape[1]
    nq = m // (nb * tq)
    assert (tq & (tq - 1)) == 0
    return pl.pallas_call(
        functools.partial(_cmp_select_body, q_off=q_off, n_sel=n_sel),
        grid=(nb, nq),
        in_specs=[pl.BlockSpec((tq, N_HEADS * SLOT), lambda b, i: (b * nq + i, 0)),
                  pl.BlockSpec((None, nc, N_KV_HEADS * SLOT), lambda b, i: (b, 0, 0)),
                  pl.BlockSpec((None, nc, N_KV_HEADS * SLOT), lambda b, i: (b, 0, 0)),
                  _const_spec(overlap.shape), _const_spec(slope_rows.shape)],
        out_specs=[pl.BlockSpec((tq, N_HEADS * SLOT), lambda b, i: (b * nq + i, 0)),
                   pl.BlockSpec((tq, N_KV_HEADS * ns), lambda b, i: (b * nq + i, 0))],
        out_shape=[jax.ShapeDtypeStruct((m, N_HEADS * SLOT), BF16),
                   jax.ShapeDtypeStruct((m, N_KV_HEADS * ns), BF16)],
        compiler_params=_cparams(("parallel", "parallel")),
        name=name,
    )(q_slot, ck_slot, cv_slot, overlap, slope_rows)


def _cmp_select_t_body(q_ref, ck_ref, cvt_ref, ovt_ref, sext_ref, pext_ref, oc_ref, sel_ref, *, q_off, n_sel):
    tq = q_ref.shape[0]
    rows = HEADS_PER_KV * tq
    nc = ck_ref.shape[0]
    ns = ovt_ref.shape[0]
    base = q_off + pl.program_id(1) * tq
    pos = base + (lax.broadcasted_iota(jnp.int32, (1, rows), 1) & (tq - 1))
    cpos = lax.broadcasted_iota(jnp.int32, (nc, 1), 0) * CMP_STRIDE + (CMP_LEN - 1)
    ok = cpos <= pos
    imps = []
    for g in range(N_KV_HEADS):
        cs = slice(g * SLOT, (g + 1) * SLOT)
        qg = jnp.concatenate(
            [q_ref[:, (g * HEADS_PER_KV + h) * SLOT:(g * HEADS_PER_KV + h + 1) * SLOT]
             for h in range(HEADS_PER_KV)], axis=0) + sext_ref[g]
        s = jnp.where(ok, _dot_nt(ck_ref[:, cs] + pext_ref[...], qg), NEG)
        e = jnp.where(ok, jnp.exp2(s - jnp.max(s, axis=0, keepdims=True)), 0.0)
        l = jnp.sum(e, axis=0, keepdims=True)
        p = e * jnp.where(l > 0.0, 1.0 / l, 0.0)
        o = _dot(cvt_ref[cs, :], p.astype(BF16)).T
        for h in range(HEADS_PER_KV):
            c0 = (g * HEADS_PER_KV + h) * SLOT
            oc_ref[:, c0:c0 + SLOT] = o[h * tq:(h + 1) * tq].astype(oc_ref.dtype)
        psum = p[:, 0:tq]
        for h in range(1, HEADS_PER_KV):
            psum = psum + p[:, h * tq:(h + 1) * tq]
        hi = psum.astype(BF16)
        rem = psum - hi.astype(F32)
        mid = rem.astype(BF16)
        lo = (rem - mid.astype(F32)).astype(BF16)
        ovt = ovt_ref[...]
        imps.append(_dot(ovt, hi) + _dot(ovt, mid) + _dot(ovt, lo))
    imp = jnp.concatenate(imps, axis=1)
    blk = lax.broadcasted_iota(jnp.int32, (ns, 1), 0)
    cur = pos >> SLC_SHIFT
    forced = (blk == 0) | (blk == cur) | (blk == cur - 1)
    imp = jnp.where(forced, FORCE, imp)
    imp = jnp.where(blk > cur, NEG, imp)
    blk_f = blk.astype(F32)
    sel = jnp.zeros(imp.shape, F32)
    for _ in range(n_sel):
        top = jnp.max(imp, axis=0, keepdims=True)
        first = jnp.min(jnp.where(imp == top, blk_f, float(ns)), axis=0, keepdims=True)
        hit = blk_f == first
        sel = jnp.where(hit, 1.0, sel)
        imp = jnp.where(hit, BELOW_NEG, imp)
    for g in range(N_KV_HEADS):
        sel_ref[:, g * ns:(g + 1) * ns] = sel[:, g * tq:(g + 1) * tq].T.astype(sel_ref.dtype)


def _cmp_select_t(q_slot, ck_slot, cvt_slot, overlap_t, tq, q_off, n_sel, name):
    m = q_slot.shape[0]
    nb, nc, _ = ck_slot.shape
    ns = overlap_t.shape[0]
    nq = m // (nb * tq)
    assert tq % LANES == 0 and (tq & (tq - 1)) == 0
    sext, pext = _alibi_lanes(np.arange(nc) * CMP_STRIDE + (CMP_LEN - 1), tq)
    return pl.pallas_call(
        functools.partial(_cmp_select_t_body, q_off=q_off, n_sel=n_sel),
        grid=(nb, nq),
        in_specs=[pl.BlockSpec((tq, N_HEADS * SLOT), lambda b, i: (b * nq + i, 0)),
                  pl.BlockSpec((None, nc, N_KV_HEADS * SLOT), lambda b, i: (b, 0, 0)),
                  pl.BlockSpec((None, N_KV_HEADS * SLOT, nc), lambda b, i: (b, 0, 0)),
                  _const_spec(overlap_t.shape), _const_spec(sext.shape), _const_spec(pext.shape)],
        out_specs=[pl.BlockSpec((tq, N_HEADS * SLOT), lambda b, i: (b * nq + i, 0)),
                   pl.BlockSpec((tq, N_KV_HEADS * ns), lambda b, i: (b * nq + i, 0))],
        out_shape=[jax.ShapeDtypeStruct((m, N_HEADS * SLOT), BF16),
                   jax.ShapeDtypeStruct((m, N_KV_HEADS * ns), BF16)],
        compiler_params=_cparams(("parallel", "parallel")),
        name=name,
    )(q_slot, ck_slot, cvt_slot, overlap_t, sext, pext)


def _slc_body(q_ref, sel_ref, k_ref, vt_ref, sext_ref, pext_ref, hot_ref, o_ref,
              qa_scr, m_scr, acc_scr, *, tk):
    i = pl.program_id(1)
    tq = q_ref.shape[0]
    rows = HEADS_PER_KV * tq
    ns = hot_ref.shape[1]
    for g in range(N_KV_HEADS):
        qg = jnp.concatenate(
            [q_ref[:, (g * HEADS_PER_KV + h) * SLOT:(g * HEADS_PER_KV + h + 1) * SLOT]
             for h in range(HEADS_PER_KV)], axis=0) + sext_ref[g]
        bias = ((sel_ref[:, g * ns:(g + 1) * ns].astype(F32) - 1.0) * MASK_BIG).astype(BF16)
        qa_scr[g] = jnp.concatenate([qg, jnp.concatenate([bias] * HEADS_PER_KV, axis=0)], axis=1)
    m_scr[...] = jnp.full_like(m_scr, NEG)
    acc_scr[...] = jnp.zeros_like(acc_scr)

    def step(j, diagonal):
        k0 = pl.multiple_of(j * tk, tk)
        kext = pext_ref[pl.ds(k0, tk), :]
        hot = hot_ref[pl.ds(k0, tk), :]
        if diagonal:
            pos = i * tq + (lax.broadcasted_iota(jnp.int32, (1, rows), 1) & (tq - 1))
            causal = (k0 + lax.broadcasted_iota(jnp.int32, (tk, 1), 0)) <= pos
        ones_row = lax.broadcasted_iota(jnp.int32, (SLOT, 1), 0) == SUM_ROW

        def scores(g):
            cs = slice(g * SLOT, (g + 1) * SLOT)
            ka = jnp.concatenate([k_ref[pl.ds(k0, tk), cs] + kext, hot], axis=1)
            s = _dot_nt(ka, qa_scr[g])
            return jnp.where(causal, s, NEG) if diagonal else s

        s_next = scores(0)
        for g in range(N_KV_HEADS):
            s = s_next
            if g + 1 < N_KV_HEADS:
                s_next = scores(g + 1)
            cs = slice(g * SLOT, (g + 1) * SLOT)
            m_old = m_scr[g]
            m_new = jnp.maximum(m_old, jnp.max(s, axis=0, keepdims=True))
            p = jnp.exp2(s - m_new).astype(BF16)
            vt = jnp.where(ones_row, 1.0, vt_ref[j, cs, :]).astype(BF16)
            acc_scr[g] = jnp.exp2(m_old - m_new) * acc_scr[g] + _dot(vt, p)
            m_scr[g] = m_new

    n_full = (i * tq) // tk

    def full_step(j, carry):
        step(j, False)
        return carry

    lax.fori_loop(0, n_full, full_step, 0)
    step(n_full, True)
    for g in range(N_KV_HEADS):
        acc = acc_scr[g]
        o = (acc * (1.0 / acc[SUM_ROW:SUM_ROW + 1, :])).T
        for h in range(HEADS_PER_KV):
            c0 = (g * HEADS_PER_KV + h) * SLOT
            o_ref[:, c0:c0 + SLOT] = o[h * tq:(h + 1) * tq].astype(o_ref.dtype)


def _slc_attention(q_slot, sel, k_slot, vt_tiles, nb, seq_len, tq, tk, name):
    nq = seq_len // tq
    nk = seq_len // tk
    rows = HEADS_PER_KV * tq
    ns = sel.shape[1] // N_KV_HEADS
    assert seq_len % tk == 0 and tk % tq == 0 and seq_len <= ns * SLC_BLOCK
    assert vt_tiles.shape == (nb * nk, N_KV_HEADS * SLOT, tk)
    sext, pext, hot = _slc_constants(seq_len, tq, ns)
    once = pl.Buffered(1)
    return pl.pallas_call(
        functools.partial(_slc_body, tk=tk),
        grid=(nb, nq),
        in_specs=[pl.BlockSpec((tq, N_HEADS * SLOT), lambda b, i: (b * nq + i, 0)),
                  pl.BlockSpec((tq, N_KV_HEADS * ns), lambda b, i: (b * nq + i, 0)),
                  pl.BlockSpec((seq_len, N_KV_HEADS * SLOT), lambda b, i: (b, 0), pipeline_mode=once),
                  pl.BlockSpec((nk, N_KV_HEADS * SLOT, tk), lambda b, i: (b, 0, 0), pipeline_mode=once),
                  pl.BlockSpec(sext.shape, lambda b, i: (0, 0, 0), pipeline_mode=once),
                  pl.BlockSpec(pext.shape, lambda b, i: (0, 0), pipeline_mode=once),
                  pl.BlockSpec(hot.shape, lambda b, i: (0, 0), pipeline_mode=once)],
        out_specs=pl.BlockSpec((tq, N_HEADS * SLOT), lambda b, i: (b * nq + i, 0)),
        out_shape=jax.ShapeDtypeStruct(q_slot.shape, BF16),
        scratch_shapes=[pltpu.VMEM((N_KV_HEADS, rows, SLOT + ns), BF16),
                        pltpu.VMEM((N_KV_HEADS, 1, rows), F32),
                        pltpu.VMEM((N_KV_HEADS, SLOT, rows), F32)],
        compiler_params=_cparams(("parallel", "arbitrary")),
        name=name,
    )(q_slot, sel, k_slot, vt_tiles, sext, pext, hot)


def _win_body(q_ref, *refs, n_blk):
    k_refs = refs[:n_blk]
    vt_refs = refs[n_blk:2 * n_blk]
    sext_ref, pext_ref, band_ref, o_ref = refs[2 * n_blk:]
    tq = q_ref.shape[0]
    i = pl.program_id(1)
    n_keys = n_blk * tq
    key = lax.broadcasted_iota(jnp.int32, (n_keys, 1), 0)
    bias = band_ref[...] + jnp.where(key >= ((n_blk - 1) - i) * tq, 0.0, NEG)
    ones_row = lax.broadcasted_iota(jnp.int32, (SLOT, 1), 0) == SUM_ROW
    for g in range(N_KV_HEADS):
        cs = slice(g * SLOT, (g + 1) * SLOT)
        qg = jnp.concatenate(
            [q_ref[:, (g * HEADS_PER_KV + h) * SLOT:(g * HEADS_PER_KV + h + 1) * SLOT]
             for h in range(HEADS_PER_KV)], axis=0) + sext_ref[g]
        kg = jnp.concatenate([k_refs[k][:, cs] for k in range(n_blk)], axis=0) + pext_ref[...]
        s = _dot_nt(kg, qg) + bias
        p = jnp.exp2(s - jnp.max(s, axis=0, keepdims=True)).astype(BF16)
        vt = jnp.concatenate([vt_refs[k][cs, :] for k in range(n_blk)], axis=1)
        vt = jnp.where(ones_row, 1.0, vt).astype(BF16)
        acc = _dot(vt, p)
        o = (acc * (1.0 / acc[SUM_ROW:SUM_ROW + 1, :])).T
        for h in range(HEADS_PER_KV):
            c0 = (g * HEADS_PER_KV + h) * SLOT
            o_ref[:, c0:c0 + SLOT] = o[h * tq:(h + 1) * tq].astype(o_ref.dtype)


def _win_attention(q_slot, k_slot, vt_tiles, nb, seq_len, tq, name):
    nq = seq_len // tq
    n_blk = WINDOW // tq + 1
    n_keys = n_blk * tq
    sext, pext, _ = _slc_constants(n_keys, tq, LANES)
    dist = WINDOW + (np.arange(HEADS_PER_KV * tq) % tq)[None, :] - np.arange(n_keys)[:, None]
    band = jnp.asarray(np.where((dist >= 0) & (dist < WINDOW), 0.0, NEG).astype(np.float32))

    def block_map(k, transposed):
        def index(b, i):
            blk = b * nq + jnp.maximum(i - (n_blk - 1) + k, 0)
            return (blk, 0, 0) if transposed else (blk, 0)
        return index

    k_specs = [pl.BlockSpec((tq, N_KV_HEADS * SLOT), block_map(k, False)) for k in range(n_blk)]
    vt_specs = [pl.BlockSpec((None, N_KV_HEADS * SLOT, tq), block_map(k, True)) for k in range(n_blk)]
    return pl.pallas_call(
        functools.partial(_win_body, n_blk=n_blk),
        grid=(nb, nq),
        in_specs=[pl.BlockSpec((tq, N_HEADS * SLOT), lambda b, i: (b * nq + i, 0))]
        + k_specs + vt_specs + [_const_spec(sext.shape), _const_spec(pext.shape), _const_spec(band.shape)],
        out_specs=pl.BlockSpec((tq, N_HEADS * SLOT), lambda b, i: (b * nq + i, 0)),
        out_shape=jax.ShapeDtypeStruct(q_slot.shape, BF16),
        compiler_params=_cparams(("parallel", "parallel")),
        name=name,
    )(q_slot, *([k_slot] * n_blk), *([vt_tiles] * n_blk), sext, pext, band)


def _paged_attn_body(pt_ref, *refs, n_pages, n_steps, past_base, q_off, tokens, window, use_sel):
    page_refs = refs[:n_pages]
    refs = refs[n_pages:]
    new_ref, q_ref, slope_ref = refs[:3]
    refs = refs[3:]
    if use_sel:
        sel_ref = refs[0]
        refs = refs[1:]
    o_ref, m_scr, l_scr, acc_scr = refs
    s_id = pl.program_id(1)
    rows = q_ref.shape[0]
    half = N_KV_HEADS * HEAD_DIM

    @pl.when(s_id == 0)
    def _():
        m_scr[...] = jnp.full_like(m_scr, NEG)
        l_scr[...] = jnp.zeros_like(l_scr)
        acc_scr[...] = jnp.zeros_like(acc_scr)

    pos = q_off + (lax.broadcasted_iota(jnp.int32, (rows, 1), 0) & (tokens - 1))
    q = q_ref[...]
    key = lax.broadcasted_iota(jnp.int32, (1, PAGE), 1)

    def update(pages, kpos0s, blk0s):
        ss, keeps = [], []
        for ref, kpos0, blk0 in zip(pages, kpos0s, blk0s):
            d = pos - (kpos0 + key)
            valid = d >= 0
            if window is not None:
                valid = valid & (d < window)
            if use_sel:
                ns = sel_ref.shape[1]
                blk_row = lax.broadcasted_iota(jnp.int32, (ns, PAGE), 0)
                key_blk = blk0 + (lax.broadcasted_iota(jnp.int32, (ns, PAGE), 1) >> SLC_SHIFT)
                expand = jnp.where(blk_row == key_blk, 1.0, 0.0).astype(BF16)
                valid = valid & (_dot(sel_ref[...], expand) > 0.5)
            k_t = ref[0].reshape(half, PAGE).astype(BF16)
            s = _dot(q, k_t) - slope_ref[...] * d.astype(F32)
            ss.append(jnp.where(valid, s, NEG))
            keeps.append(jnp.where(valid, 1.0, 0.0))
        s = jnp.concatenate(ss, axis=1)
        keep = jnp.concatenate(keeps, axis=1)
        m_old = m_scr[...]
        m_new = jnp.maximum(m_old, jnp.max(s, axis=-1, keepdims=True))
        p = jnp.exp2(s - m_new) * keep
        alpha = jnp.exp2(m_old - m_new)
        l_scr[...] = alpha * l_scr[...] + jnp.sum(p, axis=-1, keepdims=True)
        p = p.astype(BF16)
        acc = alpha * acc_scr[...]
        for k, ref in enumerate(pages):
            v_t = ref[1].reshape(half, PAGE).astype(BF16)
            acc = acc + _dot_nt(p[:, k * PAGE:(k + 1) * PAGE], v_t)
        acc_scr[...] = acc
        m_scr[...] = m_new

    first_page = s_id * n_pages
    update(page_refs,
           [past_base + (first_page + k) * PAGE for k in range(n_pages)],
           [(first_page + k) * (PAGE // SLC_BLOCK) for k in range(n_pages)])

    @pl.when(s_id == n_steps - 1)
    def _():
        update([new_ref], [q_off], [(q_off - past_base) // SLC_BLOCK])
        o_ref[...] = acc_scr[...] * (1.0 / l_scr[...])


def _paged_attention(pages_t, page_index, n_p, new_t, q_bd, slope_rows, sel_rows, n_pages_step,
                     past_base, q_off, tokens, window, table, name):
    nb, rows, half = q_bd.shape
    assert n_p % n_pages_step == 0 and (tokens & (tokens - 1)) == 0
    n_steps = n_p // n_pages_step
    use_sel = sel_rows is not None
    page_block = (None,) * (pages_t.ndim - 4) + (2, N_KV_HEADS, HEAD_DIM, PAGE)

    def page_map(k):
        def index(b, s, pt):
            return page_index(b, s * n_pages_step + k, pt)
        return index

    in_specs = [pl.BlockSpec(page_block, page_map(k)) for k in range(n_pages_step)]
    in_specs += [pl.BlockSpec((None, 2, N_KV_HEADS, HEAD_DIM, PAGE), lambda b, s, pt: (b, 0, 0, 0, 0)),
                 pl.BlockSpec((None, rows, half), lambda b, s, pt: (b, 0, 0)),
                 pl.BlockSpec(slope_rows.shape, lambda b, s, pt: (0, 0))]
    args = [pages_t] * n_pages_step + [new_t, q_bd, slope_rows]
    if use_sel:
        in_specs.append(pl.BlockSpec((None,) + sel_rows.shape[1:], lambda b, s, pt: (b, 0, 0)))
        args.append(sel_rows)
    grid_spec = pltpu.PrefetchScalarGridSpec(
        num_scalar_prefetch=1, grid=(nb, n_steps), in_specs=in_specs,
        out_specs=pl.BlockSpec((None, rows, half), lambda b, s, pt: (b, 0, 0)),
        scratch_shapes=[pltpu.VMEM((rows, 1), F32), pltpu.VMEM((rows, 1), F32),
                        pltpu.VMEM((rows, half), F32)])
    body = functools.partial(_paged_attn_body, n_pages=n_pages_step, n_steps=n_steps,
                             past_base=past_base, q_off=q_off, tokens=tokens, window=window,
                             use_sel=use_sel)
    return pl.pallas_call(
        body,
        grid_spec=grid_spec,
        out_shape=jax.ShapeDtypeStruct((nb, rows, half), F32),
        compiler_params=_cparams(("parallel", "arbitrary")),
        name=name,
    )(table.reshape(-1), *args)


def _combine_body(x_ref, gates_ref, oc_ref, os_ref, ow_ref, wo_ref, o_ref, mix_scr):
    gt = jax.nn.sigmoid(gates_ref[...])
    for hh in range(N_HEADS):
        cs = slice(hh * SLOT, (hh + 1) * SLOT)
        mix = gt[:, hh:hh + 1] * oc_ref[:, cs].astype(F32)
        mix = mix + gt[:, N_HEADS + hh:N_HEADS + hh + 1] * os_ref[:, cs].astype(F32)
        mix = mix + gt[:, 2 * N_HEADS + hh:2 * N_HEADS + hh + 1] * ow_ref[:, cs].astype(F32)
        mix_scr[:, cs] = mix.astype(BF16)
    o_ref[...] = x_ref[...] + _dot(mix_scr[...], wo_ref[...])


def _combine(x, gates, o_c, o_s, o_w, w_out_slot, tm, name):
    m, d = x.shape
    wide = N_HEADS * SLOT
    row_spec = lambda w: pl.BlockSpec((tm, w), lambda i: (i, 0))
    return pl.pallas_call(
        _combine_body,
        grid=(m // tm,),
        in_specs=[row_spec(d), row_spec(gates.shape[1]), row_spec(wide), row_spec(wide), row_spec(wide),
                  _const_spec(w_out_slot.shape)],
        out_specs=row_spec(d),
        out_shape=jax.ShapeDtypeStruct((m, d), F32),
        scratch_shapes=[pltpu.VMEM((tm, wide), BF16)],
        compiler_params=_cparams(("parallel",)),
        name=name,
    )(x, gates, o_c, o_s, o_w, w_out_slot)


def _slot_cols(w):
    d, n = w.shape
    w = w.reshape(d, n // HEAD_DIM, HEAD_DIM)
    return jnp.pad(w, ((0, 0), (0, 0), (0, SLOT - HEAD_DIM))).reshape(d, (n // HEAD_DIM) * SLOT)


def _slot_rows(w):
    n, d = w.shape
    w = w.reshape(n // HEAD_DIM, HEAD_DIM, d)
    return jnp.pad(w, ((0, 0), (0, SLOT - HEAD_DIM), (0, 0))).reshape((n // HEAD_DIM) * SLOT, d)


def _slot_last(a):
    lead = a.shape[:-1]
    n = a.shape[-1] // HEAD_DIM
    a = a.reshape(lead + (n, HEAD_DIM))
    a = jnp.pad(a, [(0, 0)] * len(lead) + [(0, 0), (0, SLOT - HEAD_DIM)])
    return a.reshape(lead + (n * SLOT,))


def _alibi_slopes():
    h = np.arange(1, N_HEADS + 1, dtype=np.float32)
    slopes = np.asarray(2.0 ** (-8.0 * h / N_HEADS), dtype=np.float32)
    return (slopes * np.float32(LOG2E)).reshape(N_KV_HEADS, HEADS_PER_KV)


def _slope_rows_by_group(tq):
    return jnp.asarray(np.repeat(_alibi_slopes(), tq, axis=1)[:, :, None])


def _bf16_parts(x):
    x = np.asarray(x, np.float32)
    parts = []
    for _ in range(3):
        part = x.astype(jnp.bfloat16)
        parts.append(part)
        x = x - part.astype(np.float32)
    assert not np.any(x)
    return parts


def _alibi_lanes(kpos, tq):
    kpos = np.asarray(kpos)
    assert kpos.min() >= 0 and kpos.max() < 256 * LANES
    parts = _bf16_parts(np.repeat(_alibi_slopes(), tq, axis=1))
    sext = np.zeros((N_KV_HEADS, HEADS_PER_KV * tq, SLOT), jnp.bfloat16)
    pext = np.zeros((len(kpos), SLOT), np.float32)
    for n, part in enumerate(parts):
        sext[:, :, ALIBI_LANE + 2 * n] = part
        sext[:, :, ALIBI_LANE + 2 * n + 1] = part
        pext[:, ALIBI_LANE + 2 * n] = (kpos // LANES) * LANES
        pext[:, ALIBI_LANE + 2 * n + 1] = kpos % LANES
    return jnp.asarray(sext), jnp.asarray(pext, dtype=BF16)


def _slc_constants(seq_len, tq, ns):
    kpos = np.arange(seq_len)
    sext, pext = _alibi_lanes(kpos, tq)
    hot = (kpos[:, None] // SLC_BLOCK) == np.arange(ns)[None, :]
    return sext, pext, jnp.asarray(hot.astype(np.float32), dtype=BF16)


def _overlap(n_cmp, n_slc, nc, ns):
    st = np.arange(nc) * CMP_STRIDE
    bs = np.arange(ns) * SLC_BLOCK
    m = (st[:, None] < bs[None, :] + SLC_BLOCK) & (st[:, None] + CMP_LEN > bs[None, :])
    m = m & (np.arange(nc)[:, None] < n_cmp) & (np.arange(ns)[None, :] < n_slc)
    return jnp.asarray(m.astype(np.float32), dtype=BF16)


def _compress_weights(cmp_pe, cmp_w):
    eye = jnp.eye(N_KV_HEADS, dtype=F32)
    w = cmp_w.reshape(2, 2, CMP_STRIDE, HEAD_DIM, HEAD_DIM)
    bd = jnp.einsum('kmrde,gh->rkgdmhe', w, eye)
    half = N_KV_HEADS * HEAD_DIM
    w_big = bd.reshape(CMP_STRIDE, 2, half, 2 * half).astype(BF16)
    pe = cmp_pe.reshape(2, 2, CMP_STRIDE, HEAD_DIM)
    pe = jnp.broadcast_to(pe.transpose(1, 2, 0, 3)[:, :, :, None, :],
                          (2, CMP_STRIDE, 2, N_KV_HEADS, HEAD_DIM)).reshape(2, SEG_W)
    pe_seg = jnp.pad(pe, ((0, SUBLANES - 2), (0, 0)))
    return pe_seg, w_big


def _nsa_weights(w_in, segs_slot_kv):
    q_dim = N_HEADS * HEAD_DIM
    half = N_KV_HEADS * HEAD_DIM
    w_q = w_in[:, :q_dim] * (HEAD_DIM ** -0.5 * LOG2E)
    w_gate = w_in[:, q_dim + N_BRANCH * KV_ROW:]
    w_kv = w_in[:, q_dim:q_dim + N_BRANCH * KV_ROW]
    w_gate = jnp.pad(w_gate, ((0, 0), (0, LANES - w_gate.shape[1])))
    if not segs_slot_kv:
        segs = [(N_HEADS * SLOT, BF16, False), (KV_ROW, F32, False), (KV_ROW, F32, False),
                (KV_ROW, F32, False), (LANES, F32, False)]
        return jnp.concatenate([_slot_cols(w_q), w_kv, w_gate], axis=1).astype(BF16), segs, None
    o_slc = q_dim + KV_ROW
    o_win = q_dim + 2 * KV_ROW
    parts = [_slot_cols(w_q), w_gate,
             _slot_cols(w_in[:, o_slc:o_slc + half]), _slot_cols(w_in[:, o_win:o_win + half])]
    segs = [(N_HEADS * SLOT, BF16, False), (LANES, F32, False)] + [(N_KV_HEADS * SLOT, BF16, False)] * 2
    w_t = jnp.concatenate([_slot_cols(w_in[:, o_slc + half:o_slc + KV_ROW]),
                           _slot_cols(w_in[:, o_win + half:o_win + KV_ROW]), w_kv], axis=1).T.astype(BF16)
    return jnp.concatenate(parts, axis=1).astype(BF16), segs, w_t


def _sgu_layer(x, p, a, seq_len, tm, emit_v, tag):
    d = x.shape[1]
    d_sgu = p['sgu_w_out'].shape[1]
    gw = d_sgu // N_SGU_GROUPS
    segs = [(d_sgu, BF16, True), (d_sgu, BF16, True)]
    u, v = _proj(x, p['norm_mix'][0], p['sgu_w_in'][a].astype(BF16), p['sgu_b_in'][a].reshape(1, -1),
                 segs, tm, f"sgu_in_{tag}")
    w_s, b_s = p['sgu_w_s'][a], p['sgu_b_s'][a]
    if seq_len % CHUNK == 0:
        bs_rows = b_s.T
    else:
        assert CHUNK % seq_len == 0
        rep = CHUNK // seq_len
        w_s = jnp.einsum('ab,gts->gatbs', jnp.eye(rep, dtype=F32), w_s[:, :seq_len, :seq_len])
        w_s = w_s.reshape(N_SGU_GROUPS, CHUNK, CHUNK)
        bs_rows = jnp.tile(b_s[:, :seq_len].T, (rep, 1))
    bs_exp = jnp.repeat(bs_rows, gw, axis=1)
    return _sgu(x, u, v, p['sgu_v_norm'][a], w_s, bs_exp, p['sgu_w_out'][a].astype(BF16), tm, emit_v,
                f"sgu_{tag}")


def _ffn_layer(x, p, layer, seq_len, past, final, tm, tag):
    gamma_final = p['norm_final'] if final else None
    return _ffn(x, p['norm_ffn'][layer], p['ffn_w_in'][layer].astype(BF16), p['ffn_conv_w'][layer],
                p['ffn_conv_b'][layer], p['ffn_w_out'][layer].astype(BF16), seq_len, past, gamma_final,
                tm, 256, f"ffn{layer}_{tag}")


def _nsa_prompt(x, p, b_idx, nb, seq_len, tm):
    w_ext, segs, w_t = _nsa_weights(p['nsa_w_in'][b_idx], True)
    tq = 128
    wide = N_KV_HEADS * SLOT
    q_slot, gates, ks_slot, kw_slot, vs_tiles, vw_tiles, cmp_t, slc_t, win_t = _proj(
        x, p['norm_mix'][1], w_ext, None, segs, tm, "nsa_in_prompt", w_t,
        t_outs=((0, wide, tm, None), (wide, wide, tq, None)) + tuple(
            (2 * wide + br * KV_ROW, KV_ROW, tm, seq_len) for br in range(N_BRANCH)))
    rows_last = (nb, 2, N_KV_HEADS, HEAD_DIM, seq_len)
    cmp_t, slc_t, win_t = (a.reshape(rows_last) for a in (cmp_t, slc_t, win_t))
    n_pages = seq_len // PAGE
    pe_seg, w_big = _compress_weights(p['nsa_cmp_pe'][b_idx], p['nsa_cmp_w'][b_idx])
    ck, cv = _compress_paged(cmp_t, lambda b, pg, pt: (b, 0, 0, 0, pg), nb, n_pages,
                             jnp.zeros((1,), jnp.int32), pe_seg, w_big, min(n_pages, 16), "compress_prompt")
    n_seg = seq_len // CMP_STRIDE
    n_cmp = n_seg - CMP_LEN // CMP_STRIDE + 1
    n_slc = -(-seq_len // SLC_BLOCK)
    ns = -(-n_slc // LANES) * LANES
    o_c, sel = _cmp_select_t(q_slot, _slot_last(ck).astype(BF16),
                             _slot_last(cv).astype(BF16).transpose(0, 2, 1),
                             _overlap(n_cmp, n_slc, n_seg, ns).T, tq, 0,
                             min(SLC_TOP_N, n_slc), "cmp_select_prompt")
    o_s = _slc_attention(q_slot, sel, ks_slot, vs_tiles, nb, seq_len, tq, tm, "slc_prompt")
    o_w = _win_attention(q_slot, kw_slot, vw_tiles, nb, seq_len, tq, "win_prompt")
    x = _combine(x, gates, o_c, o_s, o_w, _slot_rows(p['nsa_w_out'][b_idx]).astype(BF16), tm,
                 "nsa_out_prompt")
    rows_first = (0, 4, 1, 2, 3)
    win_t = win_t[..., seq_len - min(WINDOW, seq_len):]
    return x, cmp_t.transpose(rows_first), slc_t.transpose(rows_first), win_t.transpose(rows_first)


def _rows_ght(a, tokens_pad):
    nb, t, _, w = a.shape
    a = a.reshape(nb, t, N_KV_HEADS, HEADS_PER_KV, w).transpose(0, 2, 3, 1, 4)
    a = jnp.pad(a, ((0, 0), (0, 0), (0, 0), (0, tokens_pad - t), (0, 0)))
    return a.reshape(nb, N_HEADS * tokens_pad, w)


def _nsa_sample(x, p, b_idx, nb, t_new, past_len, cache_cmp, cache_slc, state_win, page_table):
    m = x.shape[0]
    half = N_KV_HEADS * HEAD_DIM
    t_pad = BF16_SUBLANES
    assert t_new <= t_pad and past_len % PAGE == 0
    w_ext, segs, _ = _nsa_weights(p['nsa_w_in'][b_idx], False)
    q_slot, kv_cmp, kv_slc, kv_win, gates = _proj(x, p['norm_mix'][1], w_ext, None, segs, m, "nsa_in_sample")
    n_pool = cache_cmp.shape[0]
    n_pages = past_len // PAGE
    total = past_len + t_new
    n_seg = total // CMP_STRIDE
    assert n_seg == past_len // CMP_STRIDE, "new rows must not complete a compression segment"
    n_cmp = n_seg - CMP_LEN // CMP_STRIDE + 1
    pe_seg, w_big = _compress_weights(p['nsa_cmp_pe'][b_idx], p['nsa_cmp_w'][b_idx])
    rows_last = (0, 2, 3, 4, 1)
    ck, cv = _compress_paged(cache_cmp.transpose(rows_last), lambda b, pg, pt: (pt[b * n_pages + pg], 0, 0, 0, 0),
                             nb, n_pages, page_table, pe_seg, w_big, min(n_pages, 32), "compress_sample")
    n_slc = -(-total // SLC_BLOCK)
    ns = -(-n_slc // LANES) * LANES
    q4 = q_slot.reshape(nb, t_new, N_HEADS * SLOT)
    q_pad = jnp.pad(q4, ((0, 0), (0, t_pad - t_new), (0, 0))).reshape(nb * t_pad, N_HEADS * SLOT)
    o_c, sel = _cmp_select(q_pad, _slot_last(ck).astype(BF16), _slot_last(cv).astype(BF16),
                           _overlap(n_cmp, n_slc, n_seg, ns), _slope_rows_by_group(t_pad), t_pad,
                           past_len, min(SLC_TOP_N, n_slc), "cmp_select_sample")
    o_c = o_c.reshape(nb, t_pad, N_HEADS * SLOT)[:, :t_new].reshape(m, N_HEADS * SLOT)
    assert (t_new & (t_new - 1)) == 0
    qh = q4.reshape(nb, t_new, N_HEADS, SLOT)[..., :HEAD_DIM]
    q_rows = _rows_ght(qh, t_new).reshape(nb, N_KV_HEADS, HEADS_PER_KV * t_new, HEAD_DIM)
    q_bd = jnp.einsum('bgrd,gk->bgrkd', q_rows, jnp.eye(N_KV_HEADS, dtype=BF16))
    q_bd = q_bd.reshape(nb, N_HEADS * t_new, half)
    slope_rows = jnp.asarray(np.repeat(_alibi_slopes().reshape(-1), t_new)[:, None])
    sel_rows = sel.reshape(nb, t_pad, N_KV_HEADS, 1, ns)[:, :t_new].transpose(0, 2, 3, 1, 4)
    sel_rows = jnp.broadcast_to(sel_rows, (nb, N_KV_HEADS, HEADS_PER_KV, t_new, ns))
    sel_rows = sel_rows.reshape(nb, N_HEADS * t_new, ns)

    def new_page(kv):
        kv = kv.reshape(nb, t_new, 2, N_KV_HEADS, HEAD_DIM).transpose(rows_last)
        return jnp.pad(kv, ((0, 0),) * 4 + ((0, PAGE - t_new),))

    o_s = _paged_attention(cache_slc.transpose(rows_last), lambda b, pg, pt: (pt[b * n_pages + pg], 0, 0, 0, 0),
                           n_pages, new_page(kv_slc), q_bd, slope_rows, sel_rows, min(n_pages, 16),
                           0, past_len, t_new, None, page_table, "slc_sample")
    n_win = state_win.shape[1]
    assert n_win % PAGE == 0
    win_pages = n_win // PAGE
    o_w = _paged_attention(state_win.transpose(rows_last), lambda b, pg, pt: (b, 0, 0, 0, pg),
                           win_pages, new_page(kv_win), q_bd, slope_rows, None, win_pages,
                           past_len - n_win, past_len, t_new, WINDOW, page_table, "win_sample")

    def to_slot(o):
        o = o.reshape(nb, N_KV_HEADS, HEADS_PER_KV, t_new, N_KV_HEADS, HEAD_DIM)
        o = jnp.stack([o[:, g, :, :, g, :] for g in range(N_KV_HEADS)], axis=1)
        o = o.transpose(0, 3, 1, 2, 4).reshape(m, N_HEADS * HEAD_DIM)
        return _slot_last(o).astype(BF16)

    x = _combine(x, gates, o_c, to_slot(o_s), to_slot(o_w), _slot_rows(p['nsa_w_out'][b_idx]).astype(BF16),
                 m, "nsa_out_sample")
    kv_shape = (nb, t_new, 2, N_KV_HEADS, HEAD_DIM)
    kv_win5 = kv_win.reshape(kv_shape)
    all_win = jnp.concatenate([state_win.reshape((nb, n_win) + kv_shape[2:]), kv_win5], axis=1)
    return x, kv_cmp.reshape(kv_shape), kv_slc.reshape(kv_shape), all_win[:, -min(WINDOW, n_win + t_new):]


def _run_prompt(x_prompt, p):
    nb, seq_len, d = x_prompt.shape
    x = x_prompt.reshape(nb * seq_len, d)
    tm = min(512, seq_len)
    tm_ffn = min(1024, seq_len)
    x = _sgu_layer(x, p, 0, seq_len, tm, False, "prompt")[0]
    x, h0 = _ffn_layer(x, p, 0, seq_len, None, False, tm_ffn, "prompt")
    x, kc, ks, kw = _nsa_prompt(x, p, 0, nb, seq_len, tm)
    x, h1 = _ffn_layer(x, p, 1, seq_len, None, True, tm_ffn, "prompt")
    tiles = seq_len // tm_ffn

    def conv_state(h):
        return h.reshape(nb, tiles, SUBLANES, -1)[:, -1, SUBLANES - (CONV_W - 1):]

    return (x.reshape(nb, seq_len, d), kc[None], ks[None], kw[None],
            jnp.stack([conv_state(h0), conv_state(h1)]))


def _run_sample(x_sample, cache_cmp, cache_slc, state_win, state_conv, page_table, p):
    nb, t_new, d = x_sample.shape
    m = nb * t_new
    past_len = page_table.shape[1] * PAGE
    x = x_sample.reshape(m, d)
    d_ff = state_conv.shape[-1]

    def conv_past(layer):
        st = state_conv[layer]
        zeros = jnp.zeros((nb, t_new, d_ff), F32)
        prev1 = zeros.at[:, 0].set(st[:, 1])
        prev2 = zeros.at[:, 0].set(st[:, 0]).at[:, 1].set(st[:, 1])
        return prev1.reshape(m, d_ff), prev2.reshape(m, d_ff)

    def conv_state(h):
        return h.reshape(nb, t_new, d_ff)[:, -(CONV_W - 1):]

    x, vn = _sgu_layer(x, p, 0, t_new, m, True, "sample")
    x, h0 = _ffn_layer(x, p, 0, t_new, conv_past(0), False, m, "sample")
    x, kc, ks, kw = _nsa_sample(x, p, 0, nb, t_new, past_len, cache_cmp[0], cache_slc[0], state_win[0],
                                page_table)
    x, h1 = _ffn_layer(x, p, 1, t_new, conv_past(1), True, m, "sample")
    return (x.reshape(nb, t_new, d), kc[None], ks[None], kw[None],
            jnp.stack([conv_state(h0), conv_state(h1)]), vn.reshape(nb, t_new, -1)[None])


def kernel(x_prompt, x_sample, cache_cmp_kv, cache_slc_kv, state_win_kv, state_ffn_conv, page_table,
           norm_mix, norm_ffn, norm_final, sgu_w_in, sgu_b_in, sgu_v_norm, sgu_w_s, sgu_b_s, sgu_w_out,
           nsa_w_in, nsa_cmp_pe, nsa_cmp_w, nsa_w_out, ffn_w_in, ffn_conv_w, ffn_conv_b, ffn_w_out):
    p = {'norm_mix': norm_mix, 'norm_ffn': norm_ffn, 'norm_final': norm_final,
         'sgu_w_in': sgu_w_in, 'sgu_b_in': sgu_b_in, 'sgu_v_norm': sgu_v_norm, 'sgu_w_s': sgu_w_s,
         'sgu_b_s': sgu_b_s, 'sgu_w_out': sgu_w_out, 'nsa_w_in': nsa_w_in, 'nsa_cmp_pe': nsa_cmp_pe,
         'nsa_cmp_w': nsa_cmp_w, 'nsa_w_out': nsa_w_out, 'ffn_w_in': ffn_w_in, 'ffn_conv_w': ffn_conv_w,
         'ffn_conv_b': ffn_conv_b, 'ffn_w_out': ffn_w_out}
    y_p, p_cmp, p_slc, p_win, p_conv = _run_prompt(x_prompt, p)
    y_s, s_cmp, s_slc, s_win, s_conv, s_v = _run_sample(
        x_sample, cache_cmp_kv, cache_slc_kv, state_win_kv, state_ffn_conv, page_table, p)
    return (y_p, y_s, p_cmp, p_slc, p_win, p_conv, s_cmp, s_slc, s_win, s_conv, s_v)
```

```python
import functools

import numpy as np
import jax
import jax.numpy as jnp
from jax import lax
from jax.experimental import pallas as pl
from jax.experimental.pallas import tpu as pltpu

F32 = jnp.float32
BF16 = jnp.bfloat16

EPS = 1e-6
NEG = -1e30
FORCE = 1e9
BELOW_NEG = -3e38
MASK_BIG = 2.0 ** 30
LOG2E = 1.4426950408889634

CHUNK = 128
N_SGU_GROUPS = 8
HEAD_DIM = 64
N_KV_HEADS = 4
HEADS_PER_KV = 4
N_HEADS = N_KV_HEADS * HEADS_PER_KV
N_BRANCH = 3
CMP_LEN = 32
CMP_STRIDE = 16
SLC_BLOCK = 64
SLC_SHIFT = SLC_BLOCK.bit_length() - 1
SLC_TOP_N = 16
WINDOW = 512
CONV_W = 3
PAGE = 128
KV_ROW = 2 * N_KV_HEADS * HEAD_DIM
SEG_W = CMP_STRIDE * KV_ROW

LANES = 128
SUBLANES = 8
BF16_SUBLANES = 16
SLOT = LANES
ALIBI_LANE = HEAD_DIM
SUM_ROW = HEAD_DIM
V7X_VMEM_BYTES = 64 * 1024 * 1024
VMEM_LIMIT = (V7X_VMEM_BYTES * 3) // 4


def _cparams(sem):
    return pltpu.CompilerParams(dimension_semantics=sem, vmem_limit_bytes=VMEM_LIMIT)


def _const_spec(shape):
    nd = len(shape)
    return pl.BlockSpec(shape, lambda *_: (0,) * nd)


def _rms(x, g):
    return x * lax.rsqrt(jnp.mean(x * x, axis=-1, keepdims=True) + EPS) * g


def _dot(a, b):
    return jnp.dot(a, b, preferred_element_type=F32)


def _dot_nt(a, b):
    return lax.dot_general(a, b, (((1,), (1,)), ((), ())), preferred_element_type=F32)


def _proj_body(x_ref, g_ref, w_ref, b_ref, *o_refs, segs, chunk, has_bias, t_outs):
    xn = _rms(x_ref[...], g_ref[...]).astype(BF16)
    if t_outs:
        wt_ref = o_refs[0]
        ot_refs = o_refs[len(o_refs) - len(t_outs):]
        o_refs = o_refs[1:len(o_refs) - len(t_outs)]
        tm = xn.shape[0]
        for (r0, n_rows, tile, seq_len), ot_ref in zip(t_outs, ot_refs):
            rt = _dot_nt(wt_ref[r0:r0 + n_rows, :], xn)
            if seq_len is not None:
                ot_ref[...] = rt
            else:
                for k in range(tm // tile):
                    ot_ref[k] = rt[:, k * tile:(k + 1) * tile].astype(ot_ref.dtype)
    off = 0
    for (width, _, act), o_ref in zip(segs, o_refs):
        for c0 in range(0, width, chunk):
            cw = min(chunk, width - c0)
            r = _dot(xn, w_ref[:, off + c0:off + c0 + cw])
            if has_bias:
                r = r + b_ref[:, off + c0:off + c0 + cw]
            if act:
                r = jax.nn.gelu(r)
            o_ref[:, c0:c0 + cw] = r.astype(o_ref.dtype)
        off += width


def _proj(x, gamma, w, bias, segs, tm, name, w_t=None, t_outs=()):
    m, d = x.shape
    n = w.shape[1]
    assert sum(s[0] for s in segs) == n and m % tm == 0
    has_bias = bias is not None
    if bias is None:
        bias = jnp.zeros((1, LANES), F32)
    in_specs = [pl.BlockSpec((tm, d), lambda i: (i, 0)),
                _const_spec((1, d)), _const_spec(w.shape), _const_spec(bias.shape)]
    out_specs = [pl.BlockSpec((tm, s[0]), lambda i: (i, 0)) for s in segs]
    out_shape = [jax.ShapeDtypeStruct((m, s[0]), s[1]) for s in segs]
    args = [x, gamma.reshape(1, d), w, bias]
    if t_outs:
        in_specs.append(_const_spec(w_t.shape))
        args.append(w_t)
        for _, n_rows, tile, seq_len in t_outs:
            assert tm % tile == 0
            if seq_len is not None:
                assert tile == tm and seq_len % tm == 0
                tps = seq_len // tm
                out_specs.append(pl.BlockSpec((None, n_rows, tm), lambda i, tps=tps: (i // tps, 0, i % tps)))
                out_shape.append(jax.ShapeDtypeStruct((m // seq_len, n_rows, seq_len), F32))
            else:
                out_specs.append(pl.BlockSpec((tm // tile, n_rows, tile), lambda i: (i, 0, 0)))
                out_shape.append(jax.ShapeDtypeStruct((m // tile, n_rows, tile), BF16))
    body = functools.partial(_proj_body, segs=tuple(segs), chunk=512, has_bias=has_bias,
                             t_outs=tuple(t_outs))
    return pl.pallas_call(
        body,
        grid=(m // tm,),
        in_specs=in_specs,
        out_specs=out_specs,
        out_shape=out_shape,
        compiler_params=_cparams(("parallel",)),
        name=name,
    )(*args)


def _sgu_body(x_ref, u_ref, v_ref, vg_ref, ws_ref, bs_ref, wo_ref, *rest, emit_v):
    if emit_v:
        o_ref, vn_ref, vn_scr, a_scr = rest
    else:
        o_ref, vn_scr, a_scr = rest
    tm = x_ref.shape[0]
    gw = v_ref.shape[1] // N_SGU_GROUPS
    vn = _rms(v_ref[...].astype(F32), vg_ref[...])
    if emit_v:
        vn_ref[...] = vn
    vn_scr[...] = vn.astype(BF16)
    row = lax.broadcasted_iota(jnp.int32, (CHUNK, CHUNK), 0)
    col = lax.broadcasted_iota(jnp.int32, (CHUNK, CHUNK), 1)
    for g in range(N_SGU_GROUPS):
        wc = jnp.where(col <= row, ws_ref[g], 0.0).astype(BF16)
        for c in range(tm // CHUNK):
            rs = slice(c * CHUNK, (c + 1) * CHUNK)
            cs = slice(g * gw, (g + 1) * gw)
            s = _dot(wc, vn_scr[rs, cs]) + bs_ref[:, cs]
            a_scr[rs, cs] = (u_ref[rs, cs].astype(F32) * s).astype(BF16)
    o_ref[...] = x_ref[...] + _dot(a_scr[...], wo_ref[...])


def _sgu(x, u, v, v_norm, w_s, bs_exp, w_out, tm, emit_v, name):
    m, d = x.shape
    ds = u.shape[1]
    out_shape = [jax.ShapeDtypeStruct((m, d), F32)]
    out_specs = [pl.BlockSpec((tm, d), lambda i: (i, 0))]
    if emit_v:
        out_shape.append(jax.ShapeDtypeStruct((m, ds), F32))
        out_specs.append(pl.BlockSpec((tm, ds), lambda i: (i, 0)))
    return pl.pallas_call(
        functools.partial(_sgu_body, emit_v=emit_v),
        grid=(m // tm,),
        in_specs=[pl.BlockSpec((tm, d), lambda i: (i, 0)),
                  pl.BlockSpec((tm, ds), lambda i: (i, 0)),
                  pl.BlockSpec((tm, ds), lambda i: (i, 0)),
                  _const_spec((1, ds)), _const_spec(w_s.shape), _const_spec(bs_exp.shape),
                  _const_spec(w_out.shape)],
        out_specs=out_specs,
        out_shape=out_shape,
        scratch_shapes=[pltpu.VMEM((tm, ds), BF16), pltpu.VMEM((tm, ds), BF16)],
        compiler_params=_cparams(("parallel",)),
        name=name,
    )(x, u, v, v_norm.reshape(1, ds), w_s, bs_exp, w_out)


def _ffn_body(*refs, sample, final_norm, period, tiles_per_seq, n_f):
    refs = list(refs)
    x_ref, g_ref, wh_ref, wg_ref, cw_ref, cb_ref, wo_ref = refs[:7]
    refs = refs[7:]
    if sample:
        p1_ref, p2_ref = refs[:2]
        refs = refs[2:]
    if final_norm:
        gf_ref = refs[0]
        refs = refs[1:]
    o_ref, h_ref, xn_scr, acc_scr = refs[:4]
    i = pl.program_id(0)
    f = pl.program_id(1)
    tm = x_ref.shape[0]

    @pl.when(f == 0)
    def _():
        xn_scr[...] = _rms(x_ref[...], g_ref[...]).astype(BF16)
        acc_scr[...] = jnp.zeros_like(acc_scr)

    xn = xn_scr[...]
    h = _dot(xn, wh_ref[...])
    gate = _dot(xn, wg_ref[...])
    r = lax.broadcasted_iota(jnp.int32, (tm, 1), 0)
    if sample:
        r = r & (period - 1)
        prev1 = p1_ref[...]
        prev2 = p2_ref[...]
        h_ref[...] = h
    else:
        carry_scr = refs[4]

        @pl.when(i % tiles_per_seq == 0)
        def _():
            carry_scr[f] = jnp.zeros(carry_scr.shape[1:], F32)

        prev = carry_scr[f]
        c0 = prev[SUBLANES - 2:SUBLANES - 1, :]
        c1 = prev[SUBLANES - 1:SUBLANES, :]
        prev1 = c1
        prev2 = jnp.where(r == 1, c1, c0)
        tail = h[tm - SUBLANES:tm, :]
        carry_scr[f] = tail
        h_ref[...] = tail
    hm1 = jnp.where(r >= 1, pltpu.roll(h, 1, 0), prev1)
    hm2 = jnp.where(r >= 2, pltpu.roll(h, 2, 0), prev2)
    hc = cb_ref[...] + cw_ref[0:1, :] * hm2
    hc = hc + cw_ref[1:2, :] * hm1
    hc = hc + cw_ref[2:3, :] * h
    act = (jax.nn.gelu(hc) * gate).astype(BF16)
    acc_scr[...] += _dot(act, wo_ref[...])

    @pl.when(f == n_f - 1)
    def _():
        y = x_ref[...] + acc_scr[...]
        if final_norm:
            y = _rms(y, gf_ref[...])
        o_ref[...] = y


def _ffn(x, gamma, w_in, conv_w, conv_b, w_out, seq_len, past, gamma_final, tm, tf, name):
    m, d = x.shape
    d_ff = w_out.shape[0]
    assert m % tm == 0 and d_ff % tf == 0
    n_f = d_ff // tf
    sample = past is not None
    final_norm = gamma_final is not None
    if sample:
        assert tm == m and (seq_len & (seq_len - 1)) == 0
    else:
        assert seq_len % tm == 0
    in_specs = [pl.BlockSpec((tm, d), lambda i, f: (i, 0)),
                _const_spec((1, d)),
                pl.BlockSpec((d, tf), lambda i, f: (0, f)),
                pl.BlockSpec((d, tf), lambda i, f: (0, f + n_f)),
                pl.BlockSpec((CONV_W, tf), lambda i, f: (0, f)),
                pl.BlockSpec((1, tf), lambda i, f: (0, f)),
                pl.BlockSpec((tf, d), lambda i, f: (f, 0))]
    args = [x, gamma.reshape(1, d), w_in, w_in, conv_w, conv_b.reshape(1, d_ff), w_out]
    if sample:
        in_specs += [pl.BlockSpec((tm, tf), lambda i, f: (i, f))] * 2
        args += list(past)
    if final_norm:
        in_specs.append(_const_spec((1, d)))
        args.append(gamma_final.reshape(1, d))
    scratch = [pltpu.VMEM((tm, d), BF16), pltpu.VMEM((tm, d), F32)]
    if sample:
        h_shape = jax.ShapeDtypeStruct((m, d_ff), F32)
        h_spec = pl.BlockSpec((tm, tf), lambda i, f: (i, f))
    else:
        h_shape = jax.ShapeDtypeStruct((m // tm, SUBLANES, d_ff), F32)
        h_spec = pl.BlockSpec((None, SUBLANES, tf), lambda i, f: (i, 0, f))
        scratch.append(pltpu.VMEM((n_f, SUBLANES, tf), F32))
    body = functools.partial(_ffn_body, sample=sample, final_norm=final_norm, period=seq_len,
                             tiles_per_seq=max(seq_len // tm, 1), n_f=n_f)
    return pl.pallas_call(
        body,
        grid=(m // tm, n_f),
        in_specs=in_specs,
        out_specs=[pl.BlockSpec((tm, d), lambda i, f: (i, 0)), h_spec],
        out_shape=[jax.ShapeDtypeStruct((m, d), F32), h_shape],
        scratch_shapes=scratch,
        compiler_params=_cparams(("arbitrary", "arbitrary")),
        name=name,
    )(*args)


def _compress_body(pt_ref, *refs, n_pages):
    n_main = n_pages * SUBLANES
    half = N_KV_HEADS * HEAD_DIM
    page_refs = refs[:n_pages + 1]
    pe_ref, w_ref, ck_ref, cv_ref, xs = refs[n_pages + 1:]
    out_row = lax.broadcasted_iota(jnp.int32, (PAGE, PAGE), 0)
    src_row = lax.broadcasted_iota(jnp.int32, (PAGE, PAGE), 1)
    wanted = (out_row & (SUBLANES - 1)) * CMP_STRIDE + (out_row >> (SUBLANES.bit_length() - 1))
    perm = jnp.where(src_row == wanted, 1.0, 0.0).astype(BF16)
    for k in range(n_pages + 1):
        for kv in range(2):
            page_t = page_refs[k][kv].reshape(half, PAGE).astype(BF16)
            xs[kv, k * PAGE:(k + 1) * PAGE, :] = _dot_nt(perm, page_t)

    def seg_rows(r, kv):
        return jnp.concatenate(
            [xs[kv, k * PAGE + r * SUBLANES:k * PAGE + (r + 1) * SUBLANES, :]
             for k in range(n_pages + 1)], axis=0)

    for kv, o_ref in ((0, ck_ref), (1, cv_ref)):
        acc = None
        for r in range(CMP_STRIDE):
            lhs = jnp.concatenate(
                [seg_rows(r, kv), pe_ref[:, r * KV_ROW + kv * half:r * KV_ROW + (kv + 1) * half]], axis=0)
            t = _dot(lhs.astype(BF16), w_ref[r, kv])
            acc = t if acc is None else acc + t
        n_tot = acc.shape[0]
        first = acc[:, :half] + acc[n_main + SUBLANES:n_main + SUBLANES + 1, :half]
        second = acc[:, half:] + acc[n_main + SUBLANES + 1:n_main + SUBLANES + 2, half:]
        second = pltpu.roll(second, n_tot - 1, 0)
        o_ref[...] = (first + second)[:n_main, :]


def _compress_paged(pages_t, page_index, nb, n_p, page_table, pe_seg, w_big, n_pages_step, name):
    assert n_p % n_pages_step == 0
    n_steps = n_p // n_pages_step
    half = N_KV_HEADS * HEAD_DIM
    n_main = n_pages_step * SUBLANES
    page_block = (None, 2, N_KV_HEADS, HEAD_DIM, PAGE)

    def page_map(k):
        def index(b, s, pt):
            return page_index(b, jnp.minimum(s * n_pages_step + k, n_p - 1), pt)
        return index

    in_specs = [pl.BlockSpec(page_block, page_map(k)) for k in range(n_pages_step + 1)]
    in_specs += [pl.BlockSpec(pe_seg.shape, lambda b, s, pt: (0, 0), pipeline_mode=pl.Buffered(1)),
                 pl.BlockSpec(w_big.shape, lambda b, s, pt: (0, 0, 0, 0), pipeline_mode=pl.Buffered(1))]
    out_spec = pl.BlockSpec((None, n_main, half), lambda b, s, pt: (b, s, 0))
    grid_spec = pltpu.PrefetchScalarGridSpec(
        num_scalar_prefetch=1, grid=(nb, n_steps), in_specs=in_specs,
        out_specs=[out_spec, out_spec],
        scratch_shapes=[pltpu.VMEM((2, (n_pages_step + 1) * PAGE, half), F32)])
    return pl.pallas_call(
        functools.partial(_compress_body, n_pages=n_pages_step),
        grid_spec=grid_spec,
        out_shape=[jax.ShapeDtypeStruct((nb, n_p * SUBLANES, half), F32)] * 2,
        compiler_params=_cparams(("parallel", "parallel")),
        name=name,
    )(page_table.reshape(-1), *([pages_t] * (n_pages_step + 1)), pe_seg, w_big)


def _cmp_select_body(q_ref, ck_ref, cv_ref, ov_ref, slope_ref, oc_ref, sel_ref, *, q_off, tokens, n_sel):
    rows = q_ref.shape[0]
    nc = ck_ref.shape[0]
    ns = ov_ref.shape[1]
    g_rows = HEADS_PER_KV * tokens
    pos = q_off + (lax.broadcasted_iota(jnp.int32, (rows, 1), 0) & (tokens - 1))
    cpos = lax.broadcasted_iota(jnp.int32, (1, nc), 1) * CMP_STRIDE + (CMP_LEN - 1)
    d_c = (pos - cpos).astype(F32)
    ok = d_c >= 0.0
    s = _dot_nt(q_ref[...], ck_ref[...].astype(BF16)) - slope_ref[...] * d_c
    s = jnp.where(ok, s, NEG)
    e = jnp.where(ok, jnp.exp2(s - jnp.max(s, axis=-1, keepdims=True)), 0.0)
    l = jnp.sum(e, axis=-1, keepdims=True)
    p = e * jnp.where(l > 0.0, 1.0 / l, 0.0)
    oc_ref[...] = _dot(p.astype(BF16), cv_ref[...].astype(BF16))
    psums = []
    for g in range(N_KV_HEADS):
        psum = p[g * g_rows:g * g_rows + tokens]
        for h in range(1, HEADS_PER_KV):
            psum = psum + p[g * g_rows + h * tokens:g * g_rows + (h + 1) * tokens]
        psums.append(psum)
    psum = jnp.concatenate(psums, axis=0)
    hi = psum.astype(BF16)
    rem = psum - hi.astype(F32)
    mid = rem.astype(BF16)
    lo = (rem - mid.astype(F32)).astype(BF16)
    ov = ov_ref[...]
    imp = _dot(hi, ov) + _dot(mid, ov) + _dot(lo, ov)
    blk = lax.broadcasted_iota(jnp.int32, (1, ns), 1)
    cur = pos[:N_KV_HEADS * tokens] >> SLC_SHIFT
    forced = (blk == 0) | (blk == cur) | (blk == cur - 1)
    imp = jnp.where(forced, FORCE, imp)
    imp = jnp.where(blk > cur, NEG, imp)
    blk_f = blk.astype(F32)
    sel = jnp.zeros(imp.shape, F32)
    for _ in range(n_sel):
        top = jnp.max(imp, axis=-1, keepdims=True)
        first = jnp.min(jnp.where(imp == top, blk_f, float(ns)), axis=-1, keepdims=True)
        hit = blk_f == first
        sel = jnp.where(hit, 1.0, sel)
        imp = jnp.where(hit, BELOW_NEG, imp)
    sel_ref[...] = sel.astype(sel_ref.dtype)


def _cmp_select(q_bd, ck, cv, overlap, slope_rows, tokens, q_off, n_sel, name):
    nb, rows, half = q_bd.shape
    nc = ck.shape[1]
    ns = overlap.shape[1]
    assert tokens % BF16_SUBLANES == 0 and (tokens & (tokens - 1)) == 0
    return pl.pallas_call(
        functools.partial(_cmp_select_body, q_off=q_off, tokens=tokens, n_sel=n_sel),
        grid=(nb,),
        in_specs=[pl.BlockSpec((None, rows, half), lambda b: (b, 0, 0)),
                  pl.BlockSpec((None, nc, half), lambda b: (b, 0, 0)),
                  pl.BlockSpec((None, nc, half), lambda b: (b, 0, 0)),
                  _const_spec(overlap.shape), _const_spec(slope_rows.shape)],
        out_specs=[pl.BlockSpec((None, rows, half), lambda b: (b, 0, 0)),
                   pl.BlockSpec((None, N_KV_HEADS * tokens, ns), lambda b: (b, 0, 0))],
        out_shape=[jax.ShapeDtypeStruct((nb, rows, half), F32),
                   jax.ShapeDtypeStruct((nb, N_KV_HEADS * tokens, ns), BF16)],
        compiler_params=_cparams(("parallel",)),
        name=name,
    )(q_bd, ck, cv, overlap, slope_rows)


def _cmp_select_t_body(q_ref, ck_ref, cvt_ref, ovt_ref, sext_ref, pext_ref, oc_ref, sel_ref, *, q_off, n_sel):
    tq = q_ref.shape[0]
    rows = HEADS_PER_KV * tq
    nc = ck_ref.shape[0]
    ns = ovt_ref.shape[0]
    base = q_off + pl.program_id(1) * tq
    pos = base + (lax.broadcasted_iota(jnp.int32, (1, rows), 1) & (tq - 1))
    cpos = lax.broadcasted_iota(jnp.int32, (nc, 1), 0) * CMP_STRIDE + (CMP_LEN - 1)
    ok = cpos <= pos
    imps = []

    def scores(g):
        qg = jnp.concatenate(
            [q_ref[:, (g * HEADS_PER_KV + h) * SLOT:(g * HEADS_PER_KV + h + 1) * SLOT]
             for h in range(HEADS_PER_KV)], axis=0) + sext_ref[g]
        return _dot_nt(ck_ref[:, g * SLOT:(g + 1) * SLOT] + pext_ref[...], qg)

    all_scores = [scores(g) for g in range(N_KV_HEADS)]
    for g in range(N_KV_HEADS):
        cs = slice(g * SLOT, (g + 1) * SLOT)
        s = jnp.where(ok, all_scores[g], NEG)
        e = jnp.where(ok, jnp.exp2(s - jnp.max(s, axis=0, keepdims=True)), 0.0)
        l = jnp.sum(e, axis=0, keepdims=True)
        p = e * jnp.where(l > 0.0, 1.0 / l, 0.0)
        o = _dot(cvt_ref[cs, :], p.astype(BF16)).T
        for h in range(HEADS_PER_KV):
            c0 = (g * HEADS_PER_KV + h) * SLOT
            oc_ref[:, c0:c0 + SLOT] = o[h * tq:(h + 1) * tq].astype(oc_ref.dtype)
        psum = p[:, 0:tq]
        for h in range(1, HEADS_PER_KV):
            psum = psum + p[:, h * tq:(h + 1) * tq]
        hi = psum.astype(BF16)
        rem = psum - hi.astype(F32)
        mid = rem.astype(BF16)
        lo = (rem - mid.astype(F32)).astype(BF16)
        ovt = ovt_ref[...]
        imps.append(_dot(ovt, hi) + _dot(ovt, mid) + _dot(ovt, lo))
    imp = jnp.concatenate(imps, axis=1)
    blk = lax.broadcasted_iota(jnp.int32, (ns, 1), 0)
    cur = pos >> SLC_SHIFT
    forced = (blk == 0) | (blk == cur) | (blk == cur - 1)
    imp = jnp.where(forced, FORCE, imp)
    imp = jnp.where(blk > cur, NEG, imp)
    blk_f = blk.astype(F32)
    sel = jnp.zeros(imp.shape, F32)
    for _ in range(n_sel):
        top = jnp.max(imp, axis=0, keepdims=True)
        first = jnp.min(jnp.where(imp == top, blk_f, float(ns)), axis=0, keepdims=True)
        hit = blk_f == first
        sel = jnp.where(hit, 1.0, sel)
        imp = jnp.where(hit, BELOW_NEG, imp)
    for g in range(N_KV_HEADS):
        sel_ref[:, g * ns:(g + 1) * ns] = sel[:, g * tq:(g + 1) * tq].T.astype(sel_ref.dtype)


def _cmp_select_t(q_slot, ck_slot, cvt_slot, overlap_t, tq, q_off, n_sel, name):
    m = q_slot.shape[0]
    nb, nc, _ = ck_slot.shape
    ns = overlap_t.shape[0]
    nq = m // (nb * tq)
    assert tq % LANES == 0 and (tq & (tq - 1)) == 0
    sext, pext = _alibi_lanes(np.arange(nc) * CMP_STRIDE + (CMP_LEN - 1), tq)
    return pl.pallas_call(
        functools.partial(_cmp_select_t_body, q_off=q_off, n_sel=n_sel),
        grid=(nb, nq),
        in_specs=[pl.BlockSpec((tq, N_HEADS * SLOT), lambda b, i: (b * nq + i, 0)),
                  pl.BlockSpec((None, nc, N_KV_HEADS * SLOT), lambda b, i: (b, 0, 0)),
                  pl.BlockSpec((None, N_KV_HEADS * SLOT, nc), lambda b, i: (b, 0, 0)),
                  _const_spec(overlap_t.shape), _const_spec(sext.shape), _const_spec(pext.shape)],
        out_specs=[pl.BlockSpec((tq, N_HEADS * SLOT), lambda b, i: (b * nq + i, 0)),
                   pl.BlockSpec((tq, N_KV_HEADS * ns), lambda b, i: (b * nq + i, 0))],
        out_shape=[jax.ShapeDtypeStruct((m, N_HEADS * SLOT), BF16),
                   jax.ShapeDtypeStruct((m, N_KV_HEADS * ns), BF16)],
        compiler_params=_cparams(("parallel", "parallel")),
        name=name,
    )(q_slot, ck_slot, cvt_slot, overlap_t, sext, pext)


def _slc_body(q_ref, sel_ref, k_ref, vt_ref, sext_ref, pext_ref, hot_ref, o_ref,
              qa_scr, m_scr, acc_scr, *, tk):
    i = pl.program_id(1)
    tq = q_ref.shape[0]
    rows = HEADS_PER_KV * tq
    ns = hot_ref.shape[1]
    for g in range(N_KV_HEADS):
        qg = jnp.concatenate(
            [q_ref[:, (g * HEADS_PER_KV + h) * SLOT:(g * HEADS_PER_KV + h + 1) * SLOT]
             for h in range(HEADS_PER_KV)], axis=0) + sext_ref[g]
        bias = ((sel_ref[:, g * ns:(g + 1) * ns].astype(F32) - 1.0) * MASK_BIG).astype(BF16)
        qa_scr[g] = jnp.concatenate([qg, jnp.concatenate([bias] * HEADS_PER_KV, axis=0)], axis=1)
    m_scr[...] = jnp.full_like(m_scr, NEG)
    acc_scr[...] = jnp.zeros_like(acc_scr)

    def step(j, diagonal):
        k0 = pl.multiple_of(j * tk, tk)
        kext = pext_ref[pl.ds(k0, tk), :]
        hot = hot_ref[pl.ds(k0, tk), :]
        if diagonal:
            pos = i * tq + (lax.broadcasted_iota(jnp.int32, (1, rows), 1) & (tq - 1))
            causal = (k0 + lax.broadcasted_iota(jnp.int32, (tk, 1), 0)) <= pos
        ones_row = lax.broadcasted_iota(jnp.int32, (SLOT, 1), 0) == SUM_ROW

        def scores(g):
            cs = slice(g * SLOT, (g + 1) * SLOT)
            ka = jnp.concatenate([k_ref[pl.ds(k0, tk), cs] + kext, hot], axis=1)
            s = _dot_nt(ka, qa_scr[g])
            return jnp.where(causal, s, NEG) if diagonal else s

        ahead = 3
        pending = [scores(g) for g in range(ahead)]
        for g in range(N_KV_HEADS):
            s = pending.pop(0)
            if g + ahead < N_KV_HEADS:
                pending.append(scores(g + ahead))
            cs = slice(g * SLOT, (g + 1) * SLOT)
            m_old = m_scr[g]
            m_new = jnp.maximum(m_old, jnp.max(s, axis=0, keepdims=True))
            p = jnp.exp2(s - m_new).astype(BF16)
            vt = jnp.where(ones_row, 1.0, vt_ref[j, cs, :]).astype(BF16)
            acc_scr[g] = jnp.exp2(m_old - m_new) * acc_scr[g] + _dot(vt, p)
            m_scr[g] = m_new

    n_full = (i * tq) // tk

    def two_steps(jj, carry):
        step(2 * jj, False)
        step(2 * jj + 1, False)
        return carry

    lax.fori_loop(0, n_full // 2, two_steps, 0)

    @pl.when(n_full % 2 == 1)
    def _():
        step(n_full - 1, False)

    step(n_full, True)
    for g in range(N_KV_HEADS):
        acc = acc_scr[g]
        o = (acc * (1.0 / acc[SUM_ROW:SUM_ROW + 1, :])).T
        for h in range(HEADS_PER_KV):
            c0 = (g * HEADS_PER_KV + h) * SLOT
            o_ref[:, c0:c0 + SLOT] = o[h * tq:(h + 1) * tq].astype(o_ref.dtype)


def _slc_attention(q_slot, sel, k_slot, vt_tiles, nb, seq_len, tq, tk, name):
    nq = seq_len // tq
    nk = seq_len // tk
    rows = HEADS_PER_KV * tq
    ns = sel.shape[1] // N_KV_HEADS
    assert seq_len % tk == 0 and tk % tq == 0 and seq_len <= ns * SLC_BLOCK
    assert vt_tiles.shape == (nb * nk, N_KV_HEADS * SLOT, tk)
    sext, pext, hot = _slc_constants(seq_len, tq, ns)
    once = pl.Buffered(1)
    return pl.pallas_call(
        functools.partial(_slc_body, tk=tk),
        grid=(nb, nq),
        in_specs=[pl.BlockSpec((tq, N_HEADS * SLOT), lambda b, i: (b * nq + i, 0)),
                  pl.BlockSpec((tq, N_KV_HEADS * ns), lambda b, i: (b * nq + i, 0)),
                  pl.BlockSpec((seq_len, N_KV_HEADS * SLOT), lambda b, i: (b, 0), pipeline_mode=once),
                  pl.BlockSpec((nk, N_KV_HEADS * SLOT, tk), lambda b, i: (b, 0, 0), pipeline_mode=once),
                  pl.BlockSpec(sext.shape, lambda b, i: (0, 0, 0), pipeline_mode=once),
                  pl.BlockSpec(pext.shape, lambda b, i: (0, 0), pipeline_mode=once),
                  pl.BlockSpec(hot.shape, lambda b, i: (0, 0), pipeline_mode=once)],
        out_specs=pl.BlockSpec((tq, N_HEADS * SLOT), lambda b, i: (b * nq + i, 0)),
        out_shape=jax.ShapeDtypeStruct(q_slot.shape, BF16),
        scratch_shapes=[pltpu.VMEM((N_KV_HEADS, rows, SLOT + ns), BF16),
                        pltpu.VMEM((N_KV_HEADS, 1, rows), F32),
                        pltpu.VMEM((N_KV_HEADS, SLOT, rows), F32)],
        compiler_params=_cparams(("parallel", "arbitrary")),
        name=name,
    )(q_slot, sel, k_slot, vt_tiles, sext, pext, hot)


def _win_body(q_ref, *refs, n_blk):
    k_refs = refs[:n_blk]
    vt_refs = refs[n_blk:2 * n_blk]
    sext_ref, pext_ref, band_ref, o_ref = refs[2 * n_blk:]
    tq = q_ref.shape[0]
    i = pl.program_id(1)
    n_keys = n_blk * tq
    key = lax.broadcasted_iota(jnp.int32, (n_keys, 1), 0)
    bias = band_ref[...] + jnp.where(key >= ((n_blk - 1) - i) * tq, 0.0, NEG)
    ones_row = lax.broadcasted_iota(jnp.int32, (SLOT, 1), 0) == SUM_ROW

    def scores(g):
        cs = slice(g * SLOT, (g + 1) * SLOT)
        qg = jnp.concatenate(
            [q_ref[:, (g * HEADS_PER_KV + h) * SLOT:(g * HEADS_PER_KV + h + 1) * SLOT]
             for h in range(HEADS_PER_KV)], axis=0) + sext_ref[g]
        kg = jnp.concatenate([k_refs[k][:, cs] for k in range(n_blk)], axis=0) + pext_ref[...]
        return _dot_nt(kg, qg) + bias

    all_scores = [scores(g) for g in range(N_KV_HEADS)]
    for g in range(N_KV_HEADS):
        cs = slice(g * SLOT, (g + 1) * SLOT)
        s = all_scores[g]
        p = jnp.exp2(s - jnp.max(s, axis=0, keepdims=True)).astype(BF16)
        vt = jnp.concatenate([vt_refs[k][cs, :] for k in range(n_blk)], axis=1)
        vt = jnp.where(ones_row, 1.0, vt).astype(BF16)
        acc = _dot(vt, p)
        o = (acc * (1.0 / acc[SUM_ROW:SUM_ROW + 1, :])).T
        for h in range(HEADS_PER_KV):
            c0 = (g * HEADS_PER_KV + h) * SLOT
            o_ref[:, c0:c0 + SLOT] = o[h * tq:(h + 1) * tq].astype(o_ref.dtype)


def _win_attention(q_slot, k_slot, vt_tiles, nb, seq_len, tq, name):
    nq = seq_len // tq
    n_blk = WINDOW // tq + 1
    n_keys = n_blk * tq
    sext, pext, _ = _slc_constants(n_keys, tq, LANES)
    dist = WINDOW + (np.arange(HEADS_PER_KV * tq) % tq)[None, :] - np.arange(n_keys)[:, None]
    band = jnp.asarray(np.where((dist >= 0) & (dist < WINDOW), 0.0, NEG).astype(np.float32))

    def block_map(k, transposed):
        def index(b, i):
            blk = b * nq + jnp.maximum(i - (n_blk - 1) + k, 0)
            return (blk, 0, 0) if transposed else (blk, 0)
        return index

    k_specs = [pl.BlockSpec((tq, N_KV_HEADS * SLOT), block_map(k, False)) for k in range(n_blk)]
    vt_specs = [pl.BlockSpec((None, N_KV_HEADS * SLOT, tq), block_map(k, True)) for k in range(n_blk)]
    return pl.pallas_call(
        functools.partial(_win_body, n_blk=n_blk),
        grid=(nb, nq),
        in_specs=[pl.BlockSpec((tq, N_HEADS * SLOT), lambda b, i: (b * nq + i, 0))]
        + k_specs + vt_specs + [_const_spec(sext.shape), _const_spec(pext.shape), _const_spec(band.shape)],
        out_specs=pl.BlockSpec((tq, N_HEADS * SLOT), lambda b, i: (b * nq + i, 0)),
        out_shape=jax.ShapeDtypeStruct(q_slot.shape, BF16),
        compiler_params=_cparams(("parallel", "parallel")),
        name=name,
    )(q_slot, *([k_slot] * n_blk), *([vt_tiles] * n_blk), sext, pext, band)


def _paged_attn_body(pt_ref, *refs, n_pages, n_steps, past_base, q_off, tokens, window, use_sel):
    page_refs = refs[:n_pages]
    refs = refs[n_pages:]
    new_ref, q_ref, slope_ref = refs[:3]
    refs = refs[3:]
    if use_sel:
        sel_ref = refs[0]
        refs = refs[1:]
    o_ref, m_scr, l_scr, acc_scr = refs
    s_id = pl.program_id(1)
    rows = q_ref.shape[0]
    half = N_KV_HEADS * HEAD_DIM

    @pl.when(s_id == 0)
    def _():
        m_scr[...] = jnp.full_like(m_scr, NEG)
        l_scr[...] = jnp.zeros_like(l_scr)
        acc_scr[...] = jnp.zeros_like(acc_scr)

    pos = q_off + (lax.broadcasted_iota(jnp.int32, (rows, 1), 0) & (tokens - 1))
    q = q_ref[...]
    key = lax.broadcasted_iota(jnp.int32, (1, PAGE), 1)

    def update(pages, kpos0s, blk0s):
        ss, keeps = [], []
        for ref, kpos0, blk0 in zip(pages, kpos0s, blk0s):
            d = pos - (kpos0 + key)
            valid = d >= 0
            if window is not None:
                valid = valid & (d < window)
            if use_sel:
                ns = sel_ref.shape[1]
                blk_row = lax.broadcasted_iota(jnp.int32, (ns, PAGE), 0)
                key_blk = blk0 + (lax.broadcasted_iota(jnp.int32, (ns, PAGE), 1) >> SLC_SHIFT)
                expand = jnp.where(blk_row == key_blk, 1.0, 0.0).astype(BF16)
                valid = valid & (_dot(sel_ref[...], expand) > 0.5)
            k_t = ref[0].reshape(half, PAGE).astype(BF16)
            s = _dot(q, k_t) - slope_ref[...] * d.astype(F32)
            ss.append(jnp.where(valid, s, NEG))
            keeps.append(jnp.where(valid, 1.0, 0.0))
        s = jnp.concatenate(ss, axis=1)
        keep = jnp.concatenate(keeps, axis=1)
        m_old = m_scr[...]
        m_new = jnp.maximum(m_old, jnp.max(s, axis=-1, keepdims=True))
        p = jnp.exp2(s - m_new) * keep
        alpha = jnp.exp2(m_old - m_new)
        l_scr[...] = alpha * l_scr[...] + jnp.sum(p, axis=-1, keepdims=True)
        p = p.astype(BF16)
        acc = alpha * acc_scr[...]
        for k, ref in enumerate(pages):
            v_t = ref[1].reshape(half, PAGE).astype(BF16)
            acc = acc + _dot_nt(p[:, k * PAGE:(k + 1) * PAGE], v_t)
        acc_scr[...] = acc
        m_scr[...] = m_new

    first_page = s_id * n_pages
    update(page_refs,
           [past_base + (first_page + k) * PAGE for k in range(n_pages)],
           [(first_page + k) * (PAGE // SLC_BLOCK) for k in range(n_pages)])

    @pl.when(s_id == n_steps - 1)
    def _():
        update([new_ref], [q_off], [(q_off - past_base) // SLC_BLOCK])
        o_ref[...] = acc_scr[...] * (1.0 / l_scr[...])


def _paged_attention(pages_t, page_index, n_p, new_t, q_bd, slope_rows, sel_rows, n_pages_step,
                     past_base, q_off, tokens, window, table, name):
    nb, rows, half = q_bd.shape
    assert n_p % n_pages_step == 0 and (tokens & (tokens - 1)) == 0
    n_steps = n_p // n_pages_step
    use_sel = sel_rows is not None
    page_block = (None,) * (pages_t.ndim - 4) + (2, N_KV_HEADS, HEAD_DIM, PAGE)

    def page_map(k):
        def index(b, s, pt):
            return page_index(b, s * n_pages_step + k, pt)
        return index

    in_specs = [pl.BlockSpec(page_block, page_map(k)) for k in range(n_pages_step)]
    in_specs += [pl.BlockSpec((None, 2, N_KV_HEADS, HEAD_DIM, PAGE), lambda b, s, pt: (b, 0, 0, 0, 0)),
                 pl.BlockSpec((None, rows, half), lambda b, s, pt: (b, 0, 0)),
                 pl.BlockSpec(slope_rows.shape, lambda b, s, pt: (0, 0))]
    args = [pages_t] * n_pages_step + [new_t, q_bd, slope_rows]
    if use_sel:
        in_specs.append(pl.BlockSpec((None,) + sel_rows.shape[1:], lambda b, s, pt: (b, 0, 0)))
        args.append(sel_rows)
    grid_spec = pltpu.PrefetchScalarGridSpec(
        num_scalar_prefetch=1, grid=(nb, n_steps), in_specs=in_specs,
        out_specs=pl.BlockSpec((None, rows, half), lambda b, s, pt: (b, 0, 0)),
        scratch_shapes=[pltpu.VMEM((rows, 1), F32), pltpu.VMEM((rows, 1), F32),
                        pltpu.VMEM((rows, half), F32)])
    body = functools.partial(_paged_attn_body, n_pages=n_pages_step, n_steps=n_steps,
                             past_base=past_base, q_off=q_off, tokens=tokens, window=window,
                             use_sel=use_sel)
    return pl.pallas_call(
        body,
        grid_spec=grid_spec,
        out_shape=jax.ShapeDtypeStruct((nb, rows, half), F32),
        compiler_params=_cparams(("parallel", "arbitrary")),
        name=name,
    )(table.reshape(-1), *args)


def _combine_body(x_ref, gates_ref, oc_ref, os_ref, ow_ref, wo_ref, o_ref, mix_scr):
    gt = jax.nn.sigmoid(gates_ref[...])
    for hh in range(N_HEADS):
        cs = slice(hh * SLOT, (hh + 1) * SLOT)
        mix = gt[:, hh:hh + 1] * oc_ref[:, cs].astype(F32)
        mix = mix + gt[:, N_HEADS + hh:N_HEADS + hh + 1] * os_ref[:, cs].astype(F32)
        mix = mix + gt[:, 2 * N_HEADS + hh:2 * N_HEADS + hh + 1] * ow_ref[:, cs].astype(F32)
        mix_scr[:, cs] = mix.astype(BF16)
    o_ref[...] = x_ref[...] + _dot(mix_scr[...], wo_ref[...])


def _combine(x, gates, o_c, o_s, o_w, w_out_slot, tm, name):
    m, d = x.shape
    wide = N_HEADS * SLOT
    row_spec = lambda w: pl.BlockSpec((tm, w), lambda i: (i, 0))
    return pl.pallas_call(
        _combine_body,
        grid=(m // tm,),
        in_specs=[row_spec(d), row_spec(gates.shape[1]), row_spec(wide), row_spec(wide), row_spec(wide),
                  _const_spec(w_out_slot.shape)],
        out_specs=row_spec(d),
        out_shape=jax.ShapeDtypeStruct((m, d), F32),
        scratch_shapes=[pltpu.VMEM((tm, wide), BF16)],
        compiler_params=_cparams(("parallel",)),
        name=name,
    )(x, gates, o_c, o_s, o_w, w_out_slot)


def _slot_cols(w):
    d, n = w.shape
    w = w.reshape(d, n // HEAD_DIM, HEAD_DIM)
    return jnp.pad(w, ((0, 0), (0, 0), (0, SLOT - HEAD_DIM))).reshape(d, (n // HEAD_DIM) * SLOT)


def _slot_rows(w):
    n, d = w.shape
    w = w.reshape(n // HEAD_DIM, HEAD_DIM, d)
    return jnp.pad(w, ((0, 0), (0, SLOT - HEAD_DIM), (0, 0))).reshape((n // HEAD_DIM) * SLOT, d)


def _slot_last(a):
    lead = a.shape[:-1]
    n = a.shape[-1] // HEAD_DIM
    a = a.reshape(lead + (n, HEAD_DIM))
    a = jnp.pad(a, [(0, 0)] * len(lead) + [(0, 0), (0, SLOT - HEAD_DIM)])
    return a.reshape(lead + (n * SLOT,))


def _alibi_slopes():
    h = np.arange(1, N_HEADS + 1, dtype=np.float32)
    slopes = np.asarray(2.0 ** (-8.0 * h / N_HEADS), dtype=np.float32)
    return (slopes * np.float32(LOG2E)).reshape(N_KV_HEADS, HEADS_PER_KV)


def _bf16_parts(x):
    x = np.asarray(x, np.float32)
    parts = []
    for _ in range(3):
        part = x.astype(jnp.bfloat16)
        parts.append(part)
        x = x - part.astype(np.float32)
    assert not np.any(x)
    return parts


def _alibi_lanes(kpos, tq):
    kpos = np.asarray(kpos)
    assert kpos.min() >= 0 and kpos.max() < 256 * LANES
    parts = _bf16_parts(np.repeat(_alibi_slopes(), tq, axis=1))
    sext = np.zeros((N_KV_HEADS, HEADS_PER_KV * tq, SLOT), jnp.bfloat16)
    pext = np.zeros((len(kpos), SLOT), np.float32)
    for n, part in enumerate(parts):
        sext[:, :, ALIBI_LANE + 2 * n] = part
        sext[:, :, ALIBI_LANE + 2 * n + 1] = part
        pext[:, ALIBI_LANE + 2 * n] = (kpos // LANES) * LANES
        pext[:, ALIBI_LANE + 2 * n + 1] = kpos % LANES
    return jnp.asarray(sext), jnp.asarray(pext, dtype=BF16)


def _slc_constants(seq_len, tq, ns):
    kpos = np.arange(seq_len)
    sext, pext = _alibi_lanes(kpos, tq)
    hot = (kpos[:, None] // SLC_BLOCK) == np.arange(ns)[None, :]
    return sext, pext, jnp.asarray(hot.astype(np.float32), dtype=BF16)


def _overlap(n_cmp, n_slc, nc, ns):
    st = np.arange(nc) * CMP_STRIDE
    bs = np.arange(ns) * SLC_BLOCK
    m = (st[:, None] < bs[None, :] + SLC_BLOCK) & (st[:, None] + CMP_LEN > bs[None, :])
    m = m & (np.arange(nc)[:, None] < n_cmp) & (np.arange(ns)[None, :] < n_slc)
    return jnp.asarray(m.astype(np.float32), dtype=BF16)


def _compress_weights(cmp_pe, cmp_w):
    eye = jnp.eye(N_KV_HEADS, dtype=F32)
    w = cmp_w.reshape(2, 2, CMP_STRIDE, HEAD_DIM, HEAD_DIM)
    bd = jnp.einsum('kmrde,gh->rkgdmhe', w, eye)
    half = N_KV_HEADS * HEAD_DIM
    w_big = bd.reshape(CMP_STRIDE, 2, half, 2 * half).astype(BF16)
    pe = cmp_pe.reshape(2, 2, CMP_STRIDE, HEAD_DIM)
    pe = jnp.broadcast_to(pe.transpose(1, 2, 0, 3)[:, :, :, None, :],
                          (2, CMP_STRIDE, 2, N_KV_HEADS, HEAD_DIM)).reshape(2, SEG_W)
    pe_seg = jnp.pad(pe, ((0, SUBLANES - 2), (0, 0)))
    return pe_seg, w_big


def _nsa_weights(w_in, segs_slot_kv):
    q_dim = N_HEADS * HEAD_DIM
    half = N_KV_HEADS * HEAD_DIM
    w_q = w_in[:, :q_dim] * (HEAD_DIM ** -0.5 * LOG2E)
    w_gate = w_in[:, q_dim + N_BRANCH * KV_ROW:]
    w_kv = w_in[:, q_dim:q_dim + N_BRANCH * KV_ROW]
    w_gate = jnp.pad(w_gate, ((0, 0), (0, LANES - w_gate.shape[1])))
    if not segs_slot_kv:
        segs = [(N_HEADS * SLOT, BF16, False), (KV_ROW, F32, False), (KV_ROW, F32, False),
                (KV_ROW, F32, False), (LANES, F32, False)]
        return jnp.concatenate([_slot_cols(w_q), w_kv, w_gate], axis=1).astype(BF16), segs, None
    o_slc = q_dim + KV_ROW
    o_win = q_dim + 2 * KV_ROW
    parts = [_slot_cols(w_q), w_gate,
             _slot_cols(w_in[:, o_slc:o_slc + half]), _slot_cols(w_in[:, o_win:o_win + half])]
    segs = [(N_HEADS * SLOT, BF16, False), (LANES, F32, False)] + [(N_KV_HEADS * SLOT, BF16, False)] * 2
    w_t = jnp.concatenate([_slot_cols(w_in[:, o_slc + half:o_slc + KV_ROW]),
                           _slot_cols(w_in[:, o_win + half:o_win + KV_ROW]), w_kv], axis=1).T.astype(BF16)
    return jnp.concatenate(parts, axis=1).astype(BF16), segs, w_t


def _sgu_layer(x, p, a, seq_len, tm, emit_v, tag):
    d = x.shape[1]
    d_sgu = p['sgu_w_out'].shape[1]
    gw = d_sgu // N_SGU_GROUPS
    segs = [(d_sgu, BF16, True), (d_sgu, BF16, True)]
    u, v = _proj(x, p['norm_mix'][0], p['sgu_w_in'][a].astype(BF16), p['sgu_b_in'][a].reshape(1, -1),
                 segs, tm, f"sgu_in_{tag}")
    w_s, b_s = p['sgu_w_s'][a], p['sgu_b_s'][a]
    if seq_len % CHUNK == 0:
        bs_rows = b_s.T
    else:
        assert CHUNK % seq_len == 0
        rep = CHUNK // seq_len
        w_s = jnp.einsum('ab,gts->gatbs', jnp.eye(rep, dtype=F32), w_s[:, :seq_len, :seq_len])
        w_s = w_s.reshape(N_SGU_GROUPS, CHUNK, CHUNK)
        bs_rows = jnp.tile(b_s[:, :seq_len].T, (rep, 1))
    bs_exp = jnp.repeat(bs_rows, gw, axis=1)
    return _sgu(x, u, v, p['sgu_v_norm'][a], w_s, bs_exp, p['sgu_w_out'][a].astype(BF16), tm, emit_v,
                f"sgu_{tag}")


def _ffn_layer(x, p, layer, seq_len, past, final, tm, tag):
    gamma_final = p['norm_final'] if final else None
    return _ffn(x, p['norm_ffn'][layer], p['ffn_w_in'][layer].astype(BF16), p['ffn_conv_w'][layer],
                p['ffn_conv_b'][layer], p['ffn_w_out'][layer].astype(BF16), seq_len, past, gamma_final,
                tm, 256, f"ffn{layer}_{tag}")


def _nsa_prompt(x, p, b_idx, nb, seq_len, tm):
    w_ext, segs, w_t = _nsa_weights(p['nsa_w_in'][b_idx], True)
    tq = 128
    wide = N_KV_HEADS * SLOT
    q_slot, gates, ks_slot, kw_slot, vs_tiles, vw_tiles, cmp_t, slc_t, win_t = _proj(
        x, p['norm_mix'][1], w_ext, None, segs, tm, "nsa_in_prompt", w_t,
        t_outs=((0, wide, tm, None), (wide, wide, tq, None)) + tuple(
            (2 * wide + br * KV_ROW, KV_ROW, tm, seq_len) for br in range(N_BRANCH)))
    rows_last = (nb, 2, N_KV_HEADS, HEAD_DIM, seq_len)
    cmp_t, slc_t, win_t = (a.reshape(rows_last) for a in (cmp_t, slc_t, win_t))
    n_pages = seq_len // PAGE
    pe_seg, w_big = _compress_weights(p['nsa_cmp_pe'][b_idx], p['nsa_cmp_w'][b_idx])
    ck, cv = _compress_paged(cmp_t, lambda b, pg, pt: (b, 0, 0, 0, pg), nb, n_pages,
                             jnp.zeros((1,), jnp.int32), pe_seg, w_big, min(n_pages, 16), "compress_prompt")
    n_seg = seq_len // CMP_STRIDE
    n_cmp = n_seg - CMP_LEN // CMP_STRIDE + 1
    n_slc = -(-seq_len // SLC_BLOCK)
    ns = -(-n_slc // LANES) * LANES
    o_c, sel = _cmp_select_t(q_slot, _slot_last(ck).astype(BF16),
                             _slot_last(cv).astype(BF16).transpose(0, 2, 1),
                             _overlap(n_cmp, n_slc, n_seg, ns).T, tq, 0,
                             min(SLC_TOP_N, n_slc), "cmp_select_prompt")
    o_s = _slc_attention(q_slot, sel, ks_slot, vs_tiles, nb, seq_len, tq, tm, "slc_prompt")
    o_w = _win_attention(q_slot, kw_slot, vw_tiles, nb, seq_len, tq, "win_prompt")
    x = _combine(x, gates, o_c, o_s, o_w, _slot_rows(p['nsa_w_out'][b_idx]).astype(BF16), tm,
                 "nsa_out_prompt")
    rows_first = (0, 4, 1, 2, 3)
    win_t = win_t[..., seq_len - min(WINDOW, seq_len):]
    return x, cmp_t.transpose(rows_first), slc_t.transpose(rows_first), win_t.transpose(rows_first)


def _rows_ght(a, tokens_pad):
    nb, t, _, w = a.shape
    a = a.reshape(nb, t, N_KV_HEADS, HEADS_PER_KV, w).transpose(0, 2, 3, 1, 4)
    a = jnp.pad(a, ((0, 0), (0, 0), (0, 0), (0, tokens_pad - t), (0, 0)))
    return a.reshape(nb, N_HEADS * tokens_pad, w)


def _nsa_sample(x, p, b_idx, nb, t_new, past_len, cache_cmp, cache_slc, state_win, page_table):
    m = x.shape[0]
    half = N_KV_HEADS * HEAD_DIM
    t_pad = BF16_SUBLANES
    assert t_new <= t_pad and past_len % PAGE == 0
    w_ext, segs, _ = _nsa_weights(p['nsa_w_in'][b_idx], False)
    q_slot, kv_cmp, kv_slc, kv_win, gates = _proj(x, p['norm_mix'][1], w_ext, None, segs, m, "nsa_in_sample")
    n_pages = past_len // PAGE
    total = past_len + t_new
    n_seg = total // CMP_STRIDE
    assert n_seg == past_len // CMP_STRIDE, "new rows must not complete a compression segment"
    n_cmp = n_seg - CMP_LEN // CMP_STRIDE + 1
    pe_seg, w_big = _compress_weights(p['nsa_cmp_pe'][b_idx], p['nsa_cmp_w'][b_idx])
    rows_last = (0, 2, 3, 4, 1)
    ck, cv = _compress_paged(cache_cmp.transpose(rows_last), lambda b, pg, pt: (pt[b * n_pages + pg], 0, 0, 0, 0),
                             nb, n_pages, page_table, pe_seg, w_big, min(n_pages, 32), "compress_sample")
    n_slc = -(-total // SLC_BLOCK)
    ns = -(-n_slc // LANES) * LANES
    assert (t_new & (t_new - 1)) == 0
    qh = q_slot.reshape(nb, t_new, N_HEADS, SLOT)[..., :HEAD_DIM]

    def block_diag_rows(tokens):
        q_rows = _rows_ght(qh, tokens).reshape(nb, N_KV_HEADS, HEADS_PER_KV * tokens, HEAD_DIM)
        q_bd = jnp.einsum('bgrd,gk->bgrkd', q_rows, jnp.eye(N_KV_HEADS, dtype=BF16))
        slopes = jnp.asarray(np.repeat(_alibi_slopes().reshape(-1), tokens)[:, None])
        return q_bd.reshape(nb, N_HEADS * tokens, half), slopes

    def own_lanes(o, tokens):
        o = o.reshape(nb, N_KV_HEADS, HEADS_PER_KV, tokens, N_KV_HEADS, HEAD_DIM)
        o = jnp.stack([o[:, g, :, :t_new, g, :] for g in range(N_KV_HEADS)], axis=1)
        o = o.transpose(0, 3, 1, 2, 4).reshape(m, N_HEADS * HEAD_DIM)
        return _slot_last(o).astype(BF16)

    q_bd_pad, slope_rows_pad = block_diag_rows(t_pad)
    o_c, sel = _cmp_select(q_bd_pad, ck, cv, _overlap(n_cmp, n_slc, n_seg, ns), slope_rows_pad, t_pad,
                           past_len, min(SLC_TOP_N, n_slc), "cmp_select_sample")
    o_c = own_lanes(o_c, t_pad)
    q_bd, slope_rows = block_diag_rows(t_new)
    sel_rows = sel.reshape(nb, N_KV_HEADS, 1, t_pad, ns)[:, :, :, :t_new]
    sel_rows = jnp.broadcast_to(sel_rows, (nb, N_KV_HEADS, HEADS_PER_KV, t_new, ns))
    sel_rows = sel_rows.reshape(nb, N_HEADS * t_new, ns)

    def new_page(kv):
        kv = kv.reshape(nb, t_new, 2, N_KV_HEADS, HEAD_DIM).transpose(rows_last)
        return jnp.pad(kv, ((0, 0),) * 4 + ((0, PAGE - t_new),))

    o_s = _paged_attention(cache_slc.transpose(rows_last), lambda b, pg, pt: (pt[b * n_pages + pg], 0, 0, 0, 0),
                           n_pages, new_page(kv_slc), q_bd, slope_rows, sel_rows, min(n_pages, 16),
                           0, past_len, t_new, None, page_table, "slc_sample")
    n_win = state_win.shape[1]
    assert n_win % PAGE == 0
    win_pages = n_win // PAGE
    o_w = _paged_attention(state_win.transpose(rows_last), lambda b, pg, pt: (b, 0, 0, 0, pg),
                           win_pages, new_page(kv_win), q_bd, slope_rows, None, win_pages,
                           past_len - n_win, past_len, t_new, WINDOW, page_table, "win_sample")

    x = _combine(x, gates, o_c, own_lanes(o_s, t_new), own_lanes(o_w, t_new),
                 _slot_rows(p['nsa_w_out'][b_idx]).astype(BF16), m, "nsa_out_sample")
    kv_shape = (nb, t_new, 2, N_KV_HEADS, HEAD_DIM)
    kv_win5 = kv_win.reshape(kv_shape)
    all_win = jnp.concatenate([state_win.reshape((nb, n_win) + kv_shape[2:]), kv_win5], axis=1)
    return x, kv_cmp.reshape(kv_shape), kv_slc.reshape(kv_shape), all_win[:, -min(WINDOW, n_win + t_new):]


def _run_prompt(x_prompt, p):
    nb, seq_len, d = x_prompt.shape
    x = x_prompt.reshape(nb * seq_len, d)
    tm = min(512, seq_len)
    tm_ffn = min(1024, seq_len)
    x = _sgu_layer(x, p, 0, seq_len, tm, False, "prompt")[0]
    x, h0 = _ffn_layer(x, p, 0, seq_len, None, False, tm_ffn, "prompt")
    x, kc, ks, kw = _nsa_prompt(x, p, 0, nb, seq_len, tm)
    x, h1 = _ffn_layer(x, p, 1, seq_len, None, True, tm_ffn, "prompt")
    tiles = seq_len // tm_ffn

    def conv_state(h):
        return h.reshape(nb, tiles, SUBLANES, -1)[:, -1, SUBLANES - (CONV_W - 1):]

    return (x.reshape(nb, seq_len, d), kc[None], ks[None], kw[None],
            jnp.stack([conv_state(h0), conv_state(h1)]))


def _run_sample(x_sample, cache_cmp, cache_slc, state_win, state_conv, page_table, p):
    nb, t_new, d = x_sample.shape
    m = nb * t_new
    past_len = page_table.shape[1] * PAGE
    x = x_sample.reshape(m, d)
    d_ff = state_conv.shape[-1]

    def conv_past(layer):
        st = state_conv[layer]
        zeros = jnp.zeros((nb, t_new, d_ff), F32)
        prev1 = zeros.at[:, 0].set(st[:, 1])
        prev2 = zeros.at[:, 0].set(st[:, 0]).at[:, 1].set(st[:, 1])
        return prev1.reshape(m, d_ff), prev2.reshape(m, d_ff)

    def conv_state(h):
        return h.reshape(nb, t_new, d_ff)[:, -(CONV_W - 1):]

    x, vn = _sgu_layer(x, p, 0, t_new, m, True, "sample")
    x, h0 = _ffn_layer(x, p, 0, t_new, conv_past(0), False, m, "sample")
    x, kc, ks, kw = _nsa_sample(x, p, 0, nb, t_new, past_len, cache_cmp[0], cache_slc[0], state_win[0],
                                page_table)
    x, h1 = _ffn_layer(x, p, 1, t_new, conv_past(1), True, m, "sample")
    return (x.reshape(nb, t_new, d), kc[None], ks[None], kw[None],
            jnp.stack([conv_state(h0), conv_state(h1)]), vn.reshape(nb, t_new, -1)[None])


def kernel(x_prompt, x_sample, cache_cmp_kv, cache_slc_kv, state_win_kv, state_ffn_conv, page_table,
           norm_mix, norm_ffn, norm_final, sgu_w_in, sgu_b_in, sgu_v_norm, sgu_w_s, sgu_b_s, sgu_w_out,
           nsa_w_in, nsa_cmp_pe, nsa_cmp_w, nsa_w_out, ffn_w_in, ffn_conv_w, ffn_conv_b, ffn_w_out):
    p = {'norm_mix': norm_mix, 'norm_ffn': norm_ffn, 'norm_final': norm_final,
         'sgu_w_in': sgu_w_in, 'sgu_b_in': sgu_b_in, 'sgu_v_norm': sgu_v_norm, 'sgu_w_s': sgu_w_s,
         'sgu_b_s': sgu_b_s, 'sgu_w_out': sgu_w_out, 'nsa_w_in': nsa_w_in, 'nsa_cmp_pe': nsa_cmp_pe,
         'nsa_cmp_w': nsa_cmp_w, 'nsa_w_out': nsa_w_out, 'ffn_w_in': ffn_w_in, 'ffn_conv_w': ffn_conv_w,
         'ffn_conv_b': ffn_conv_b, 'ffn_w_out': ffn_w_out}
    y_p, p_cmp, p_slc, p_win, p_conv = _run_prompt(x_prompt, p)
    y_s, s_cmp, s_slc, s_win, s_conv, s_v = _run_sample(
        x_sample, cache_cmp_kv, cache_slc_kv, state_win_kv, state_ffn_conv, page_table, p)
    return (y_p, y_s, p_cmp, p_slc, p_win, p_conv, s_cmp, s_slc, s_win, s_conv, s_v)
```

```python
import functools

import numpy as np
import jax
import jax.numpy as jnp
from jax import lax
from jax.experimental import pallas as pl
from jax.experimental.pallas import tpu as pltpu

F32 = jnp.float32
BF16 = jnp.bfloat16

EPS = 1e-6
NEG = -1e30
FORCE = 1e9
BELOW_NEG = -3e38
MASK_BIG = 2.0 ** 30
LOG2E = 1.4426950408889634

CHUNK = 128
N_SGU_GROUPS = 8
HEAD_DIM = 64
N_KV_HEADS = 4
HEADS_PER_KV = 4
N_HEADS = N_KV_HEADS * HEADS_PER_KV
N_BRANCH = 3
CMP_LEN = 32
CMP_STRIDE = 16
SLC_BLOCK = 64
SLC_SHIFT = SLC_BLOCK.bit_length() - 1
SLC_TOP_N = 16
WINDOW = 512
CONV_W = 3
PAGE = 128
KV_ROW = 2 * N_KV_HEADS * HEAD_DIM
SEG_W = CMP_STRIDE * KV_ROW

LANES = 128
SUBLANES = 8
BF16_SUBLANES = 16
SLOT = LANES
ALIBI_LANE = HEAD_DIM
SUM_ROW = HEAD_DIM
PV_ROWS = HEAD_DIM + 16
V7X_VMEM_BYTES = 64 * 1024 * 1024
VMEM_LIMIT = (V7X_VMEM_BYTES * 3) // 4


def _cparams(sem):
    return pltpu.CompilerParams(dimension_semantics=sem, vmem_limit_bytes=VMEM_LIMIT)


def _const_spec(shape):
    nd = len(shape)
    return pl.BlockSpec(shape, lambda *_: (0,) * nd)


def _rms(x, g):
    return x * lax.rsqrt(jnp.mean(x * x, axis=-1, keepdims=True) + EPS) * g


def _dot(a, b):
    return jnp.dot(a, b, preferred_element_type=F32)


def _dot_nt(a, b):
    return lax.dot_general(a, b, (((1,), (1,)), ((), ())), preferred_element_type=F32)


def _proj_body(x_ref, g_ref, w_ref, b_ref, *o_refs, segs, chunk, has_bias, t_outs):
    xn = _rms(x_ref[...], g_ref[...]).astype(BF16)
    if t_outs:
        wt_ref = o_refs[0]
        ot_refs = o_refs[len(o_refs) - len(t_outs):]
        o_refs = o_refs[1:len(o_refs) - len(t_outs)]
        tm = xn.shape[0]
        for (r0, n_rows, tile, seq_len), ot_ref in zip(t_outs, ot_refs):
            rt = _dot_nt(wt_ref[r0:r0 + n_rows, :], xn)
            if seq_len is not None:
                ot_ref[...] = rt
            else:
                for k in range(tm // tile):
                    ot_ref[k] = rt[:, k * tile:(k + 1) * tile].astype(ot_ref.dtype)
    off = 0
    for (width, _, act), o_ref in zip(segs, o_refs):
        for c0 in range(0, width, chunk):
            cw = min(chunk, width - c0)
            r = _dot(xn, w_ref[:, off + c0:off + c0 + cw])
            if has_bias:
                r = r + b_ref[:, off + c0:off + c0 + cw]
            if act:
                r = jax.nn.gelu(r)
            o_ref[:, c0:c0 + cw] = r.astype(o_ref.dtype)
        off += width


def _proj(x, gamma, w, bias, segs, tm, name, w_t=None, t_outs=()):
    m, d = x.shape
    n = w.shape[1]
    assert sum(s[0] for s in segs) == n and m % tm == 0
    has_bias = bias is not None
    if bias is None:
        bias = jnp.zeros((1, LANES), F32)
    in_specs = [pl.BlockSpec((tm, d), lambda i: (i, 0)),
                _const_spec((1, d)), _const_spec(w.shape), _const_spec(bias.shape)]
    out_specs = [pl.BlockSpec((tm, s[0]), lambda i: (i, 0)) for s in segs]
    out_shape = [jax.ShapeDtypeStruct((m, s[0]), s[1]) for s in segs]
    args = [x, gamma.reshape(1, d), w, bias]
    if t_outs:
        in_specs.append(_const_spec(w_t.shape))
        args.append(w_t)
        for _, n_rows, tile, seq_len in t_outs:
            assert tm % tile == 0
            if seq_len is not None:
                assert tile == tm and seq_len % tm == 0
                tps = seq_len // tm
                out_specs.append(pl.BlockSpec((None, n_rows, tm), lambda i, tps=tps: (i // tps, 0, i % tps)))
                out_shape.append(jax.ShapeDtypeStruct((m // seq_len, n_rows, seq_len), F32))
            else:
                out_specs.append(pl.BlockSpec((tm // tile, n_rows, tile), lambda i: (i, 0, 0)))
                out_shape.append(jax.ShapeDtypeStruct((m // tile, n_rows, tile), BF16))
    body = functools.partial(_proj_body, segs=tuple(segs), chunk=512, has_bias=has_bias,
                             t_outs=tuple(t_outs))
    return pl.pallas_call(
        body,
        grid=(m // tm,),
        in_specs=in_specs,
        out_specs=out_specs,
        out_shape=out_shape,
        compiler_params=_cparams(("parallel",)),
        name=name,
    )(*args)


def _sgu_body(x_ref, u_ref, v_ref, vg_ref, ws_ref, bs_ref, wo_ref, *rest, emit_v):
    if emit_v:
        o_ref, vn_ref, vn_scr, a_scr = rest
    else:
        o_ref, vn_scr, a_scr = rest
    tm = x_ref.shape[0]
    gw = v_ref.shape[1] // N_SGU_GROUPS
    vn = _rms(v_ref[...].astype(F32), vg_ref[...])
    if emit_v:
        vn_ref[...] = vn
    vn_scr[...] = vn.astype(BF16)
    row = lax.broadcasted_iota(jnp.int32, (CHUNK, CHUNK), 0)
    col = lax.broadcasted_iota(jnp.int32, (CHUNK, CHUNK), 1)
    for g in range(N_SGU_GROUPS):
        wc = jnp.where(col <= row, ws_ref[g], 0.0).astype(BF16)
        for c in range(tm // CHUNK):
            rs = slice(c * CHUNK, (c + 1) * CHUNK)
            cs = slice(g * gw, (g + 1) * gw)
            s = _dot(wc, vn_scr[rs, cs]) + bs_ref[:, cs]
            a_scr[rs, cs] = (u_ref[rs, cs].astype(F32) * s).astype(BF16)
    o_ref[...] = x_ref[...] + _dot(a_scr[...], wo_ref[...])


def _sgu(x, u, v, v_norm, w_s, bs_exp, w_out, tm, emit_v, name):
    m, d = x.shape
    ds = u.shape[1]
    out_shape = [jax.ShapeDtypeStruct((m, d), F32)]
    out_specs = [pl.BlockSpec((tm, d), lambda i: (i, 0))]
    if emit_v:
        out_shape.append(jax.ShapeDtypeStruct((m, ds), F32))
        out_specs.append(pl.BlockSpec((tm, ds), lambda i: (i, 0)))
    return pl.pallas_call(
        functools.partial(_sgu_body, emit_v=emit_v),
        grid=(m // tm,),
        in_specs=[pl.BlockSpec((tm, d), lambda i: (i, 0)),
                  pl.BlockSpec((tm, ds), lambda i: (i, 0)),
                  pl.BlockSpec((tm, ds), lambda i: (i, 0)),
                  _const_spec((1, ds)), _const_spec(w_s.shape), _const_spec(bs_exp.shape),
                  _const_spec(w_out.shape)],
        out_specs=out_specs,
        out_shape=out_shape,
        scratch_shapes=[pltpu.VMEM((tm, ds), BF16), pltpu.VMEM((tm, ds), BF16)],
        compiler_params=_cparams(("parallel",)),
        name=name,
    )(x, u, v, v_norm.reshape(1, ds), w_s, bs_exp, w_out)


def _ffn_body(*refs, sample, final_norm, period, tiles_per_seq, n_f):
    refs = list(refs)
    x_ref, g_ref, wh_ref, wg_ref, cw_ref, cb_ref, wo_ref = refs[:7]
    refs = refs[7:]
    if sample:
        p1_ref, p2_ref = refs[:2]
        refs = refs[2:]
    if final_norm:
        gf_ref = refs[0]
        refs = refs[1:]
    o_ref, h_ref, xn_scr, acc_scr, act_scr = refs[:5]
    i = pl.program_id(0)
    f = pl.program_id(1)
    tm = x_ref.shape[0]

    def step(make, flush):
        if make:
            xn = xn_scr[...]
            h = _dot(xn, wh_ref[...])
            gate = _dot(xn, wg_ref[...])
        if flush:
            acc_scr[...] += _dot(act_scr[...], wo_ref[...])
        if not make:
            return
        r = lax.broadcasted_iota(jnp.int32, (tm, 1), 0)
        if sample:
            r = r & (period - 1)
            prev1 = p1_ref[...]
            prev2 = p2_ref[...]
            h_ref[...] = h
        else:
            carry_scr = refs[5]
            prev = carry_scr[f]
            c0 = prev[SUBLANES - 2:SUBLANES - 1, :]
            c1 = prev[SUBLANES - 1:SUBLANES, :]
            prev1 = c1
            prev2 = jnp.where(r == 1, c1, c0)
            tail = h[tm - SUBLANES:tm, :]
            carry_scr[f] = tail
            h_ref[...] = tail
        hm1 = jnp.where(r >= 1, pltpu.roll(h, 1, 0), prev1)
        hm2 = jnp.where(r >= 2, pltpu.roll(h, 2, 0), prev2)
        hc = cb_ref[...] + cw_ref[0:1, :] * hm2
        hc = hc + cw_ref[1:2, :] * hm1
        hc = hc + cw_ref[2:3, :] * h
        act_scr[...] = (jax.nn.gelu(hc) * gate).astype(BF16)

    if not sample:
        @pl.when((i % tiles_per_seq == 0) & (f < n_f))
        def _():
            refs[5][f] = jnp.zeros(refs[5].shape[1:], F32)

    @pl.when(f == 0)
    def _():
        xn_scr[...] = _rms(x_ref[...], g_ref[...]).astype(BF16)
        acc_scr[...] = jnp.zeros_like(acc_scr)
        step(True, False)

    @pl.when((f > 0) & (f < n_f))
    def _():
        step(True, True)

    @pl.when(f == n_f)
    def _():
        step(False, True)
        y = x_ref[...] + acc_scr[...]
        if final_norm:
            y = _rms(y, gf_ref[...])
        o_ref[...] = y


def _ffn(x, gamma, w_in, conv_w, conv_b, w_out, seq_len, past, gamma_final, tm, tf, name):
    m, d = x.shape
    d_ff = w_out.shape[0]
    assert m % tm == 0 and d_ff % tf == 0
    n_f = d_ff // tf
    sample = past is not None
    final_norm = gamma_final is not None
    if sample:
        assert tm == m and (seq_len & (seq_len - 1)) == 0
    else:
        assert seq_len % tm == 0
    def made(f):
        return jnp.minimum(f, n_f - 1)

    in_specs = [pl.BlockSpec((tm, d), lambda i, f: (i, 0)),
                _const_spec((1, d)),
                pl.BlockSpec((d, tf), lambda i, f: (0, made(f))),
                pl.BlockSpec((d, tf), lambda i, f: (0, made(f) + n_f)),
                pl.BlockSpec((CONV_W, tf), lambda i, f: (0, made(f))),
                pl.BlockSpec((1, tf), lambda i, f: (0, made(f))),
                pl.BlockSpec((tf, d), lambda i, f: (jnp.maximum(f - 1, 0), 0))]
    args = [x, gamma.reshape(1, d), w_in, w_in, conv_w, conv_b.reshape(1, d_ff), w_out]
    if sample:
        in_specs += [pl.BlockSpec((tm, tf), lambda i, f: (i, made(f)))] * 2
        args += list(past)
    if final_norm:
        in_specs.append(_const_spec((1, d)))
        args.append(gamma_final.reshape(1, d))
    scratch = [pltpu.VMEM((tm, d), BF16), pltpu.VMEM((tm, d), F32), pltpu.VMEM((tm, tf), BF16)]
    if sample:
        h_shape = jax.ShapeDtypeStruct((m, d_ff), F32)
        h_spec = pl.BlockSpec((tm, tf), lambda i, f: (i, made(f)))
    else:
        h_shape = jax.ShapeDtypeStruct((m // tm, SUBLANES, d_ff), F32)
        h_spec = pl.BlockSpec((None, SUBLANES, tf), lambda i, f: (i, 0, made(f)))
        scratch.append(pltpu.VMEM((n_f, SUBLANES, tf), F32))
    body = functools.partial(_ffn_body, sample=sample, final_norm=final_norm, period=seq_len,
                             tiles_per_seq=max(seq_len // tm, 1), n_f=n_f)
    return pl.pallas_call(
        body,
        grid=(m // tm, n_f + 1),
        in_specs=in_specs,
        out_specs=[pl.BlockSpec((tm, d), lambda i, f: (i, 0)), h_spec],
        out_shape=[jax.ShapeDtypeStruct((m, d), F32), h_shape],
        scratch_shapes=scratch,
        compiler_params=_cparams(("arbitrary", "arbitrary")),
        name=name,
    )(*args)


def _compress_body(pt_ref, *refs, n_pages):
    n_main = n_pages * SUBLANES
    half = N_KV_HEADS * HEAD_DIM
    page_refs = refs[:n_pages + 1]
    pe_ref, w_ref, ck_ref, cv_ref, xs = refs[n_pages + 1:]
    out_row = lax.broadcasted_iota(jnp.int32, (PAGE, PAGE), 0)
    src_row = lax.broadcasted_iota(jnp.int32, (PAGE, PAGE), 1)
    wanted = (out_row & (SUBLANES - 1)) * CMP_STRIDE + (out_row >> (SUBLANES.bit_length() - 1))
    perm = jnp.where(src_row == wanted, 1.0, 0.0).astype(BF16)
    for k in range(n_pages + 1):
        for kv in range(2):
            page_t = page_refs[k][kv].reshape(half, PAGE).astype(BF16)
            xs[kv, k * PAGE:(k + 1) * PAGE, :] = _dot_nt(perm, page_t)

    def seg_rows(r, kv):
        return jnp.concatenate(
            [xs[kv, k * PAGE + r * SUBLANES:k * PAGE + (r + 1) * SUBLANES, :]
             for k in range(n_pages + 1)], axis=0)

    for kv, o_ref in ((0, ck_ref), (1, cv_ref)):
        acc = None
        for r in range(CMP_STRIDE):
            lhs = jnp.concatenate(
                [seg_rows(r, kv), pe_ref[:, r * KV_ROW + kv * half:r * KV_ROW + (kv + 1) * half]], axis=0)
            t = _dot(lhs.astype(BF16), w_ref[r, kv])
            acc = t if acc is None else acc + t
        n_tot = acc.shape[0]
        first = acc[:, :half] + acc[n_main + SUBLANES:n_main + SUBLANES + 1, :half]
        second = acc[:, half:] + acc[n_main + SUBLANES + 1:n_main + SUBLANES + 2, half:]
        second = pltpu.roll(second, n_tot - 1, 0)
        o_ref[...] = (first + second)[:n_main, :]


def _compress_paged(pages_t, page_index, nb, n_p, page_table, pe_seg, w_big, n_pages_step, name):
    assert n_p % n_pages_step == 0
    n_steps = n_p // n_pages_step
    half = N_KV_HEADS * HEAD_DIM
    n_main = n_pages_step * SUBLANES
    page_block = (None, 2, N_KV_HEADS, HEAD_DIM, PAGE)

    def page_map(k):
        def index(b, s, pt):
            return page_index(b, jnp.minimum(s * n_pages_step + k, n_p - 1), pt)
        return index

    in_specs = [pl.BlockSpec(page_block, page_map(k)) for k in range(n_pages_step + 1)]
    in_specs += [pl.BlockSpec(pe_seg.shape, lambda b, s, pt: (0, 0), pipeline_mode=pl.Buffered(1)),
                 pl.BlockSpec(w_big.shape, lambda b, s, pt: (0, 0, 0, 0), pipeline_mode=pl.Buffered(1))]
    out_spec = pl.BlockSpec((None, n_main, half), lambda b, s, pt: (b, s, 0))
    grid_spec = pltpu.PrefetchScalarGridSpec(
        num_scalar_prefetch=1, grid=(nb, n_steps), in_specs=in_specs,
        out_specs=[out_spec, out_spec],
        scratch_shapes=[pltpu.VMEM((2, (n_pages_step + 1) * PAGE, half), F32)])
    return pl.pallas_call(
        functools.partial(_compress_body, n_pages=n_pages_step),
        grid_spec=grid_spec,
        out_shape=[jax.ShapeDtypeStruct((nb, n_p * SUBLANES, half), F32)] * 2,
        compiler_params=_cparams(("parallel", "parallel")),
        name=name,
    )(page_table.reshape(-1), *([pages_t] * (n_pages_step + 1)), pe_seg, w_big)


def _cmp_select_body(q_ref, ck_ref, cv_ref, ov_ref, slope_ref, oc_ref, sel_ref, *, q_off, tokens, n_sel):
    rows = q_ref.shape[0]
    nc = ck_ref.shape[0]
    ns = ov_ref.shape[1]
    g_rows = HEADS_PER_KV * tokens
    pos = q_off + (lax.broadcasted_iota(jnp.int32, (rows, 1), 0) & (tokens - 1))
    cpos = lax.broadcasted_iota(jnp.int32, (1, nc), 1) * CMP_STRIDE + (CMP_LEN - 1)
    d_c = (pos - cpos).astype(F32)
    ok = d_c >= 0.0
    s = _dot_nt(q_ref[...], ck_ref[...].astype(BF16)) - slope_ref[...] * d_c
    s = jnp.where(ok, s, NEG)
    e = jnp.where(ok, jnp.exp2(s - jnp.max(s, axis=-1, keepdims=True)), 0.0)
    l = jnp.sum(e, axis=-1, keepdims=True)
    p = e * jnp.where(l > 0.0, 1.0 / l, 0.0)
    oc_ref[...] = _dot(p.astype(BF16), cv_ref[...].astype(BF16))
    psums = []
    for g in range(N_KV_HEADS):
        psum = p[g * g_rows:g * g_rows + tokens]
        for h in range(1, HEADS_PER_KV):
            psum = psum + p[g * g_rows + h * tokens:g * g_rows + (h + 1) * tokens]
        psums.append(psum)
    psum = jnp.concatenate(psums, axis=0)
    hi = psum.astype(BF16)
    rem = psum - hi.astype(F32)
    mid = rem.astype(BF16)
    lo = (rem - mid.astype(F32)).astype(BF16)
    ov = ov_ref[...]
    imp = _dot(hi, ov) + _dot(mid, ov) + _dot(lo, ov)
    blk = lax.broadcasted_iota(jnp.int32, (1, ns), 1)
    cur = pos[:N_KV_HEADS * tokens] >> SLC_SHIFT
    forced = (blk == 0) | (blk == cur) | (blk == cur - 1)
    imp = jnp.where(forced, FORCE, imp)
    imp = jnp.where(blk > cur, NEG, imp)
    blk_f = blk.astype(F32)
    sel = jnp.zeros(imp.shape, F32)
    for _ in range(n_sel):
        top = jnp.max(imp, axis=-1, keepdims=True)
        first = jnp.min(jnp.where(imp == top, blk_f, float(ns)), axis=-1, keepdims=True)
        hit = blk_f == first
        sel = jnp.where(hit, 1.0, sel)
        imp = jnp.where(hit, BELOW_NEG, imp)
    sel_ref[...] = sel.astype(sel_ref.dtype)


def _cmp_select(q_bd, ck, cv, overlap, slope_rows, tokens, q_off, n_sel, name):
    nb, rows, half = q_bd.shape
    nc = ck.shape[1]
    ns = overlap.shape[1]
    assert tokens % BF16_SUBLANES == 0 and (tokens & (tokens - 1)) == 0
    return pl.pallas_call(
        functools.partial(_cmp_select_body, q_off=q_off, tokens=tokens, n_sel=n_sel),
        grid=(nb,),
        in_specs=[pl.BlockSpec((None, rows, half), lambda b: (b, 0, 0)),
                  pl.BlockSpec((None, nc, half), lambda b: (b, 0, 0)),
                  pl.BlockSpec((None, nc, half), lambda b: (b, 0, 0)),
                  _const_spec(overlap.shape), _const_spec(slope_rows.shape)],
        out_specs=[pl.BlockSpec((None, rows, half), lambda b: (b, 0, 0)),
                   pl.BlockSpec((None, N_KV_HEADS * tokens, ns), lambda b: (b, 0, 0))],
        out_shape=[jax.ShapeDtypeStruct((nb, rows, half), F32),
                   jax.ShapeDtypeStruct((nb, N_KV_HEADS * tokens, ns), BF16)],
        compiler_params=_cparams(("parallel",)),
        name=name,
    )(q_bd, ck, cv, overlap, slope_rows)


def _cmp_select_t_body(q_ref, ck_ref, cvt_ref, ovt_ref, sext_ref, pext_ref, oc_ref, sel_ref, *, q_off, n_sel):
    tq = q_ref.shape[0]
    rows = HEADS_PER_KV * tq
    nc = ck_ref.shape[0]
    ns = ovt_ref.shape[0]
    base = q_off + pl.program_id(1) * tq
    pos = base + (lax.broadcasted_iota(jnp.int32, (1, rows), 1) & (tq - 1))
    cpos = lax.broadcasted_iota(jnp.int32, (nc, 1), 0) * CMP_STRIDE + (CMP_LEN - 1)
    ok = cpos <= pos
    imps = []

    def scores(g):
        qg = jnp.concatenate(
            [q_ref[:, (g * HEADS_PER_KV + h) * SLOT:(g * HEADS_PER_KV + h + 1) * SLOT]
             for h in range(HEADS_PER_KV)], axis=0) + sext_ref[g]
        return _dot_nt(ck_ref[:, g * SLOT:(g + 1) * SLOT] + pext_ref[...], qg)

    all_scores = [scores(g) for g in range(N_KV_HEADS)]
    for g in range(N_KV_HEADS):
        cs = slice(g * SLOT, (g + 1) * SLOT)
        s = jnp.where(ok, all_scores[g], NEG)
        e = jnp.where(ok, jnp.exp2(s - jnp.max(s, axis=0, keepdims=True)), 0.0)
        l = jnp.sum(e, axis=0, keepdims=True)
        p = e * jnp.where(l > 0.0, 1.0 / l, 0.0)
        o = _dot(cvt_ref[cs, :], p.astype(BF16)).T
        for h in range(HEADS_PER_KV):
            c0 = (g * HEADS_PER_KV + h) * SLOT
            oc_ref[:, c0:c0 + SLOT] = o[h * tq:(h + 1) * tq].astype(oc_ref.dtype)
        psum = p[:, 0:tq]
        for h in range(1, HEADS_PER_KV):
            psum = psum + p[:, h * tq:(h + 1) * tq]
        hi = psum.astype(BF16)
        rem = psum - hi.astype(F32)
        mid = rem.astype(BF16)
        lo = (rem - mid.astype(F32)).astype(BF16)
        ovt = ovt_ref[...]
        imps.append(_dot(ovt, hi) + _dot(ovt, mid) + _dot(ovt, lo))
    imp = jnp.concatenate(imps, axis=1)
    blk = lax.broadcasted_iota(jnp.int32, (ns, 1), 0)
    cur = pos >> SLC_SHIFT
    forced = (blk == 0) | (blk == cur) | (blk == cur - 1)
    imp = jnp.where(forced, FORCE, imp)
    imp = jnp.where(blk > cur, NEG, imp)
    blk_f = blk.astype(F32)
    sel = jnp.zeros(imp.shape, F32)
    for _ in range(n_sel):
        top = jnp.max(imp, axis=0, keepdims=True)
        first = jnp.min(jnp.where(imp == top, blk_f, float(ns)), axis=0, keepdims=True)
        hit = blk_f == first
        sel = jnp.where(hit, 1.0, sel)
        imp = jnp.where(hit, BELOW_NEG, imp)
    for g in range(N_KV_HEADS):
        sel_ref[:, g * ns:(g + 1) * ns] = sel[:, g * tq:(g + 1) * tq].T.astype(sel_ref.dtype)


def _cmp_select_t(q_slot, ck_slot, cvt_slot, overlap_t, tq, q_off, n_sel, name):
    m = q_slot.shape[0]
    nb, nc, _ = ck_slot.shape
    ns = overlap_t.shape[0]
    nq = m // (nb * tq)
    assert tq % LANES == 0 and (tq & (tq - 1)) == 0
    sext, pext = _alibi_lanes(np.arange(nc) * CMP_STRIDE + (CMP_LEN - 1), tq)
    return pl.pallas_call(
        functools.partial(_cmp_select_t_body, q_off=q_off, n_sel=n_sel),
        grid=(nb, nq),
        in_specs=[pl.BlockSpec((tq, N_HEADS * SLOT), lambda b, i: (b * nq + i, 0)),
                  pl.BlockSpec((None, nc, N_KV_HEADS * SLOT), lambda b, i: (b, 0, 0)),
                  pl.BlockSpec((None, N_KV_HEADS * SLOT, nc), lambda b, i: (b, 0, 0)),
                  _const_spec(overlap_t.shape), _const_spec(sext.shape), _const_spec(pext.shape)],
        out_specs=[pl.BlockSpec((tq, N_HEADS * SLOT), lambda b, i: (b * nq + i, 0)),
                   pl.BlockSpec((tq, N_KV_HEADS * ns), lambda b, i: (b * nq + i, 0))],
        out_shape=[jax.ShapeDtypeStruct((m, N_HEADS * SLOT), BF16),
                   jax.ShapeDtypeStruct((m, N_KV_HEADS * ns), BF16)],
        compiler_params=_cparams(("parallel", "parallel")),
        name=name,
    )(q_slot, ck_slot, cvt_slot, overlap_t, sext, pext)


def _slc_body(q_ref, sel_ref, k_ref, vt_ref, sext_ref, pext_ref, hot_ref, o_ref,
              qa_scr, m_scr, acc_scr, *, tk):
    i = pl.program_id(1)
    tq = q_ref.shape[0]
    rows = HEADS_PER_KV * tq
    ns = hot_ref.shape[1]
    for g in range(N_KV_HEADS):
        qg = jnp.concatenate(
            [q_ref[:, (g * HEADS_PER_KV + h) * SLOT:(g * HEADS_PER_KV + h + 1) * SLOT]
             for h in range(HEADS_PER_KV)], axis=0) + sext_ref[g]
        bias = ((sel_ref[:, g * ns:(g + 1) * ns].astype(F32) - 1.0) * MASK_BIG).astype(BF16)
        qa_scr[g] = jnp.concatenate([qg, jnp.concatenate([bias] * HEADS_PER_KV, axis=0)], axis=1)
    m_scr[...] = jnp.full_like(m_scr, NEG)
    acc_scr[...] = jnp.zeros_like(acc_scr)

    def step(j, diagonal):
        k0 = pl.multiple_of(j * tk, tk)
        kext = pext_ref[pl.ds(k0, tk), :]
        hot = hot_ref[pl.ds(k0, tk), :]
        if diagonal:
            pos = i * tq + (lax.broadcasted_iota(jnp.int32, (1, rows), 1) & (tq - 1))
            causal = (k0 + lax.broadcasted_iota(jnp.int32, (tk, 1), 0)) <= pos
        ones_row = lax.broadcasted_iota(jnp.int32, (PV_ROWS, 1), 0) == SUM_ROW

        def scores(g):
            cs = slice(g * SLOT, (g + 1) * SLOT)
            ka = jnp.concatenate([k_ref[pl.ds(k0, tk), cs] + kext, hot], axis=1)
            s = _dot_nt(ka, qa_scr[g])
            return jnp.where(causal, s, NEG) if diagonal else s

        ahead = 3
        pending = [scores(g) for g in range(ahead)]
        for g in range(N_KV_HEADS):
            s = pending.pop(0)
            if g + ahead < N_KV_HEADS:
                pending.append(scores(g + ahead))
            cs = slice(g * SLOT, (g + 1) * SLOT)
            m_old = m_scr[g]
            m_new = jnp.maximum(m_old, jnp.max(s, axis=0, keepdims=True))
            p = jnp.exp2(s - m_new).astype(BF16)
            vt = jnp.where(ones_row, 1.0, vt_ref[j, g * SLOT:g * SLOT + PV_ROWS, :]).astype(BF16)
            acc_scr[g] = jnp.exp2(m_old - m_new) * acc_scr[g] + _dot(vt, p)
            m_scr[g] = m_new

    n_full = (i * tq) // tk

    def two_steps(jj, carry):
        step(2 * jj, False)
        step(2 * jj + 1, False)
        return carry

    lax.fori_loop(0, n_full // 2, two_steps, 0)

    @pl.when(n_full % 2 == 1)
    def _():
        step(n_full - 1, False)

    step(n_full, True)
    for g in range(N_KV_HEADS):
        acc = acc_scr[g]
        o = acc * (1.0 / acc[SUM_ROW:SUM_ROW + 1, :])
        o = jnp.concatenate([o, jnp.zeros((SLOT - PV_ROWS, rows), F32)], axis=0).T
        for h in range(HEADS_PER_KV):
            c0 = (g * HEADS_PER_KV + h) * SLOT
            o_ref[:, c0:c0 + SLOT] = o[h * tq:(h + 1) * tq].astype(o_ref.dtype)


def _slc_attention(q_slot, sel, k_slot, vt_tiles, nb, seq_len, tq, tk, name):
    nq = seq_len // tq
    nk = seq_len // tk
    rows = HEADS_PER_KV * tq
    ns = sel.shape[1] // N_KV_HEADS
    assert seq_len % tk == 0 and tk % tq == 0 and seq_len <= ns * SLC_BLOCK
    assert vt_tiles.shape == (nb * nk, N_KV_HEADS * SLOT, tk)
    sext, pext, hot = _slc_constants(seq_len, tq, ns)
    once = pl.Buffered(1)
    return pl.pallas_call(
        functools.partial(_slc_body, tk=tk),
        grid=(nb, nq),
        in_specs=[pl.BlockSpec((tq, N_HEADS * SLOT), lambda b, i: (b * nq + i, 0)),
                  pl.BlockSpec((tq, N_KV_HEADS * ns), lambda b, i: (b * nq + i, 0)),
                  pl.BlockSpec((seq_len, N_KV_HEADS * SLOT), lambda b, i: (b, 0), pipeline_mode=once),
                  pl.BlockSpec((nk, N_KV_HEADS * SLOT, tk), lambda b, i: (b, 0, 0), pipeline_mode=once),
                  pl.BlockSpec(sext.shape, lambda b, i: (0, 0, 0), pipeline_mode=once),
                  pl.BlockSpec(pext.shape, lambda b, i: (0, 0), pipeline_mode=once),
                  pl.BlockSpec(hot.shape, lambda b, i: (0, 0), pipeline_mode=once)],
        out_specs=pl.BlockSpec((tq, N_HEADS * SLOT), lambda b, i: (b * nq + i, 0)),
        out_shape=jax.ShapeDtypeStruct(q_slot.shape, BF16),
        scratch_shapes=[pltpu.VMEM((N_KV_HEADS, rows, SLOT + ns), BF16),
                        pltpu.VMEM((N_KV_HEADS, 1, rows), F32),
                        pltpu.VMEM((N_KV_HEADS, PV_ROWS, rows), F32)],
        compiler_params=_cparams(("parallel", "arbitrary")),
        name=name,
    )(q_slot, sel, k_slot, vt_tiles, sext, pext, hot)


def _win_body(q_ref, *refs, n_blk):
    k_refs = refs[:n_blk]
    vt_refs = refs[n_blk:2 * n_blk]
    sext_ref, pext_ref, band_ref, o_ref = refs[2 * n_blk:]
    tq = q_ref.shape[0]
    i = pl.program_id(1)
    n_keys = n_blk * tq
    key = lax.broadcasted_iota(jnp.int32, (n_keys, 1), 0)
    bias = band_ref[...] + jnp.where(key >= ((n_blk - 1) - i) * tq, 0.0, NEG)
    ones_row = lax.broadcasted_iota(jnp.int32, (SLOT, 1), 0) == SUM_ROW

    def scores(g):
        cs = slice(g * SLOT, (g + 1) * SLOT)
        qg = jnp.concatenate(
            [q_ref[:, (g * HEADS_PER_KV + h) * SLOT:(g * HEADS_PER_KV + h + 1) * SLOT]
             for h in range(HEADS_PER_KV)], axis=0) + sext_ref[g]
        kg = jnp.concatenate([k_refs[k][:, cs] for k in range(n_blk)], axis=0) + pext_ref[...]
        return _dot_nt(kg, qg) + bias

    all_scores = [scores(g) for g in range(N_KV_HEADS)]
    for g in range(N_KV_HEADS):
        cs = slice(g * SLOT, (g + 1) * SLOT)
        s = all_scores[g]
        p = jnp.exp2(s - jnp.max(s, axis=0, keepdims=True)).astype(BF16)
        vt = jnp.concatenate([vt_refs[k][cs, :] for k in range(n_blk)], axis=1)
        vt = jnp.where(ones_row, 1.0, vt).astype(BF16)
        acc = _dot(vt, p)
        o = (acc * (1.0 / acc[SUM_ROW:SUM_ROW + 1, :])).T
        for h in range(HEADS_PER_KV):
            c0 = (g * HEADS_PER_KV + h) * SLOT
            o_ref[:, c0:c0 + SLOT] = o[h * tq:(h + 1) * tq].astype(o_ref.dtype)


def _win_attention(q_slot, k_slot, vt_tiles, nb, seq_len, tq, name):
    nq = seq_len // tq
    n_blk = WINDOW // tq + 1
    n_keys = n_blk * tq
    sext, pext, _ = _slc_constants(n_keys, tq, LANES)
    dist = WINDOW + (np.arange(HEADS_PER_KV * tq) % tq)[None, :] - np.arange(n_keys)[:, None]
    band = jnp.asarray(np.where((dist >= 0) & (dist < WINDOW), 0.0, NEG).astype(np.float32))

    def block_map(k, transposed):
        def index(b, i):
            blk = b * nq + jnp.maximum(i - (n_blk - 1) + k, 0)
            return (blk, 0, 0) if transposed else (blk, 0)
        return index

    k_specs = [pl.BlockSpec((tq, N_KV_HEADS * SLOT), block_map(k, False)) for k in range(n_blk)]
    vt_specs = [pl.BlockSpec((None, N_KV_HEADS * SLOT, tq), block_map(k, True)) for k in range(n_blk)]
    return pl.pallas_call(
        functools.partial(_win_body, n_blk=n_blk),
        grid=(nb, nq),
        in_specs=[pl.BlockSpec((tq, N_HEADS * SLOT), lambda b, i: (b * nq + i, 0))]
        + k_specs + vt_specs + [_const_spec(sext.shape), _const_spec(pext.shape), _const_spec(band.shape)],
        out_specs=pl.BlockSpec((tq, N_HEADS * SLOT), lambda b, i: (b * nq + i, 0)),
        out_shape=jax.ShapeDtypeStruct(q_slot.shape, BF16),
        compiler_params=_cparams(("parallel", "parallel")),
        name=name,
    )(q_slot, *([k_slot] * n_blk), *([vt_tiles] * n_blk), sext, pext, band)


def _paged_attn_body(pt_ref, *refs, n_pages, n_steps, past_base, q_off, tokens, window, use_sel):
    page_refs = refs[:n_pages]
    refs = refs[n_pages:]
    new_ref, q_ref, slope_ref = refs[:3]
    refs = refs[3:]
    if use_sel:
        sel_ref = refs[0]
        refs = refs[1:]
    o_ref, m_scr, l_scr, acc_scr = refs
    s_id = pl.program_id(1)
    rows = q_ref.shape[0]
    half = N_KV_HEADS * HEAD_DIM

    @pl.when(s_id == 0)
    def _():
        m_scr[...] = jnp.full_like(m_scr, NEG)
        l_scr[...] = jnp.zeros_like(l_scr)
        acc_scr[...] = jnp.zeros_like(acc_scr)

    pos = q_off + (lax.broadcasted_iota(jnp.int32, (rows, 1), 0) & (tokens - 1))
    q = q_ref[...]
    key = lax.broadcasted_iota(jnp.int32, (1, PAGE), 1)

    def update(pages, kpos0s, blk0s):
        ss, keeps = [], []
        for ref, kpos0, blk0 in zip(pages, kpos0s, blk0s):
            d = pos - (kpos0 + key)
            valid = d >= 0
            if window is not None:
                valid = valid & (d < window)
            if use_sel:
                ns = sel_ref.shape[1]
                blk_row = lax.broadcasted_iota(jnp.int32, (ns, PAGE), 0)
                key_blk = blk0 + (lax.broadcasted_iota(jnp.int32, (ns, PAGE), 1) >> SLC_SHIFT)
                expand = jnp.where(blk_row == key_blk, 1.0, 0.0).astype(BF16)
                valid = valid & (_dot(sel_ref[...], expand) > 0.5)
            k_t = ref[0].reshape(half, PAGE).astype(BF16)
            s = _dot(q, k_t) - slope_ref[...] * d.astype(F32)
            ss.append(jnp.where(valid, s, NEG))
            keeps.append(jnp.where(valid, 1.0, 0.0))
        s = jnp.concatenate(ss, axis=1)
        keep = jnp.concatenate(keeps, axis=1)
        m_old = m_scr[...]
        m_new = jnp.maximum(m_old, jnp.max(s, axis=-1, keepdims=True))
        p = jnp.exp2(s - m_new) * keep
        alpha = jnp.exp2(m_old - m_new)
        l_scr[...] = alpha * l_scr[...] + jnp.sum(p, axis=-1, keepdims=True)
        p = p.astype(BF16)
        acc = alpha * acc_scr[...]
        for k, ref in enumerate(pages):
            v_t = ref[1].reshape(half, PAGE).astype(BF16)
            acc = acc + _dot_nt(p[:, k * PAGE:(k + 1) * PAGE], v_t)
        acc_scr[...] = acc
        m_scr[...] = m_new

    first_page = s_id * n_pages
    update(page_refs,
           [past_base + (first_page + k) * PAGE for k in range(n_pages)],
           [(first_page + k) * (PAGE // SLC_BLOCK) for k in range(n_pages)])

    @pl.when(s_id == n_steps - 1)
    def _():
        update([new_ref], [q_off], [(q_off - past_base) // SLC_BLOCK])
        o_ref[...] = acc_scr[...] * (1.0 / l_scr[...])


def _paged_attention(pages_t, page_index, n_p, new_t, q_bd, slope_rows, sel_rows, n_pages_step,
                     past_base, q_off, tokens, window, table, name):
    nb, rows, half = q_bd.shape
    assert n_p % n_pages_step == 0 and (tokens & (tokens - 1)) == 0
    n_steps = n_p // n_pages_step
    use_sel = sel_rows is not None
    page_block = (None,) * (pages_t.ndim - 4) + (2, N_KV_HEADS, HEAD_DIM, PAGE)

    def page_map(k):
        def index(b, s, pt):
            return page_index(b, s * n_pages_step + k, pt)
        return index

    in_specs = [pl.BlockSpec(page_block, page_map(k)) for k in range(n_pages_step)]
    in_specs += [pl.BlockSpec((None, 2, N_KV_HEADS, HEAD_DIM, PAGE), lambda b, s, pt: (b, 0, 0, 0, 0)),
                 pl.BlockSpec((None, rows, half), lambda b, s, pt: (b, 0, 0)),
                 pl.BlockSpec(slope_rows.shape, lambda b, s, pt: (0, 0))]
    args = [pages_t] * n_pages_step + [new_t, q_bd, slope_rows]
    if use_sel:
        in_specs.append(pl.BlockSpec((None,) + sel_rows.shape[1:], lambda b, s, pt: (b, 0, 0)))
        args.append(sel_rows)
    grid_spec = pltpu.PrefetchScalarGridSpec(
        num_scalar_prefetch=1, grid=(nb, n_steps), in_specs=in_specs,
        out_specs=pl.BlockSpec((None, rows, half), lambda b, s, pt: (b, 0, 0)),
        scratch_shapes=[pltpu.VMEM((rows, 1), F32), pltpu.VMEM((rows, 1), F32),
                        pltpu.VMEM((rows, half), F32)])
    body = functools.partial(_paged_attn_body, n_pages=n_pages_step, n_steps=n_steps,
                             past_base=past_base, q_off=q_off, tokens=tokens, window=window,
                             use_sel=use_sel)
    return pl.pallas_call(
        body,
        grid_spec=grid_spec,
        out_shape=jax.ShapeDtypeStruct((nb, rows, half), F32),
        compiler_params=_cparams(("parallel", "arbitrary")),
        name=name,
    )(table.reshape(-1), *args)


def _combine_body(x_ref, gates_ref, oc_ref, os_ref, ow_ref, wo_ref, o_ref, mix_scr):
    gt = jax.nn.sigmoid(gates_ref[...])
    for hh in range(N_HEADS):
        cs = slice(hh * SLOT, (hh + 1) * SLOT)
        mix = gt[:, hh:hh + 1] * oc_ref[:, cs].astype(F32)
        mix = mix + gt[:, N_HEADS + hh:N_HEADS + hh + 1] * os_ref[:, cs].astype(F32)
        mix = mix + gt[:, 2 * N_HEADS + hh:2 * N_HEADS + hh + 1] * ow_ref[:, cs].astype(F32)
        mix_scr[:, cs] = mix.astype(BF16)
    o_ref[...] = x_ref[...] + _dot(mix_scr[...], wo_ref[...])


def _combine(x, gates, o_c, o_s, o_w, w_out_slot, tm, name):
    m, d = x.shape
    wide = N_HEADS * SLOT
    row_spec = lambda w: pl.BlockSpec((tm, w), lambda i: (i, 0))
    return pl.pallas_call(
        _combine_body,
        grid=(m // tm,),
        in_specs=[row_spec(d), row_spec(gates.shape[1]), row_spec(wide), row_spec(wide), row_spec(wide),
                  _const_spec(w_out_slot.shape)],
        out_specs=row_spec(d),
        out_shape=jax.ShapeDtypeStruct((m, d), F32),
        scratch_shapes=[pltpu.VMEM((tm, wide), BF16)],
        compiler_params=_cparams(("parallel",)),
        name=name,
    )(x, gates, o_c, o_s, o_w, w_out_slot)


def _slot_cols(w):
    d, n = w.shape
    w = w.reshape(d, n // HEAD_DIM, HEAD_DIM)
    return jnp.pad(w, ((0, 0), (0, 0), (0, SLOT - HEAD_DIM))).reshape(d, (n // HEAD_DIM) * SLOT)


def _slot_rows(w):
    n, d = w.shape
    w = w.reshape(n // HEAD_DIM, HEAD_DIM, d)
    return jnp.pad(w, ((0, 0), (0, SLOT - HEAD_DIM), (0, 0))).reshape((n // HEAD_DIM) * SLOT, d)


def _slot_last(a):
    lead = a.shape[:-1]
    n = a.shape[-1] // HEAD_DIM
    a = a.reshape(lead + (n, HEAD_DIM))
    a = jnp.pad(a, [(0, 0)] * len(lead) + [(0, 0), (0, SLOT - HEAD_DIM)])
    return a.reshape(lead + (n * SLOT,))


def _alibi_slopes():
    h = np.arange(1, N_HEADS + 1, dtype=np.float32)
    slopes = np.asarray(2.0 ** (-8.0 * h / N_HEADS), dtype=np.float32)
    return (slopes * np.float32(LOG2E)).reshape(N_KV_HEADS, HEADS_PER_KV)


def _bf16_parts(x):
    x = np.asarray(x, np.float32)
    parts = []
    for _ in range(3):
        part = x.astype(jnp.bfloat16)
        parts.append(part)
        x = x - part.astype(np.float32)
    assert not np.any(x)
    return parts


def _alibi_lanes(kpos, tq):
    kpos = np.asarray(kpos)
    assert kpos.min() >= 0 and kpos.max() < 256 * LANES
    parts = _bf16_parts(np.repeat(_alibi_slopes(), tq, axis=1))
    sext = np.zeros((N_KV_HEADS, HEADS_PER_KV * tq, SLOT), jnp.bfloat16)
    pext = np.zeros((len(kpos), SLOT), np.float32)
    for n, part in enumerate(parts):
        sext[:, :, ALIBI_LANE + 2 * n] = part
        sext[:, :, ALIBI_LANE + 2 * n + 1] = part
        pext[:, ALIBI_LANE + 2 * n] = (kpos // LANES) * LANES
        pext[:, ALIBI_LANE + 2 * n + 1] = kpos % LANES
    return jnp.asarray(sext), jnp.asarray(pext, dtype=BF16)


def _slc_constants(seq_len, tq, ns):
    kpos = np.arange(seq_len)
    sext, pext = _alibi_lanes(kpos, tq)
    hot = (kpos[:, None] // SLC_BLOCK) == np.arange(ns)[None, :]
    return sext, pext, jnp.asarray(hot.astype(np.float32), dtype=BF16)


def _overlap(n_cmp, n_slc, nc, ns):
    st = np.arange(nc) * CMP_STRIDE
    bs = np.arange(ns) * SLC_BLOCK
    m = (st[:, None] < bs[None, :] + SLC_BLOCK) & (st[:, None] + CMP_LEN > bs[None, :])
    m = m & (np.arange(nc)[:, None] < n_cmp) & (np.arange(ns)[None, :] < n_slc)
    return jnp.asarray(m.astype(np.float32), dtype=BF16)


def _compress_weights(cmp_pe, cmp_w):
    eye = jnp.eye(N_KV_HEADS, dtype=F32)
    w = cmp_w.reshape(2, 2, CMP_STRIDE, HEAD_DIM, HEAD_DIM)
    bd = jnp.einsum('kmrde,gh->rkgdmhe', w, eye)
    half = N_KV_HEADS * HEAD_DIM
    w_big = bd.reshape(CMP_STRIDE, 2, half, 2 * half).astype(BF16)
    pe = cmp_pe.reshape(2, 2, CMP_STRIDE, HEAD_DIM)
    pe = jnp.broadcast_to(pe.transpose(1, 2, 0, 3)[:, :, :, None, :],
                          (2, CMP_STRIDE, 2, N_KV_HEADS, HEAD_DIM)).reshape(2, SEG_W)
    pe_seg = jnp.pad(pe, ((0, SUBLANES - 2), (0, 0)))
    return pe_seg, w_big


def _nsa_weights(w_in, segs_slot_kv):
    q_dim = N_HEADS * HEAD_DIM
    half = N_KV_HEADS * HEAD_DIM
    w_q = w_in[:, :q_dim] * (HEAD_DIM ** -0.5 * LOG2E)
    w_gate = w_in[:, q_dim + N_BRANCH * KV_ROW:]
    w_kv = w_in[:, q_dim:q_dim + N_BRANCH * KV_ROW]
    w_gate = jnp.pad(w_gate, ((0, 0), (0, LANES - w_gate.shape[1])))
    if not segs_slot_kv:
        segs = [(N_HEADS * SLOT, BF16, False), (KV_ROW, F32, False), (KV_ROW, F32, False),
                (KV_ROW, F32, False), (LANES, F32, False)]
        return jnp.concatenate([_slot_cols(w_q), w_kv, w_gate], axis=1).astype(BF16), segs, None
    o_slc = q_dim + KV_ROW
    o_win = q_dim + 2 * KV_ROW
    parts = [_slot_cols(w_q), w_gate,
             _slot_cols(w_in[:, o_slc:o_slc + half]), _slot_cols(w_in[:, o_win:o_win + half])]
    segs = [(N_HEADS * SLOT, BF16, False), (LANES, F32, False)] + [(N_KV_HEADS * SLOT, BF16, False)] * 2
    w_t = jnp.concatenate([_slot_cols(w_in[:, o_slc + half:o_slc + KV_ROW]),
                           _slot_cols(w_in[:, o_win + half:o_win + KV_ROW]), w_kv], axis=1).T.astype(BF16)
    return jnp.concatenate(parts, axis=1).astype(BF16), segs, w_t


def _sgu_layer(x, p, a, seq_len, tm, emit_v, tag):
    d = x.shape[1]
    d_sgu = p['sgu_w_out'].shape[1]
    gw = d_sgu // N_SGU_GROUPS
    segs = [(d_sgu, BF16, True), (d_sgu, BF16, True)]
    u, v = _proj(x, p['norm_mix'][0], p['sgu_w_in'][a].astype(BF16), p['sgu_b_in'][a].reshape(1, -1),
                 segs, tm, f"sgu_in_{tag}")
    w_s, b_s = p['sgu_w_s'][a], p['sgu_b_s'][a]
    if seq_len % CHUNK == 0:
        bs_rows = b_s.T
    else:
        assert CHUNK % seq_len == 0
        rep = CHUNK // seq_len
        w_s = jnp.einsum('ab,gts->gatbs', jnp.eye(rep, dtype=F32), w_s[:, :seq_len, :seq_len])
        w_s = w_s.reshape(N_SGU_GROUPS, CHUNK, CHUNK)
        bs_rows = jnp.tile(b_s[:, :seq_len].T, (rep, 1))
    bs_exp = jnp.repeat(bs_rows, gw, axis=1)
    return _sgu(x, u, v, p['sgu_v_norm'][a], w_s, bs_exp, p['sgu_w_out'][a].astype(BF16), tm, emit_v,
                f"sgu_{tag}")


def _ffn_layer(x, p, layer, seq_len, past, final, tm, tag):
    gamma_final = p['norm_final'] if final else None
    return _ffn(x, p['norm_ffn'][layer], p['ffn_w_in'][layer].astype(BF16), p['ffn_conv_w'][layer],
                p['ffn_conv_b'][layer], p['ffn_w_out'][layer].astype(BF16), seq_len, past, gamma_final,
                tm, 256, f"ffn{layer}_{tag}")


def _nsa_prompt(x, p, b_idx, nb, seq_len, tm):
    w_ext, segs, w_t = _nsa_weights(p['nsa_w_in'][b_idx], True)
    tq = 128
    wide = N_KV_HEADS * SLOT
    q_slot, gates, ks_slot, kw_slot, vs_tiles, vw_tiles, cmp_t, slc_t, win_t = _proj(
        x, p['norm_mix'][1], w_ext, None, segs, tm, "nsa_in_prompt", w_t,
        t_outs=((0, wide, tm, None), (wide, wide, tq, None)) + tuple(
            (2 * wide + br * KV_ROW, KV_ROW, tm, seq_len) for br in range(N_BRANCH)))
    rows_last = (nb, 2, N_KV_HEADS, HEAD_DIM, seq_len)
    cmp_t, slc_t, win_t = (a.reshape(rows_last) for a in (cmp_t, slc_t, win_t))
    n_pages = seq_len // PAGE
    pe_seg, w_big = _compress_weights(p['nsa_cmp_pe'][b_idx], p['nsa_cmp_w'][b_idx])
    ck, cv = _compress_paged(cmp_t, lambda b, pg, pt: (b, 0, 0, 0, pg), nb, n_pages,
                             jnp.zeros((1,), jnp.int32), pe_seg, w_big, min(n_pages, 16), "compress_prompt")
    n_seg = seq_len // CMP_STRIDE
    n_cmp = n_seg - CMP_LEN // CMP_STRIDE + 1
    n_slc = -(-seq_len // SLC_BLOCK)
    ns = -(-n_slc // LANES) * LANES
    o_c, sel = _cmp_select_t(q_slot, _slot_last(ck).astype(BF16),
                             _slot_last(cv).astype(BF16).transpose(0, 2, 1),
                             _overlap(n_cmp, n_slc, n_seg, ns).T, tq, 0,
                             min(SLC_TOP_N, n_slc), "cmp_select_prompt")
    o_s = _slc_attention(q_slot, sel, ks_slot, vs_tiles, nb, seq_len, tq, tm, "slc_prompt")
    o_w = _win_attention(q_slot, kw_slot, vw_tiles, nb, seq_len, tq, "win_prompt")
    x = _combine(x, gates, o_c, o_s, o_w, _slot_rows(p['nsa_w_out'][b_idx]).astype(BF16), tm,
                 "nsa_out_prompt")
    rows_first = (0, 4, 1, 2, 3)
    win_t = win_t[..., seq_len - min(WINDOW, seq_len):]
    return x, cmp_t.transpose(rows_first), slc_t.transpose(rows_first), win_t.transpose(rows_first)


def _rows_ght(a, tokens_pad):
    nb, t, _, w = a.shape
    a = a.reshape(nb, t, N_KV_HEADS, HEADS_PER_KV, w).transpose(0, 2, 3, 1, 4)
    a = jnp.pad(a, ((0, 0), (0, 0), (0, 0), (0, tokens_pad - t), (0, 0)))
    return a.reshape(nb, N_HEADS * tokens_pad, w)


def _nsa_sample(x, p, b_idx, nb, t_new, past_len, cache_cmp, cache_slc, state_win, page_table):
    m = x.shape[0]
    half = N_KV_HEADS * HEAD_DIM
    t_pad = BF16_SUBLANES
    assert t_new <= t_pad and past_len % PAGE == 0
    w_ext, segs, _ = _nsa_weights(p['nsa_w_in'][b_idx], False)
    q_slot, kv_cmp, kv_slc, kv_win, gates = _proj(x, p['norm_mix'][1], w_ext, None, segs, m, "nsa_in_sample")
    n_pages = past_len // PAGE
    total = past_len + t_new
    n_seg = total // CMP_STRIDE
    assert n_seg == past_len // CMP_STRIDE, "new rows must not complete a compression segment"
    n_cmp = n_seg - CMP_LEN // CMP_STRIDE + 1
    pe_seg, w_big = _compress_weights(p['nsa_cmp_pe'][b_idx], p['nsa_cmp_w'][b_idx])
    rows_last = (0, 2, 3, 4, 1)
    ck, cv = _compress_paged(cache_cmp.transpose(rows_last), lambda b, pg, pt: (pt[b * n_pages + pg], 0, 0, 0, 0),
                             nb, n_pages, page_table, pe_seg, w_big, min(n_pages, 32), "compress_sample")
    n_slc = -(-total // SLC_BLOCK)
    ns = -(-n_slc // LANES) * LANES
    assert (t_new & (t_new - 1)) == 0
    qh = q_slot.reshape(nb, t_new, N_HEADS, SLOT)[..., :HEAD_DIM]

    def block_diag_rows(tokens):
        q_rows = _rows_ght(qh, tokens).reshape(nb, N_KV_HEADS, HEADS_PER_KV * tokens, HEAD_DIM)
        q_bd = jnp.einsum('bgrd,gk->bgrkd', q_rows, jnp.eye(N_KV_HEADS, dtype=BF16))
        slopes = jnp.asarray(np.repeat(_alibi_slopes().reshape(-1), tokens)[:, None])
        return q_bd.reshape(nb, N_HEADS * tokens, half), slopes

    def own_lanes(o, tokens):
        o = o.reshape(nb, N_KV_HEADS, HEADS_PER_KV, tokens, N_KV_HEADS, HEAD_DIM)
        o = jnp.stack([o[:, g, :, :t_new, g, :] for g in range(N_KV_HEADS)], axis=1)
        o = o.transpose(0, 3, 1, 2, 4).reshape(m, N_HEADS * HEAD_DIM)
        return _slot_last(o).astype(BF16)

    q_bd_pad, slope_rows_pad = block_diag_rows(t_pad)
    o_c, sel = _cmp_select(q_bd_pad, ck, cv, _overlap(n_cmp, n_slc, n_seg, ns), slope_rows_pad, t_pad,
                           past_len, min(SLC_TOP_N, n_slc), "cmp_select_sample")
    o_c = own_lanes(o_c, t_pad)
    q_bd, slope_rows = block_diag_rows(t_new)
    sel_rows = sel.reshape(nb, N_KV_HEADS, 1, t_pad, ns)[:, :, :, :t_new]
    sel_rows = jnp.broadcast_to(sel_rows, (nb, N_KV_HEADS, HEADS_PER_KV, t_new, ns))
    sel_rows = sel_rows.reshape(nb, N_HEADS * t_new, ns)

    def new_page(kv):
        kv = kv.reshape(nb, t_new, 2, N_KV_HEADS, HEAD_DIM).transpose(rows_last)
        return jnp.pad(kv, ((0, 0),) * 4 + ((0, PAGE - t_new),))

    o_s = _paged_attention(cache_slc.transpose(rows_last), lambda b, pg, pt: (pt[b * n_pages + pg], 0, 0, 0, 0),
                           n_pages, new_page(kv_slc), q_bd, slope_rows, sel_rows, min(n_pages, 16),
                           0, past_len, t_new, None, page_table, "slc_sample")
    n_win = state_win.shape[1]
    assert n_win % PAGE == 0
    win_pages = n_win // PAGE
    o_w = _paged_attention(state_win.transpose(rows_last), lambda b, pg, pt: (b, 0, 0, 0, pg),
                           win_pages, new_page(kv_win), q_bd, slope_rows, None, win_pages,
                           past_len - n_win, past_len, t_new, WINDOW, page_table, "win_sample")

    x = _combine(x, gates, o_c, own_lanes(o_s, t_new), own_lanes(o_w, t_new),
                 _slot_rows(p['nsa_w_out'][b_idx]).astype(BF16), m, "nsa_out_sample")
    kv_shape = (nb, t_new, 2, N_KV_HEADS, HEAD_DIM)
    kv_win5 = kv_win.reshape(kv_shape)
    all_win = jnp.concatenate([state_win.reshape((nb, n_win) + kv_shape[2:]), kv_win5], axis=1)
    return x, kv_cmp.reshape(kv_shape), kv_slc.reshape(kv_shape), all_win[:, -min(WINDOW, n_win + t_new):]


def _run_prompt(x_prompt, p):
    nb, seq_len, d = x_prompt.shape
    x = x_prompt.reshape(nb * seq_len, d)
    tm = min(512, seq_len)
    tm_ffn = min(1024, seq_len)
    x = _sgu_layer(x, p, 0, seq_len, tm, False, "prompt")[0]
    x, h0 = _ffn_layer(x, p, 0, seq_len, None, False, tm_ffn, "prompt")
    x, kc, ks, kw = _nsa_prompt(x, p, 0, nb, seq_len, tm)
    x, h1 = _ffn_layer(x, p, 1, seq_len, None, True, tm_ffn, "prompt")
    tiles = seq_len // tm_ffn

    def conv_state(h):
        return h.reshape(nb, tiles, SUBLANES, -1)[:, -1, SUBLANES - (CONV_W - 1):]

    return (x.reshape(nb, seq_len, d), kc[None], ks[None], kw[None],
            jnp.stack([conv_state(h0), conv_state(h1)]))


def _run_sample(x_sample, cache_cmp, cache_slc, state_win, state_conv, page_table, p):
    nb, t_new, d = x_sample.shape
    m = nb * t_new
    past_len = page_table.shape[1] * PAGE
    x = x_sample.reshape(m, d)
    d_ff = state_conv.shape[-1]

    def conv_past(layer):
        st = state_conv[layer]
        zeros = jnp.zeros((nb, t_new, d_ff), F32)
        prev1 = zeros.at[:, 0].set(st[:, 1])
        prev2 = zeros.at[:, 0].set(st[:, 0]).at[:, 1].set(st[:, 1])
        return prev1.reshape(m, d_ff), prev2.reshape(m, d_ff)

    def conv_state(h):
        return h.reshape(nb, t_new, d_ff)[:, -(CONV_W - 1):]

    x, vn = _sgu_layer(x, p, 0, t_new, m, True, "sample")
    x, h0 = _ffn_layer(x, p, 0, t_new, conv_past(0), False, m, "sample")
    x, kc, ks, kw = _nsa_sample(x, p, 0, nb, t_new, past_len, cache_cmp[0], cache_slc[0], state_win[0],
                                page_table)
    x, h1 = _ffn_layer(x, p, 1, t_new, conv_past(1), True, m, "sample")
    return (x.reshape(nb, t_new, d), kc[None], ks[None], kw[None],
            jnp.stack([conv_state(h0), conv_state(h1)]), vn.reshape(nb, t_new, -1)[None])


def kernel(x_prompt, x_sample, cache_cmp_kv, cache_slc_kv, state_win_kv, state_ffn_conv, page_table,
           norm_mix, norm_ffn, norm_final, sgu_w_in, sgu_b_in, sgu_v_norm, sgu_w_s, sgu_b_s, sgu_w_out,
           nsa_w_in, nsa_cmp_pe, nsa_cmp_w, nsa_w_out, ffn_w_in, ffn_conv_w, ffn_conv_b, ffn_w_out):
    p = {'norm_mix': norm_mix, 'norm_ffn': norm_ffn, 'norm_final': norm_final,
         'sgu_w_in': sgu_w_in, 'sgu_b_in': sgu_b_in, 'sgu_v_norm': sgu_v_norm, 'sgu_w_s': sgu_w_s,
         'sgu_b_s': sgu_b_s, 'sgu_w_out': sgu_w_out, 'nsa_w_in': nsa_w_in, 'nsa_cmp_pe': nsa_cmp_pe,
         'nsa_cmp_w': nsa_cmp_w, 'nsa_w_out': nsa_w_out, 'ffn_w_in': ffn_w_in, 'ffn_conv_w': ffn_conv_w,
         'ffn_conv_b': ffn_conv_b, 'ffn_w_out': ffn_w_out}
    y_p, p_cmp, p_slc, p_win, p_conv = _run_prompt(x_prompt, p)
    y_s, s_cmp, s_slc, s_win, s_conv, s_v = _run_sample(
        x_sample, cache_cmp_kv, cache_slc_kv, state_win_kv, state_ffn_conv, page_table, p)
    return (y_p, y_s, p_cmp, p_slc, p_win, p_conv, s_cmp, s_slc, s_win, s_conv, s_v)
```

```python
import functools

import numpy as np
import jax
import jax.numpy as jnp
from jax import lax
from jax.experimental import pallas as pl
from jax.experimental.pallas import tpu as pltpu

F32 = jnp.float32
BF16 = jnp.bfloat16

EPS = 1e-6
NEG = -1e30
FORCE = 1e9
BELOW_NEG = -3e38
MASK_BIG = 2.0 ** 30
LOG2E = 1.4426950408889634

CHUNK = 128
N_SGU_GROUPS = 8
HEAD_DIM = 64
N_KV_HEADS = 4
HEADS_PER_KV = 4
N_HEADS = N_KV_HEADS * HEADS_PER_KV
N_BRANCH = 3
CMP_LEN = 32
CMP_STRIDE = 16
SLC_BLOCK = 64
SLC_SHIFT = SLC_BLOCK.bit_length() - 1
SLC_TOP_N = 16
WINDOW = 512
CONV_W = 3
PAGE = 128
KV_ROW = 2 * N_KV_HEADS * HEAD_DIM
SEG_W = CMP_STRIDE * KV_ROW

LANES = 128
SUBLANES = 8
BF16_SUBLANES = 16
SLOT = LANES
ALIBI_LANE = HEAD_DIM
SUM_ROW = HEAD_DIM
PV_ROWS = HEAD_DIM + 16
V7X_VMEM_BYTES = 64 * 1024 * 1024
VMEM_LIMIT = (V7X_VMEM_BYTES * 3) // 4


def _cparams(sem):
    return pltpu.CompilerParams(dimension_semantics=sem, vmem_limit_bytes=VMEM_LIMIT)


def _const_spec(shape):
    nd = len(shape)
    return pl.BlockSpec(shape, lambda *_: (0,) * nd)


def _rms(x, g):
    return x * lax.rsqrt(jnp.mean(x * x, axis=-1, keepdims=True) + EPS) * g


def _dot(a, b):
    return jnp.dot(a, b, preferred_element_type=F32)


def _dot_nt(a, b):
    return lax.dot_general(a, b, (((1,), (1,)), ((), ())), preferred_element_type=F32)


def _proj_body(x_ref, g_ref, w_ref, b_ref, *o_refs, segs, chunk, has_bias, t_outs):
    xn = _rms(x_ref[...], g_ref[...]).astype(BF16)
    if t_outs:
        wt_ref = o_refs[0]
        ot_refs = o_refs[len(o_refs) - len(t_outs):]
        o_refs = o_refs[1:len(o_refs) - len(t_outs)]
        tm = xn.shape[0]
        for (r0, n_rows, tile, seq_len), ot_ref in zip(t_outs, ot_refs):
            rt = _dot_nt(wt_ref[r0:r0 + n_rows, :], xn)
            if seq_len is not None:
                ot_ref[...] = rt
            else:
                for k in range(tm // tile):
                    ot_ref[k] = rt[:, k * tile:(k + 1) * tile].astype(ot_ref.dtype)
    off = 0
    for (width, _, act), o_ref in zip(segs, o_refs):
        for c0 in range(0, width, chunk):
            cw = min(chunk, width - c0)
            r = _dot(xn, w_ref[:, off + c0:off + c0 + cw])
            if has_bias:
                r = r + b_ref[:, off + c0:off + c0 + cw]
            if act:
                r = jax.nn.gelu(r)
            o_ref[:, c0:c0 + cw] = r.astype(o_ref.dtype)
        off += width


def _proj(x, gamma, w, bias, segs, tm, name, w_t=None, t_outs=()):
    m, d = x.shape
    n = w.shape[1]
    assert sum(s[0] for s in segs) == n and m % tm == 0
    has_bias = bias is not None
    if bias is None:
        bias = jnp.zeros((1, LANES), F32)
    in_specs = [pl.BlockSpec((tm, d), lambda i: (i, 0)),
                _const_spec((1, d)), _const_spec(w.shape), _const_spec(bias.shape)]
    out_specs = [pl.BlockSpec((tm, s[0]), lambda i: (i, 0)) for s in segs]
    out_shape = [jax.ShapeDtypeStruct((m, s[0]), s[1]) for s in segs]
    args = [x, gamma.reshape(1, d), w, bias]
    if t_outs:
        in_specs.append(_const_spec(w_t.shape))
        args.append(w_t)
        for _, n_rows, tile, seq_len in t_outs:
            assert tm % tile == 0
            if seq_len is not None:
                assert tile == tm and seq_len % tm == 0
                tps = seq_len // tm
                out_specs.append(pl.BlockSpec((None, n_rows, tm), lambda i, tps=tps: (i // tps, 0, i % tps)))
                out_shape.append(jax.ShapeDtypeStruct((m // seq_len, n_rows, seq_len), F32))
            else:
                out_specs.append(pl.BlockSpec((tm // tile, n_rows, tile), lambda i: (i, 0, 0)))
                out_shape.append(jax.ShapeDtypeStruct((m // tile, n_rows, tile), BF16))
    body = functools.partial(_proj_body, segs=tuple(segs), chunk=512, has_bias=has_bias,
                             t_outs=tuple(t_outs))
    return pl.pallas_call(
        body,
        grid=(m // tm,),
        in_specs=in_specs,
        out_specs=out_specs,
        out_shape=out_shape,
        compiler_params=_cparams(("parallel",)),
        name=name,
    )(*args)


def _sgu_body(x_ref, u_ref, v_ref, vg_ref, ws_ref, bs_ref, wo_ref, *rest, emit_v):
    if emit_v:
        o_ref, vn_ref, vn_scr, a_scr = rest
    else:
        o_ref, vn_scr, a_scr = rest
    tm = x_ref.shape[0]
    gw = v_ref.shape[1] // N_SGU_GROUPS
    vn = _rms(v_ref[...].astype(F32), vg_ref[...])
    if emit_v:
        vn_ref[...] = vn
    vn_scr[...] = vn.astype(BF16)
    row = lax.broadcasted_iota(jnp.int32, (CHUNK, CHUNK), 0)
    col = lax.broadcasted_iota(jnp.int32, (CHUNK, CHUNK), 1)
    for g in range(N_SGU_GROUPS):
        wc = jnp.where(col <= row, ws_ref[g], 0.0).astype(BF16)
        for c in range(tm // CHUNK):
            rs = slice(c * CHUNK, (c + 1) * CHUNK)
            cs = slice(g * gw, (g + 1) * gw)
            s = _dot(wc, vn_scr[rs, cs]) + bs_ref[:, cs]
            a_scr[rs, cs] = (u_ref[rs, cs].astype(F32) * s).astype(BF16)
    o_ref[...] = x_ref[...] + _dot(a_scr[...], wo_ref[...])


def _sgu(x, u, v, v_norm, w_s, bs_exp, w_out, tm, emit_v, name):
    m, d = x.shape
    ds = u.shape[1]
    out_shape = [jax.ShapeDtypeStruct((m, d), F32)]
    out_specs = [pl.BlockSpec((tm, d), lambda i: (i, 0))]
    if emit_v:
        out_shape.append(jax.ShapeDtypeStruct((m, ds), F32))
        out_specs.append(pl.BlockSpec((tm, ds), lambda i: (i, 0)))
    return pl.pallas_call(
        functools.partial(_sgu_body, emit_v=emit_v),
        grid=(m // tm,),
        in_specs=[pl.BlockSpec((tm, d), lambda i: (i, 0)),
                  pl.BlockSpec((tm, ds), lambda i: (i, 0)),
                  pl.BlockSpec((tm, ds), lambda i: (i, 0)),
                  _const_spec((1, ds)), _const_spec(w_s.shape), _const_spec(bs_exp.shape),
                  _const_spec(w_out.shape)],
        out_specs=out_specs,
        out_shape=out_shape,
        scratch_shapes=[pltpu.VMEM((tm, ds), BF16), pltpu.VMEM((tm, ds), BF16)],
        compiler_params=_cparams(("parallel",)),
        name=name,
    )(x, u, v, v_norm.reshape(1, ds), w_s, bs_exp, w_out)


def _ffn_body(*refs, sample, final_norm, period, tiles_per_seq, n_f):
    refs = list(refs)
    x_ref, g_ref, wh_ref, wg_ref, cw_ref, cb_ref, wo_ref = refs[:7]
    refs = refs[7:]
    if sample:
        p1_ref, p2_ref = refs[:2]
        refs = refs[2:]
    if final_norm:
        gf_ref = refs[0]
        refs = refs[1:]
    o_ref, h_ref, xn_scr, acc_scr, act_scr = refs[:5]
    i = pl.program_id(0)
    f = pl.program_id(1)
    tm = x_ref.shape[0]

    def step(make, flush):
        if make:
            xn = xn_scr[...]
            h = _dot(xn, wh_ref[...])
            gate = _dot(xn, wg_ref[...])
        if flush:
            acc_scr[...] += _dot(act_scr[...], wo_ref[...])
        if not make:
            return
        r = lax.broadcasted_iota(jnp.int32, (tm, 1), 0)
        if sample:
            r = r & (period - 1)
            prev1 = p1_ref[...]
            prev2 = p2_ref[...]
            h_ref[...] = h
        else:
            carry_scr = refs[5]
            prev = carry_scr[f]
            c0 = prev[SUBLANES - 2:SUBLANES - 1, :]
            c1 = prev[SUBLANES - 1:SUBLANES, :]
            prev1 = c1
            prev2 = jnp.where(r == 1, c1, c0)
            tail = h[tm - SUBLANES:tm, :]
            carry_scr[f] = tail
            h_ref[...] = tail
        hm1 = jnp.where(r >= 1, pltpu.roll(h, 1, 0), prev1)
        hm2 = jnp.where(r >= 2, pltpu.roll(h, 2, 0), prev2)
        hc = cb_ref[...] + cw_ref[0:1, :] * hm2
        hc = hc + cw_ref[1:2, :] * hm1
        hc = hc + cw_ref[2:3, :] * h
        act_scr[...] = (jax.nn.gelu(hc) * gate).astype(BF16)

    if not sample:
        @pl.when((i % tiles_per_seq == 0) & (f < n_f))
        def _():
            refs[5][f] = jnp.zeros(refs[5].shape[1:], F32)

    @pl.when(f == 0)
    def _():
        xn_scr[...] = _rms(x_ref[...], g_ref[...]).astype(BF16)
        acc_scr[...] = jnp.zeros_like(acc_scr)
        step(True, False)

    @pl.when((f > 0) & (f < n_f))
    def _():
        step(True, True)

    @pl.when(f == n_f)
    def _():
        step(False, True)
        y = x_ref[...] + acc_scr[...]
        if final_norm:
            y = _rms(y, gf_ref[...])
        o_ref[...] = y


def _ffn(x, gamma, w_in, conv_w, conv_b, w_out, seq_len, past, gamma_final, tm, tf, name):
    m, d = x.shape
    d_ff = w_out.shape[0]
    assert m % tm == 0 and d_ff % tf == 0
    n_f = d_ff // tf
    sample = past is not None
    final_norm = gamma_final is not None
    if sample:
        assert tm == m and (seq_len & (seq_len - 1)) == 0
    else:
        assert seq_len % tm == 0
    def made(f):
        return jnp.minimum(f, n_f - 1)

    in_specs = [pl.BlockSpec((tm, d), lambda i, f: (i, 0)),
                _const_spec((1, d)),
                pl.BlockSpec((d, tf), lambda i, f: (0, made(f))),
                pl.BlockSpec((d, tf), lambda i, f: (0, made(f) + n_f)),
                pl.BlockSpec((CONV_W, tf), lambda i, f: (0, made(f))),
                pl.BlockSpec((1, tf), lambda i, f: (0, made(f))),
                pl.BlockSpec((tf, d), lambda i, f: (jnp.maximum(f - 1, 0), 0))]
    args = [x, gamma.reshape(1, d), w_in, w_in, conv_w, conv_b.reshape(1, d_ff), w_out]
    if sample:
        in_specs += [pl.BlockSpec((tm, tf), lambda i, f: (i, made(f)))] * 2
        args += list(past)
    if final_norm:
        in_specs.append(_const_spec((1, d)))
        args.append(gamma_final.reshape(1, d))
    scratch = [pltpu.VMEM((tm, d), BF16), pltpu.VMEM((tm, d), F32), pltpu.VMEM((tm, tf), BF16)]
    if sample:
        h_shape = jax.ShapeDtypeStruct((m, d_ff), F32)
        h_spec = pl.BlockSpec((tm, tf), lambda i, f: (i, made(f)))
    else:
        h_shape = jax.ShapeDtypeStruct((m // tm, SUBLANES, d_ff), F32)
        h_spec = pl.BlockSpec((None, SUBLANES, tf), lambda i, f: (i, 0, made(f)))
        scratch.append(pltpu.VMEM((n_f, SUBLANES, tf), F32))
    body = functools.partial(_ffn_body, sample=sample, final_norm=final_norm, period=seq_len,
                             tiles_per_seq=max(seq_len // tm, 1), n_f=n_f)
    return pl.pallas_call(
        body,
        grid=(m // tm, n_f + 1),
        in_specs=in_specs,
        out_specs=[pl.BlockSpec((tm, d), lambda i, f: (i, 0)), h_spec],
        out_shape=[jax.ShapeDtypeStruct((m, d), F32), h_shape],
        scratch_shapes=scratch,
        compiler_params=_cparams(("arbitrary", "arbitrary")),
        name=name,
    )(*args)


def _compress_body(pt_ref, *refs, n_pages):
    n_main = n_pages * SUBLANES
    half = N_KV_HEADS * HEAD_DIM
    page_refs = refs[:n_pages + 1]
    pe_ref, w_ref, ck_ref, cv_ref, xs = refs[n_pages + 1:]
    out_row = lax.broadcasted_iota(jnp.int32, (PAGE, PAGE), 0)
    src_row = lax.broadcasted_iota(jnp.int32, (PAGE, PAGE), 1)
    wanted = (out_row & (SUBLANES - 1)) * CMP_STRIDE + (out_row >> (SUBLANES.bit_length() - 1))
    perm = jnp.where(src_row == wanted, 1.0, 0.0).astype(BF16)
    for k in range(n_pages + 1):
        for kv in range(2):
            page_t = page_refs[k][kv].reshape(half, PAGE).astype(BF16)
            xs[kv, k * PAGE:(k + 1) * PAGE, :] = _dot_nt(perm, page_t)

    def seg_rows(r, kv):
        return jnp.concatenate(
            [xs[kv, k * PAGE + r * SUBLANES:k * PAGE + (r + 1) * SUBLANES, :]
             for k in range(n_pages + 1)], axis=0)

    for kv, o_ref in ((0, ck_ref), (1, cv_ref)):
        acc = None
        for r in range(CMP_STRIDE):
            lhs = jnp.concatenate(
                [seg_rows(r, kv), pe_ref[:, r * KV_ROW + kv * half:r * KV_ROW + (kv + 1) * half]], axis=0)
            t = _dot(lhs.astype(BF16), w_ref[r, kv])
            acc = t if acc is None else acc + t
        n_tot = acc.shape[0]
        first = acc[:, :half] + acc[n_main + SUBLANES:n_main + SUBLANES + 1, :half]
        second = acc[:, half:] + acc[n_main + SUBLANES + 1:n_main + SUBLANES + 2, half:]
        second = pltpu.roll(second, n_tot - 1, 0)
        o_ref[...] = (first + second)[:n_main, :]


def _compress_paged(pages_t, page_index, nb, n_p, page_table, pe_seg, w_big, n_pages_step, name):
    assert n_p % n_pages_step == 0
    n_steps = n_p // n_pages_step
    half = N_KV_HEADS * HEAD_DIM
    n_main = n_pages_step * SUBLANES
    page_block = (None, 2, N_KV_HEADS, HEAD_DIM, PAGE)

    def page_map(k):
        def index(b, s, pt):
            return page_index(b, jnp.minimum(s * n_pages_step + k, n_p - 1), pt)
        return index

    in_specs = [pl.BlockSpec(page_block, page_map(k)) for k in range(n_pages_step + 1)]
    in_specs += [pl.BlockSpec(pe_seg.shape, lambda b, s, pt: (0, 0), pipeline_mode=pl.Buffered(1)),
                 pl.BlockSpec(w_big.shape, lambda b, s, pt: (0, 0, 0, 0), pipeline_mode=pl.Buffered(1))]
    out_spec = pl.BlockSpec((None, n_main, half), lambda b, s, pt: (b, s, 0))
    grid_spec = pltpu.PrefetchScalarGridSpec(
        num_scalar_prefetch=1, grid=(nb, n_steps), in_specs=in_specs,
        out_specs=[out_spec, out_spec],
        scratch_shapes=[pltpu.VMEM((2, (n_pages_step + 1) * PAGE, half), F32)])
    return pl.pallas_call(
        functools.partial(_compress_body, n_pages=n_pages_step),
        grid_spec=grid_spec,
        out_shape=[jax.ShapeDtypeStruct((nb, n_p * SUBLANES, half), F32)] * 2,
        compiler_params=_cparams(("parallel", "parallel")),
        name=name,
    )(page_table.reshape(-1), *([pages_t] * (n_pages_step + 1)), pe_seg, w_big)


def _cmp_select_body(q_ref, ck_ref, cv_ref, ov_ref, slope_ref, oc_ref, sel_ref, *, q_off, tokens, n_sel):
    rows = q_ref.shape[0]
    nc = ck_ref.shape[0]
    ns = ov_ref.shape[1]
    g_rows = HEADS_PER_KV * tokens
    pos = q_off + (lax.broadcasted_iota(jnp.int32, (rows, 1), 0) & (tokens - 1))
    cpos = lax.broadcasted_iota(jnp.int32, (1, nc), 1) * CMP_STRIDE + (CMP_LEN - 1)
    d_c = (pos - cpos).astype(F32)
    ok = d_c >= 0.0
    s = _dot_nt(q_ref[...], ck_ref[...].astype(BF16)) - slope_ref[...] * d_c
    s = jnp.where(ok, s, NEG)
    e = jnp.where(ok, jnp.exp2(s - jnp.max(s, axis=-1, keepdims=True)), 0.0)
    l = jnp.sum(e, axis=-1, keepdims=True)
    p = e * jnp.where(l > 0.0, 1.0 / l, 0.0)
    oc_ref[...] = _dot(p.astype(BF16), cv_ref[...].astype(BF16))
    psums = []
    for g in range(N_KV_HEADS):
        psum = p[g * g_rows:g * g_rows + tokens]
        for h in range(1, HEADS_PER_KV):
            psum = psum + p[g * g_rows + h * tokens:g * g_rows + (h + 1) * tokens]
        psums.append(psum)
    psum = jnp.concatenate(psums, axis=0)
    hi = psum.astype(BF16)
    rem = psum - hi.astype(F32)
    mid = rem.astype(BF16)
    lo = (rem - mid.astype(F32)).astype(BF16)
    ov = ov_ref[...]
    imp = _dot(hi, ov) + _dot(mid, ov) + _dot(lo, ov)
    blk = lax.broadcasted_iota(jnp.int32, (1, ns), 1)
    cur = pos[:N_KV_HEADS * tokens] >> SLC_SHIFT
    forced = (blk == 0) | (blk == cur) | (blk == cur - 1)
    imp = jnp.where(forced, FORCE, imp)
    imp = jnp.where(blk > cur, NEG, imp)
    blk_f = blk.astype(F32)
    sel = jnp.zeros(imp.shape, F32)
    for _ in range(n_sel):
        top = jnp.max(imp, axis=-1, keepdims=True)
        first = jnp.min(jnp.where(imp == top, blk_f, float(ns)), axis=-1, keepdims=True)
        hit = blk_f == first
        sel = jnp.where(hit, 1.0, sel)
        imp = jnp.where(hit, BELOW_NEG, imp)
    sel_ref[...] = sel.astype(sel_ref.dtype)


def _cmp_select(q_bd, ck, cv, overlap, slope_rows, tokens, q_off, n_sel, name):
    nb, rows, half = q_bd.shape
    nc = ck.shape[1]
    ns = overlap.shape[1]
    assert tokens % BF16_SUBLANES == 0 and (tokens & (tokens - 1)) == 0
    return pl.pallas_call(
        functools.partial(_cmp_select_body, q_off=q_off, tokens=tokens, n_sel=n_sel),
        grid=(nb,),
        in_specs=[pl.BlockSpec((None, rows, half), lambda b: (b, 0, 0)),
                  pl.BlockSpec((None, nc, half), lambda b: (b, 0, 0)),
                  pl.BlockSpec((None, nc, half), lambda b: (b, 0, 0)),
                  _const_spec(overlap.shape), _const_spec(slope_rows.shape)],
        out_specs=[pl.BlockSpec((None, rows, half), lambda b: (b, 0, 0)),
                   pl.BlockSpec((None, N_KV_HEADS * tokens, ns), lambda b: (b, 0, 0))],
        out_shape=[jax.ShapeDtypeStruct((nb, rows, half), F32),
                   jax.ShapeDtypeStruct((nb, N_KV_HEADS * tokens, ns), BF16)],
        compiler_params=_cparams(("parallel",)),
        name=name,
    )(q_bd, ck, cv, overlap, slope_rows)


def _cmp_select_t_body(q_ref, ck_ref, cvt_ref, ovt_ref, sext_ref, pext_ref, oc_ref, sel_ref, imp_scr,
                       *, q_off, n_sel):
    tq = q_ref.shape[0]
    rows = HEADS_PER_KV * tq
    nc = ck_ref.shape[0]
    ns = ovt_ref.shape[0]
    base = q_off + pl.program_id(1) * tq
    pos = base + (lax.broadcasted_iota(jnp.int32, (1, rows), 1) & (tq - 1))

    def attend(n_blocks):
        cpos = lax.broadcasted_iota(jnp.int32, (n_blocks, 1), 0) * CMP_STRIDE + (CMP_LEN - 1)
        ok = cpos <= pos

        def scores(g):
            qg = jnp.concatenate(
                [q_ref[:, (g * HEADS_PER_KV + h) * SLOT:(g * HEADS_PER_KV + h + 1) * SLOT]
                 for h in range(HEADS_PER_KV)], axis=0) + sext_ref[g]
            return _dot_nt(ck_ref[:n_blocks, g * SLOT:(g + 1) * SLOT] + pext_ref[:n_blocks, :], qg)

        all_scores = [scores(g) for g in range(N_KV_HEADS)]
        for g in range(N_KV_HEADS):
            cs = slice(g * SLOT, (g + 1) * SLOT)
            s = jnp.where(ok, all_scores[g], NEG)
            e = jnp.where(ok, jnp.exp2(s - jnp.max(s, axis=0, keepdims=True)), 0.0)
            l = jnp.sum(e, axis=0, keepdims=True)
            p = e * jnp.where(l > 0.0, 1.0 / l, 0.0)
            o = _dot(cvt_ref[cs, :n_blocks], p.astype(BF16)).T
            for h in range(HEADS_PER_KV):
                c0 = (g * HEADS_PER_KV + h) * SLOT
                oc_ref[:, c0:c0 + SLOT] = o[h * tq:(h + 1) * tq].astype(oc_ref.dtype)
            psum = p[:, 0:tq]
            for h in range(1, HEADS_PER_KV):
                psum = psum + p[:, h * tq:(h + 1) * tq]
            hi = psum.astype(BF16)
            rem = psum - hi.astype(F32)
            mid = rem.astype(BF16)
            lo = (rem - mid.astype(F32)).astype(BF16)
            ovt = ovt_ref[:, :n_blocks]
            imp_scr[:, g * tq:(g + 1) * tq] = _dot(ovt, hi) + _dot(ovt, mid) + _dot(ovt, lo)

    half = nc // 2
    if half % LANES == 0:
        early = base + tq <= half * CMP_STRIDE

        @pl.when(early)
        def _():
            attend(half)

        @pl.when(jnp.logical_not(early))
        def _():
            attend(nc)
    else:
        attend(nc)
    imp = imp_scr[...]
    blk = lax.broadcasted_iota(jnp.int32, (ns, 1), 0)
    cur = pos >> SLC_SHIFT
    forced = (blk == 0) | (blk == cur) | (blk == cur - 1)
    imp = jnp.where(forced, FORCE, imp)
    imp = jnp.where(blk > cur, NEG, imp)
    blk_f = blk.astype(F32)
    sel = jnp.zeros(imp.shape, F32)
    for _ in range(n_sel):
        top = jnp.max(imp, axis=0, keepdims=True)
        first = jnp.min(jnp.where(imp == top, blk_f, float(ns)), axis=0, keepdims=True)
        hit = blk_f == first
        sel = jnp.where(hit, 1.0, sel)
        imp = jnp.where(hit, BELOW_NEG, imp)
    for g in range(N_KV_HEADS):
        sel_ref[:, g * ns:(g + 1) * ns] = sel[:, g * tq:(g + 1) * tq].T.astype(sel_ref.dtype)


def _cmp_select_t(q_slot, ck_slot, cvt_slot, overlap_t, tq, q_off, n_sel, name):
    m = q_slot.shape[0]
    nb, nc, _ = ck_slot.shape
    ns = overlap_t.shape[0]
    nq = m // (nb * tq)
    assert tq % LANES == 0 and (tq & (tq - 1)) == 0
    sext, pext = _alibi_lanes(np.arange(nc) * CMP_STRIDE + (CMP_LEN - 1), tq)
    return pl.pallas_call(
        functools.partial(_cmp_select_t_body, q_off=q_off, n_sel=n_sel),
        grid=(nb, nq),
        in_specs=[pl.BlockSpec((tq, N_HEADS * SLOT), lambda b, i: (b * nq + i, 0)),
                  pl.BlockSpec((None, nc, N_KV_HEADS * SLOT), lambda b, i: (b, 0, 0)),
                  pl.BlockSpec((None, N_KV_HEADS * SLOT, nc), lambda b, i: (b, 0, 0)),
                  _const_spec(overlap_t.shape), _const_spec(sext.shape), _const_spec(pext.shape)],
        out_specs=[pl.BlockSpec((tq, N_HEADS * SLOT), lambda b, i: (b * nq + i, 0)),
                   pl.BlockSpec((tq, N_KV_HEADS * ns), lambda b, i: (b * nq + i, 0))],
        out_shape=[jax.ShapeDtypeStruct((m, N_HEADS * SLOT), BF16),
                   jax.ShapeDtypeStruct((m, N_KV_HEADS * ns), BF16)],
        scratch_shapes=[pltpu.VMEM((ns, N_KV_HEADS * tq), F32)],
        compiler_params=_cparams(("parallel", "parallel")),
        name=name,
    )(q_slot, ck_slot, cvt_slot, overlap_t, sext, pext)


def _slc_body(q_ref, sel_ref, k_ref, vt_ref, sext_ref, pext_ref, hot_ref, o_ref,
              qa_scr, m_scr, acc_scr, *, tk):
    i = pl.program_id(1)
    tq = q_ref.shape[0]
    rows = HEADS_PER_KV * tq
    ns = hot_ref.shape[1]
    for g in range(N_KV_HEADS):
        qg = jnp.concatenate(
            [q_ref[:, (g * HEADS_PER_KV + h) * SLOT:(g * HEADS_PER_KV + h + 1) * SLOT]
             for h in range(HEADS_PER_KV)], axis=0) + sext_ref[g]
        bias = ((sel_ref[:, g * ns:(g + 1) * ns].astype(F32) - 1.0) * MASK_BIG).astype(BF16)
        qa_scr[g] = jnp.concatenate([qg, jnp.concatenate([bias] * HEADS_PER_KV, axis=0)], axis=1)
    m_scr[...] = jnp.full_like(m_scr, NEG)
    acc_scr[...] = jnp.zeros_like(acc_scr)

    def step(j, diagonal):
        k0 = pl.multiple_of(j * tk, tk)
        kext = pext_ref[pl.ds(k0, tk), :]
        hot = hot_ref[pl.ds(k0, tk), :]
        if diagonal:
            pos = i * tq + (lax.broadcasted_iota(jnp.int32, (1, rows), 1) & (tq - 1))
            causal = (k0 + lax.broadcasted_iota(jnp.int32, (tk, 1), 0)) <= pos
        ones_row = lax.broadcasted_iota(jnp.int32, (PV_ROWS, 1), 0) == SUM_ROW

        def scores(g):
            cs = slice(g * SLOT, (g + 1) * SLOT)
            ka = jnp.concatenate([k_ref[pl.ds(k0, tk), cs] + kext, hot], axis=1)
            s = _dot_nt(ka, qa_scr[g])
            return jnp.where(causal, s, NEG) if diagonal else s

        ahead = 3
        pending = [scores(g) for g in range(ahead)]
        for g in range(N_KV_HEADS):
            s = pending.pop(0)
            if g + ahead < N_KV_HEADS:
                pending.append(scores(g + ahead))
            cs = slice(g * SLOT, (g + 1) * SLOT)
            m_old = m_scr[g]
            m_new = jnp.maximum(m_old, jnp.max(s, axis=0, keepdims=True))
            p = jnp.exp2(s - m_new).astype(BF16)
            vt = jnp.where(ones_row, 1.0, vt_ref[j, g * SLOT:g * SLOT + PV_ROWS, :]).astype(BF16)
            acc_scr[g] = jnp.exp2(m_old - m_new) * acc_scr[g] + _dot(vt, p)
            m_scr[g] = m_new

    n_full = (i * tq) // tk

    def two_steps(jj, carry):
        step(2 * jj, False)
        step(2 * jj + 1, False)
        return carry

    lax.fori_loop(0, n_full // 2, two_steps, 0)

    @pl.when(n_full % 2 == 1)
    def _():
        step(n_full - 1, False)

    step(n_full, True)
    for g in range(N_KV_HEADS):
        acc = acc_scr[g]
        o = acc * (1.0 / acc[SUM_ROW:SUM_ROW + 1, :])
        o = jnp.concatenate([o, jnp.zeros((SLOT - PV_ROWS, rows), F32)], axis=0).T
        for h in range(HEADS_PER_KV):
            c0 = (g * HEADS_PER_KV + h) * SLOT
            o_ref[:, c0:c0 + SLOT] = o[h * tq:(h + 1) * tq].astype(o_ref.dtype)


def _slc_attention(q_slot, sel, k_slot, vt_tiles, nb, seq_len, tq, tk, name):
    nq = seq_len // tq
    nk = seq_len // tk
    rows = HEADS_PER_KV * tq
    ns = sel.shape[1] // N_KV_HEADS
    assert seq_len % tk == 0 and tk % tq == 0 and seq_len <= ns * SLC_BLOCK
    assert vt_tiles.shape == (nb * nk, N_KV_HEADS * SLOT, tk)
    sext, pext, hot = _slc_constants(seq_len, tq, ns)
    once = pl.Buffered(1)
    return pl.pallas_call(
        functools.partial(_slc_body, tk=tk),
        grid=(nb, nq),
        in_specs=[pl.BlockSpec((tq, N_HEADS * SLOT), lambda b, i: (b * nq + i, 0)),
                  pl.BlockSpec((tq, N_KV_HEADS * ns), lambda b, i: (b * nq + i, 0)),
                  pl.BlockSpec((seq_len, N_KV_HEADS * SLOT), lambda b, i: (b, 0), pipeline_mode=once),
                  pl.BlockSpec((nk, N_KV_HEADS * SLOT, tk), lambda b, i: (b, 0, 0), pipeline_mode=once),
                  pl.BlockSpec(sext.shape, lambda b, i: (0, 0, 0), pipeline_mode=once),
                  pl.BlockSpec(pext.shape, lambda b, i: (0, 0), pipeline_mode=once),
                  pl.BlockSpec(hot.shape, lambda b, i: (0, 0), pipeline_mode=once)],
        out_specs=pl.BlockSpec((tq, N_HEADS * SLOT), lambda b, i: (b * nq + i, 0)),
        out_shape=jax.ShapeDtypeStruct(q_slot.shape, BF16),
        scratch_shapes=[pltpu.VMEM((N_KV_HEADS, rows, SLOT + ns), BF16),
                        pltpu.VMEM((N_KV_HEADS, 1, rows), F32),
                        pltpu.VMEM((N_KV_HEADS, PV_ROWS, rows), F32)],
        compiler_params=_cparams(("parallel", "arbitrary")),
        name=name,
    )(q_slot, sel, k_slot, vt_tiles, sext, pext, hot)


def _win_body(q_ref, *refs, n_blk):
    k_refs = refs[:n_blk]
    vt_refs = refs[n_blk:2 * n_blk]
    sext_ref, pext_ref, band_ref, o_ref = refs[2 * n_blk:]
    tq = q_ref.shape[0]
    i = pl.program_id(1)
    n_keys = n_blk * tq
    key = lax.broadcasted_iota(jnp.int32, (n_keys, 1), 0)
    bias = band_ref[...] + jnp.where(key >= ((n_blk - 1) - i) * tq, 0.0, NEG)
    ones_row = lax.broadcasted_iota(jnp.int32, (SLOT, 1), 0) == SUM_ROW

    def scores(g):
        cs = slice(g * SLOT, (g + 1) * SLOT)
        qg = jnp.concatenate(
            [q_ref[:, (g * HEADS_PER_KV + h) * SLOT:(g * HEADS_PER_KV + h + 1) * SLOT]
             for h in range(HEADS_PER_KV)], axis=0) + sext_ref[g]
        kg = jnp.concatenate([k_refs[k][:, cs] for k in range(n_blk)], axis=0) + pext_ref[...]
        return _dot_nt(kg, qg) + bias

    all_scores = [scores(g) for g in range(N_KV_HEADS)]
    for g in range(N_KV_HEADS):
        cs = slice(g * SLOT, (g + 1) * SLOT)
        s = all_scores[g]
        p = jnp.exp2(s - jnp.max(s, axis=0, keepdims=True)).astype(BF16)
        vt = jnp.concatenate([vt_refs[k][cs, :] for k in range(n_blk)], axis=1)
        vt = jnp.where(ones_row, 1.0, vt).astype(BF16)
        acc = _dot(vt, p)
        o = (acc * (1.0 / acc[SUM_ROW:SUM_ROW + 1, :])).T
        for h in range(HEADS_PER_KV):
            c0 = (g * HEADS_PER_KV + h) * SLOT
            o_ref[:, c0:c0 + SLOT] = o[h * tq:(h + 1) * tq].astype(o_ref.dtype)


def _win_attention(q_slot, k_slot, vt_tiles, nb, seq_len, tq, name):
    nq = seq_len // tq
    n_blk = WINDOW // tq + 1
    n_keys = n_blk * tq
    sext, pext, _ = _slc_constants(n_keys, tq, LANES)
    dist = WINDOW + (np.arange(HEADS_PER_KV * tq) % tq)[None, :] - np.arange(n_keys)[:, None]
    band = jnp.asarray(np.where((dist >= 0) & (dist < WINDOW), 0.0, NEG).astype(np.float32))

    def block_map(k, transposed):
        def index(b, i):
            blk = b * nq + jnp.maximum(i - (n_blk - 1) + k, 0)
            return (blk, 0, 0) if transposed else (blk, 0)
        return index

    k_specs = [pl.BlockSpec((tq, N_KV_HEADS * SLOT), block_map(k, False)) for k in range(n_blk)]
    vt_specs = [pl.BlockSpec((None, N_KV_HEADS * SLOT, tq), block_map(k, True)) for k in range(n_blk)]
    return pl.pallas_call(
        functools.partial(_win_body, n_blk=n_blk),
        grid=(nb, nq),
        in_specs=[pl.BlockSpec((tq, N_HEADS * SLOT), lambda b, i: (b * nq + i, 0))]
        + k_specs + vt_specs + [_const_spec(sext.shape), _const_spec(pext.shape), _const_spec(band.shape)],
        out_specs=pl.BlockSpec((tq, N_HEADS * SLOT), lambda b, i: (b * nq + i, 0)),
        out_shape=jax.ShapeDtypeStruct(q_slot.shape, BF16),
        compiler_params=_cparams(("parallel", "parallel")),
        name=name,
    )(q_slot, *([k_slot] * n_blk), *([vt_tiles] * n_blk), sext, pext, band)


def _paged_attn_body(pt_ref, *refs, n_pages, n_steps, past_base, q_off, tokens, window, use_sel):
    page_refs = refs[:n_pages]
    refs = refs[n_pages:]
    new_ref, q_ref, slope_ref = refs[:3]
    refs = refs[3:]
    if use_sel:
        sel_ref = refs[0]
        refs = refs[1:]
    o_ref, m_scr, l_scr, acc_scr = refs
    s_id = pl.program_id(1)
    rows = q_ref.shape[0]
    half = N_KV_HEADS * HEAD_DIM

    @pl.when(s_id == 0)
    def _():
        m_scr[...] = jnp.full_like(m_scr, NEG)
        l_scr[...] = jnp.zeros_like(l_scr)
        acc_scr[...] = jnp.zeros_like(acc_scr)

    pos = q_off + (lax.broadcasted_iota(jnp.int32, (rows, 1), 0) & (tokens - 1))
    q = q_ref[...]
    key = lax.broadcasted_iota(jnp.int32, (1, PAGE), 1)

    def update(pages, kpos0s, blk0s):
        ss, keeps = [], []
        for ref, kpos0, blk0 in zip(pages, kpos0s, blk0s):
            d = pos - (kpos0 + key)
            valid = d >= 0
            if window is not None:
                valid = valid & (d < window)
            if use_sel:
                ns = sel_ref.shape[1]
                blk_row = lax.broadcasted_iota(jnp.int32, (ns, PAGE), 0)
                key_blk = blk0 + (lax.broadcasted_iota(jnp.int32, (ns, PAGE), 1) >> SLC_SHIFT)
                expand = jnp.where(blk_row == key_blk, 1.0, 0.0).astype(BF16)
                valid = valid & (_dot(sel_ref[...], expand) > 0.5)
            k_t = ref[0].reshape(half, PAGE).astype(BF16)
            s = _dot(q, k_t) - slope_ref[...] * d.astype(F32)
            ss.append(jnp.where(valid, s, NEG))
            keeps.append(jnp.where(valid, 1.0, 0.0))
        s = jnp.concatenate(ss, axis=1)
        keep = jnp.concatenate(keeps, axis=1)
        m_old = m_scr[...]
        m_new = jnp.maximum(m_old, jnp.max(s, axis=-1, keepdims=True))
        p = jnp.exp2(s - m_new) * keep
        alpha = jnp.exp2(m_old - m_new)
        l_scr[...] = alpha * l_scr[...] + jnp.sum(p, axis=-1, keepdims=True)
        p = p.astype(BF16)
        acc = alpha * acc_scr[...]
        for k, ref in enumerate(pages):
            v_t = ref[1].reshape(half, PAGE).astype(BF16)
            acc = acc + _dot_nt(p[:, k * PAGE:(k + 1) * PAGE], v_t)
        acc_scr[...] = acc
        m_scr[...] = m_new

    first_page = s_id * n_pages
    update(page_refs,
           [past_base + (first_page + k) * PAGE for k in range(n_pages)],
           [(first_page + k) * (PAGE // SLC_BLOCK) for k in range(n_pages)])

    @pl.when(s_id == n_steps - 1)
    def _():
        update([new_ref], [q_off], [(q_off - past_base) // SLC_BLOCK])
        o_ref[...] = acc_scr[...] * (1.0 / l_scr[...])


def _paged_attention(pages_t, page_index, n_p, new_t, q_bd, slope_rows, sel_rows, n_pages_step,
                     past_base, q_off, tokens, window, table, name):
    nb, rows, half = q_bd.shape
    assert n_p % n_pages_step == 0 and (tokens & (tokens - 1)) == 0
    n_steps = n_p // n_pages_step
    use_sel = sel_rows is not None
    page_block = (None,) * (pages_t.ndim - 4) + (2, N_KV_HEADS, HEAD_DIM, PAGE)

    def page_map(k):
        def index(b, s, pt):
            return page_index(b, s * n_pages_step + k, pt)
        return index

    in_specs = [pl.BlockSpec(page_block, page_map(k)) for k in range(n_pages_step)]
    in_specs += [pl.BlockSpec((None, 2, N_KV_HEADS, HEAD_DIM, PAGE), lambda b, s, pt: (b, 0, 0, 0, 0)),
                 pl.BlockSpec((None, rows, half), lambda b, s, pt: (b, 0, 0)),
                 pl.BlockSpec(slope_rows.shape, lambda b, s, pt: (0, 0))]
    args = [pages_t] * n_pages_step + [new_t, q_bd, slope_rows]
    if use_sel:
        in_specs.append(pl.BlockSpec((None,) + sel_rows.shape[1:], lambda b, s, pt: (b, 0, 0)))
        args.append(sel_rows)
    grid_spec = pltpu.PrefetchScalarGridSpec(
        num_scalar_prefetch=1, grid=(nb, n_steps), in_specs=in_specs,
        out_specs=pl.BlockSpec((None, rows, half), lambda b, s, pt: (b, 0, 0)),
        scratch_shapes=[pltpu.VMEM((rows, 1), F32), pltpu.VMEM((rows, 1), F32),
                        pltpu.VMEM((rows, half), F32)])
    body = functools.partial(_paged_attn_body, n_pages=n_pages_step, n_steps=n_steps,
                             past_base=past_base, q_off=q_off, tokens=tokens, window=window,
                             use_sel=use_sel)
    return pl.pallas_call(
        body,
        grid_spec=grid_spec,
        out_shape=jax.ShapeDtypeStruct((nb, rows, half), F32),
        compiler_params=_cparams(("parallel", "arbitrary")),
        name=name,
    )(table.reshape(-1), *args)


def _combine_body(x_ref, gates_ref, oc_ref, os_ref, ow_ref, wo_ref, o_ref, mix_scr):
    gt = jax.nn.sigmoid(gates_ref[...])
    for hh in range(N_HEADS):
        cs = slice(hh * SLOT, (hh + 1) * SLOT)
        mix = gt[:, hh:hh + 1] * oc_ref[:, cs].astype(F32)
        mix = mix + gt[:, N_HEADS + hh:N_HEADS + hh + 1] * os_ref[:, cs].astype(F32)
        mix = mix + gt[:, 2 * N_HEADS + hh:2 * N_HEADS + hh + 1] * ow_ref[:, cs].astype(F32)
        mix_scr[:, cs] = mix.astype(BF16)
    o_ref[...] = x_ref[...] + _dot(mix_scr[...], wo_ref[...])


def _combine(x, gates, o_c, o_s, o_w, w_out_slot, tm, name):
    m, d = x.shape
    wide = N_HEADS * SLOT
    row_spec = lambda w: pl.BlockSpec((tm, w), lambda i: (i, 0))
    return pl.pallas_call(
        _combine_body,
        grid=(m // tm,),
        in_specs=[row_spec(d), row_spec(gates.shape[1]), row_spec(wide), row_spec(wide), row_spec(wide),
                  _const_spec(w_out_slot.shape)],
        out_specs=row_spec(d),
        out_shape=jax.ShapeDtypeStruct((m, d), F32),
        scratch_shapes=[pltpu.VMEM((tm, wide), BF16)],
        compiler_params=_cparams(("parallel",)),
        name=name,
    )(x, gates, o_c, o_s, o_w, w_out_slot)


def _slot_cols(w):
    d, n = w.shape
    w = w.reshape(d, n // HEAD_DIM, HEAD_DIM)
    return jnp.pad(w, ((0, 0), (0, 0), (0, SLOT - HEAD_DIM))).reshape(d, (n // HEAD_DIM) * SLOT)


def _slot_rows(w):
    n, d = w.shape
    w = w.reshape(n // HEAD_DIM, HEAD_DIM, d)
    return jnp.pad(w, ((0, 0), (0, SLOT - HEAD_DIM), (0, 0))).reshape((n // HEAD_DIM) * SLOT, d)


def _slot_last(a):
    lead = a.shape[:-1]
    n = a.shape[-1] // HEAD_DIM
    a = a.reshape(lead + (n, HEAD_DIM))
    a = jnp.pad(a, [(0, 0)] * len(lead) + [(0, 0), (0, SLOT - HEAD_DIM)])
    return a.reshape(lead + (n * SLOT,))


def _alibi_slopes():
    h = np.arange(1, N_HEADS + 1, dtype=np.float32)
    slopes = np.asarray(2.0 ** (-8.0 * h / N_HEADS), dtype=np.float32)
    return (slopes * np.float32(LOG2E)).reshape(N_KV_HEADS, HEADS_PER_KV)


def _bf16_parts(x):
    x = np.asarray(x, np.float32)
    parts = []
    for _ in range(3):
        part = x.astype(jnp.bfloat16)
        parts.append(part)
        x = x - part.astype(np.float32)
    assert not np.any(x)
    return parts


def _alibi_lanes(kpos, tq):
    kpos = np.asarray(kpos)
    assert kpos.min() >= 0 and kpos.max() < 256 * LANES
    parts = _bf16_parts(np.repeat(_alibi_slopes(), tq, axis=1))
    sext = np.zeros((N_KV_HEADS, HEADS_PER_KV * tq, SLOT), jnp.bfloat16)
    pext = np.zeros((len(kpos), SLOT), np.float32)
    for n, part in enumerate(parts):
        sext[:, :, ALIBI_LANE + 2 * n] = part
        sext[:, :, ALIBI_LANE + 2 * n + 1] = part
        pext[:, ALIBI_LANE + 2 * n] = (kpos // LANES) * LANES
        pext[:, ALIBI_LANE + 2 * n + 1] = kpos % LANES
    return jnp.asarray(sext), jnp.asarray(pext, dtype=BF16)


def _slc_constants(seq_len, tq, ns):
    kpos = np.arange(seq_len)
    sext, pext = _alibi_lanes(kpos, tq)
    hot = (kpos[:, None] // SLC_BLOCK) == np.arange(ns)[None, :]
    return sext, pext, jnp.asarray(hot.astype(np.float32), dtype=BF16)


def _overlap(n_cmp, n_slc, nc, ns):
    st = np.arange(nc) * CMP_STRIDE
    bs = np.arange(ns) * SLC_BLOCK
    m = (st[:, None] < bs[None, :] + SLC_BLOCK) & (st[:, None] + CMP_LEN > bs[None, :])
    m = m & (np.arange(nc)[:, None] < n_cmp) & (np.arange(ns)[None, :] < n_slc)
    return jnp.asarray(m.astype(np.float32), dtype=BF16)


def _compress_weights(cmp_pe, cmp_w):
    eye = jnp.eye(N_KV_HEADS, dtype=F32)
    w = cmp_w.reshape(2, 2, CMP_STRIDE, HEAD_DIM, HEAD_DIM)
    bd = jnp.einsum('kmrde,gh->rkgdmhe', w, eye)
    half = N_KV_HEADS * HEAD_DIM
    w_big = bd.reshape(CMP_STRIDE, 2, half, 2 * half).astype(BF16)
    pe = cmp_pe.reshape(2, 2, CMP_STRIDE, HEAD_DIM)
    pe = jnp.broadcast_to(pe.transpose(1, 2, 0, 3)[:, :, :, None, :],
                          (2, CMP_STRIDE, 2, N_KV_HEADS, HEAD_DIM)).reshape(2, SEG_W)
    pe_seg = jnp.pad(pe, ((0, SUBLANES - 2), (0, 0)))
    return pe_seg, w_big


def _nsa_weights(w_in, segs_slot_kv):
    q_dim = N_HEADS * HEAD_DIM
    half = N_KV_HEADS * HEAD_DIM
    w_q = w_in[:, :q_dim] * (HEAD_DIM ** -0.5 * LOG2E)
    w_gate = w_in[:, q_dim + N_BRANCH * KV_ROW:]
    w_kv = w_in[:, q_dim:q_dim + N_BRANCH * KV_ROW]
    w_gate = jnp.pad(w_gate, ((0, 0), (0, LANES - w_gate.shape[1])))
    if not segs_slot_kv:
        segs = [(N_HEADS * SLOT, BF16, False), (KV_ROW, F32, False), (KV_ROW, F32, False),
                (KV_ROW, F32, False), (LANES, F32, False)]
        return jnp.concatenate([_slot_cols(w_q), w_kv, w_gate], axis=1).astype(BF16), segs, None
    o_slc = q_dim + KV_ROW
    o_win = q_dim + 2 * KV_ROW
    parts = [_slot_cols(w_q), w_gate,
             _slot_cols(w_in[:, o_slc:o_slc + half]), _slot_cols(w_in[:, o_win:o_win + half])]
    segs = [(N_HEADS * SLOT, BF16, False), (LANES, F32, False)] + [(N_KV_HEADS * SLOT, BF16, False)] * 2
    w_t = jnp.concatenate([_slot_cols(w_in[:, o_slc + half:o_slc + KV_ROW]),
                           _slot_cols(w_in[:, o_win + half:o_win + KV_ROW]), w_kv], axis=1).T.astype(BF16)
    return jnp.concatenate(parts, axis=1).astype(BF16), segs, w_t


def _sgu_layer(x, p, a, seq_len, tm, emit_v, tag):
    d = x.shape[1]
    d_sgu = p['sgu_w_out'].shape[1]
    gw = d_sgu // N_SGU_GROUPS
    segs = [(d_sgu, BF16, True), (d_sgu, BF16, True)]
    u, v = _proj(x, p['norm_mix'][0], p['sgu_w_in'][a].astype(BF16), p['sgu_b_in'][a].reshape(1, -1),
                 segs, tm, f"sgu_in_{tag}")
    w_s, b_s = p['sgu_w_s'][a], p['sgu_b_s'][a]
    if seq_len % CHUNK == 0:
        bs_rows = b_s.T
    else:
        assert CHUNK % seq_len == 0
        rep = CHUNK // seq_len
        w_s = jnp.einsum('ab,gts->gatbs', jnp.eye(rep, dtype=F32), w_s[:, :seq_len, :seq_len])
        w_s = w_s.reshape(N_SGU_GROUPS, CHUNK, CHUNK)
        bs_rows = jnp.tile(b_s[:, :seq_len].T, (rep, 1))
    bs_exp = jnp.repeat(bs_rows, gw, axis=1)
    return _sgu(x, u, v, p['sgu_v_norm'][a], w_s, bs_exp, p['sgu_w_out'][a].astype(BF16), tm, emit_v,
                f"sgu_{tag}")


def _ffn_layer(x, p, layer, seq_len, past, final, tm, tag):
    gamma_final = p['norm_final'] if final else None
    return _ffn(x, p['norm_ffn'][layer], p['ffn_w_in'][layer].astype(BF16), p['ffn_conv_w'][layer],
                p['ffn_conv_b'][layer], p['ffn_w_out'][layer].astype(BF16), seq_len, past, gamma_final,
                tm, 256, f"ffn{layer}_{tag}")


def _nsa_prompt(x, p, b_idx, nb, seq_len, tm):
    w_ext, segs, w_t = _nsa_weights(p['nsa_w_in'][b_idx], True)
    tq = 128
    wide = N_KV_HEADS * SLOT
    q_slot, gates, ks_slot, kw_slot, vs_tiles, vw_tiles, cmp_t, slc_t, win_t = _proj(
        x, p['norm_mix'][1], w_ext, None, segs, tm, "nsa_in_prompt", w_t,
        t_outs=((0, wide, tm, None), (wide, wide, tq, None)) + tuple(
            (2 * wide + br * KV_ROW, KV_ROW, tm, seq_len) for br in range(N_BRANCH)))
    rows_last = (nb, 2, N_KV_HEADS, HEAD_DIM, seq_len)
    cmp_t, slc_t, win_t = (a.reshape(rows_last) for a in (cmp_t, slc_t, win_t))
    n_pages = seq_len // PAGE
    pe_seg, w_big = _compress_weights(p['nsa_cmp_pe'][b_idx], p['nsa_cmp_w'][b_idx])
    ck, cv = _compress_paged(cmp_t, lambda b, pg, pt: (b, 0, 0, 0, pg), nb, n_pages,
                             jnp.zeros((1,), jnp.int32), pe_seg, w_big, min(n_pages, 16), "compress_prompt")
    n_seg = seq_len // CMP_STRIDE
    n_cmp = n_seg - CMP_LEN // CMP_STRIDE + 1
    n_slc = -(-seq_len // SLC_BLOCK)
    ns = -(-n_slc // LANES) * LANES
    o_c, sel = _cmp_select_t(q_slot, _slot_last(ck).astype(BF16),
                             _slot_last(cv).astype(BF16).transpose(0, 2, 1),
                             _overlap(n_cmp, n_slc, n_seg, ns).T, tq, 0,
                             min(SLC_TOP_N, n_slc), "cmp_select_prompt")
    o_s = _slc_attention(q_slot, sel, ks_slot, vs_tiles, nb, seq_len, tq, tm, "slc_prompt")
    o_w = _win_attention(q_slot, kw_slot, vw_tiles, nb, seq_len, tq, "win_prompt")
    x = _combine(x, gates, o_c, o_s, o_w, _slot_rows(p['nsa_w_out'][b_idx]).astype(BF16), tm,
                 "nsa_out_prompt")
    rows_first = (0, 4, 1, 2, 3)
    win_t = win_t[..., seq_len - min(WINDOW, seq_len):]
    return x, cmp_t.transpose(rows_first), slc_t.transpose(rows_first), win_t.transpose(rows_first)


def _rows_ght(a, tokens_pad):
    nb, t, _, w = a.shape
    a = a.reshape(nb, t, N_KV_HEADS, HEADS_PER_KV, w).transpose(0, 2, 3, 1, 4)
    a = jnp.pad(a, ((0, 0), (0, 0), (0, 0), (0, tokens_pad - t), (0, 0)))
    return a.reshape(nb, N_HEADS * tokens_pad, w)


def _nsa_sample(x, p, b_idx, nb, t_new, past_len, cache_cmp, cache_slc, state_win, page_table):
    m = x.shape[0]
    half = N_KV_HEADS * HEAD_DIM
    t_pad = BF16_SUBLANES
    assert t_new <= t_pad and past_len % PAGE == 0
    w_ext, segs, _ = _nsa_weights(p['nsa_w_in'][b_idx], False)
    q_slot, kv_cmp, kv_slc, kv_win, gates = _proj(x, p['norm_mix'][1], w_ext, None, segs, m, "nsa_in_sample")
    n_pages = past_len // PAGE
    total = past_len + t_new
    n_seg = total // CMP_STRIDE
    assert n_seg == past_len // CMP_STRIDE, "new rows must not complete a compression segment"
    n_cmp = n_seg - CMP_LEN // CMP_STRIDE + 1
    pe_seg, w_big = _compress_weights(p['nsa_cmp_pe'][b_idx], p['nsa_cmp_w'][b_idx])
    rows_last = (0, 2, 3, 4, 1)
    ck, cv = _compress_paged(cache_cmp.transpose(rows_last), lambda b, pg, pt: (pt[b * n_pages + pg], 0, 0, 0, 0),
                             nb, n_pages, page_table, pe_seg, w_big, min(n_pages, 32), "compress_sample")
    n_slc = -(-total // SLC_BLOCK)
    ns = -(-n_slc // LANES) * LANES
    assert (t_new & (t_new - 1)) == 0
    qh = q_slot.reshape(nb, t_new, N_HEADS, SLOT)[..., :HEAD_DIM]

    def block_diag_rows(tokens):
        q_rows = _rows_ght(qh, tokens).reshape(nb, N_KV_HEADS, HEADS_PER_KV * tokens, HEAD_DIM)
        q_bd = jnp.einsum('bgrd,gk->bgrkd', q_rows, jnp.eye(N_KV_HEADS, dtype=BF16))
        slopes = jnp.asarray(np.repeat(_alibi_slopes().reshape(-1), tokens)[:, None])
        return q_bd.reshape(nb, N_HEADS * tokens, half), slopes

    def own_lanes(o, tokens):
        o = o.reshape(nb, N_KV_HEADS, HEADS_PER_KV, tokens, N_KV_HEADS, HEAD_DIM)
        o = jnp.stack([o[:, g, :, :t_new, g, :] for g in range(N_KV_HEADS)], axis=1)
        o = o.transpose(0, 3, 1, 2, 4).reshape(m, N_HEADS * HEAD_DIM)
        return _slot_last(o).astype(BF16)

    q_bd_pad, slope_rows_pad = block_diag_rows(t_pad)
    o_c, sel = _cmp_select(q_bd_pad, ck, cv, _overlap(n_cmp, n_slc, n_seg, ns), slope_rows_pad, t_pad,
                           past_len, min(SLC_TOP_N, n_slc), "cmp_select_sample")
    o_c = own_lanes(o_c, t_pad)
    q_bd, slope_rows = block_diag_rows(t_new)
    sel_rows = sel.reshape(nb, N_KV_HEADS, 1, t_pad, ns)[:, :, :, :t_new]
    sel_rows = jnp.broadcast_to(sel_rows, (nb, N_KV_HEADS, HEADS_PER_KV, t_new, ns))
    sel_rows = sel_rows.reshape(nb, N_HEADS * t_new, ns)

    def new_page(kv):
        kv = kv.reshape(nb, t_new, 2, N_KV_HEADS, HEAD_DIM).transpose(rows_last)
        return jnp.pad(kv, ((0, 0),) * 4 + ((0, PAGE - t_new),))

    o_s = _paged_attention(cache_slc.transpose(rows_last), lambda b, pg, pt: (pt[b * n_pages + pg], 0, 0, 0, 0),
                           n_pages, new_page(kv_slc), q_bd, slope_rows, sel_rows, min(n_pages, 16),
                           0, past_len, t_new, None, page_table, "slc_sample")
    n_win = state_win.shape[1]
    assert n_win % PAGE == 0
    win_pages = n_win // PAGE
    o_w = _paged_attention(state_win.transpose(rows_last), lambda b, pg, pt: (b, 0, 0, 0, pg),
                           win_pages, new_page(kv_win), q_bd, slope_rows, None, win_pages,
                           past_len - n_win, past_len, t_new, WINDOW, page_table, "win_sample")

    x = _combine(x, gates, o_c, own_lanes(o_s, t_new), own_lanes(o_w, t_new),
                 _slot_rows(p['nsa_w_out'][b_idx]).astype(BF16), m, "nsa_out_sample")
    kv_shape = (nb, t_new, 2, N_KV_HEADS, HEAD_DIM)
    kv_win5 = kv_win.reshape(kv_shape)
    all_win = jnp.concatenate([state_win.reshape((nb, n_win) + kv_shape[2:]), kv_win5], axis=1)
    return x, kv_cmp.reshape(kv_shape), kv_slc.reshape(kv_shape), all_win[:, -min(WINDOW, n_win + t_new):]


def _run_prompt(x_prompt, p):
    nb, seq_len, d = x_prompt.shape
    x = x_prompt.reshape(nb * seq_len, d)
    tm = min(512, seq_len)
    tm_ffn = min(1024, seq_len)
    x = _sgu_layer(x, p, 0, seq_len, tm, False, "prompt")[0]
    x, h0 = _ffn_layer(x, p, 0, seq_len, None, False, tm_ffn, "prompt")
    x, kc, ks, kw = _nsa_prompt(x, p, 0, nb, seq_len, tm)
    x, h1 = _ffn_layer(x, p, 1, seq_len, None, True, tm_ffn, "prompt")
    tiles = seq_len // tm_ffn

    def conv_state(h):
        return h.reshape(nb, tiles, SUBLANES, -1)[:, -1, SUBLANES - (CONV_W - 1):]

    return (x.reshape(nb, seq_len, d), kc[None], ks[None], kw[None],
            jnp.stack([conv_state(h0), conv_state(h1)]))


def _run_sample(x_sample, cache_cmp, cache_slc, state_win, state_conv, page_table, p):
    nb, t_new, d = x_sample.shape
    m = nb * t_new
    past_len = page_table.shape[1] * PAGE
    x = x_sample.reshape(m, d)
    d_ff = state_conv.shape[-1]

    def conv_past(layer):
        st = state_conv[layer]
        zeros = jnp.zeros((nb, t_new, d_ff), F32)
        prev1 = zeros.at[:, 0].set(st[:, 1])
        prev2 = zeros.at[:, 0].set(st[:, 0]).at[:, 1].set(st[:, 1])
        return prev1.reshape(m, d_ff), prev2.reshape(m, d_ff)

    def conv_state(h):
        return h.reshape(nb, t_new, d_ff)[:, -(CONV_W - 1):]

    x, vn = _sgu_layer(x, p, 0, t_new, m, True, "sample")
    x, h0 = _ffn_layer(x, p, 0, t_new, conv_past(0), False, m, "sample")
    x, kc, ks, kw = _nsa_sample(x, p, 0, nb, t_new, past_len, cache_cmp[0], cache_slc[0], state_win[0],
                                page_table)
    x, h1 = _ffn_layer(x, p, 1, t_new, conv_past(1), True, m, "sample")
    return (x.reshape(nb, t_new, d), kc[None], ks[None], kw[None],
            jnp.stack([conv_state(h0), conv_state(h1)]), vn.reshape(nb, t_new, -1)[None])


def kernel(x_prompt, x_sample, cache_cmp_kv, cache_slc_kv, state_win_kv, state_ffn_conv, page_table,
           norm_mix, norm_ffn, norm_final, sgu_w_in, sgu_b_in, sgu_v_norm, sgu_w_s, sgu_b_s, sgu_w_out,
           nsa_w_in, nsa_cmp_pe, nsa_cmp_w, nsa_w_out, ffn_w_in, ffn_conv_w, ffn_conv_b, ffn_w_out):
    p = {'norm_mix': norm_mix, 'norm_ffn': norm_ffn, 'norm_final': norm_final,
         'sgu_w_in': sgu_w_in, 'sgu_b_in': sgu_b_in, 'sgu_v_norm': sgu_v_norm, 'sgu_w_s': sgu_w_s,
         'sgu_b_s': sgu_b_s, 'sgu_w_out': sgu_w_out, 'nsa_w_in': nsa_w_in, 'nsa_cmp_pe': nsa_cmp_pe,
         'nsa_cmp_w': nsa_cmp_w, 'nsa_w_out': nsa_w_out, 'ffn_w_in': ffn_w_in, 'ffn_conv_w': ffn_conv_w,
         'ffn_conv_b': ffn_conv_b, 'ffn_w_out': ffn_w_out}
    y_p, p_cmp, p_slc, p_win, p_conv = _run_prompt(x_prompt, p)
    y_s, s_cmp, s_slc, s_win, s_conv, s_v = _run_sample(
        x_sample, cache_cmp_kv, cache_slc_kv, state_win_kv, state_ffn_conv, page_table, p)
    return (y_p, y_s, p_cmp, p_slc, p_win, p_conv, s_cmp, s_slc, s_win, s_conv, s_v)
```
